```python
import jax, jax.numpy as jnp
from jax import lax
import numpy as np


D_MODEL = 1024
BATCH = 4
SEQ = 8192
DEPTH = 1

CHUNK = 64
ROPE_THETA = 10000.0
LN_EPS = 1e-5
DSA_WIDTH = D_MODEL // 2
DSA_HEAD_DIM = 64
DSA_HEADS = DSA_WIDTH // DSA_HEAD_DIM
IDX_HEADS = 8
IDX_DIM = 32
IDX_SCALE = (IDX_HEADS * IDX_DIM) ** -0.5
DSA_TOPK_MAX = 256
Q_BLOCK = CHUNK
GLA_WIDTH = D_MODEL - DSA_WIDTH
GLA_HEADS = 4
GLA_DV = GLA_WIDTH // GLA_HEADS
GLA_DK = GLA_DV // 2
GLA_GATE_RANK = 16
GLA_TAU = 16.0
N_GROUPS = 4
EXPERTS_PER_GROUP = 8
N_EXPERTS = N_GROUPS * EXPERTS_PER_GROUP
TOP_K_INNER = 2
D_EXPERT = 512
DEEPNORM_ALPHA = (2.0 * DEPTH) ** 0.25
DEEPNORM_BETA = (8.0 * DEPTH) ** -0.25
IN_SIZES = (DSA_WIDTH, DSA_WIDTH, DSA_WIDTH, IDX_HEADS * IDX_DIM, IDX_DIM, IDX_HEADS,
            GLA_HEADS * GLA_DK, GLA_HEADS * GLA_DK, GLA_WIDTH, GLA_WIDTH, GLA_GATE_RANK)
IN_WIDTH = sum(IN_SIZES)

kernel_name = 'hybrid_dsa_gla_hmoe_block'


def layer_norm(x, g, b):
    xf = x.astype(jnp.float32)
    mu = jnp.mean(xf, axis=-1, keepdims=True)
    var = jnp.mean(jnp.square(xf - mu), axis=-1, keepdims=True)
    return ((xf - mu) * lax.rsqrt(var + LN_EPS) * g + b).astype(x.dtype)


def rope(x, pos):
    d = x.shape[-1]
    half = d // 2
    inv = 1.0 / (ROPE_THETA ** (jnp.arange(half, dtype=jnp.float32) / half))
    ang = pos.astype(jnp.float32)[:, None] * inv[None, :]
    cos = jnp.cos(ang)[:, None, :].astype(x.dtype)
    sin = jnp.sin(ang)[:, None, :].astype(x.dtype)
    x1, x2 = x[..., :half], x[..., half:]
    return jnp.concatenate([x1 * cos - x2 * sin, x2 * cos + x1 * sin], axis=-1)


def dsa_mixer(q, k, v, iq, ik, iw):
    B, T, H, dh = q.shape
    topk = min(DSA_TOPK_MAX, T // 4)
    nb = T // Q_BLOCK
    key_chunk = jnp.arange(T) // CHUNK

    def block(args):
        qb, iqb, iwb, start = args
        qpos = start + jnp.arange(Q_BLOCK)
        adm = key_chunk[None, :] <= (qpos // CHUNK)[:, None]
        rel = jax.nn.relu(jnp.einsum('bqhd,bsd->bqhs', iqb, ik))
        score = jnp.einsum('bqhs,bqh->bqs', rel, iwb).astype(jnp.float32)
        score = jnp.where(adm[None], score, -jnp.inf)
        vals, idx = lax.top_k(score, topk)
        ksel = jax.vmap(lambda kb, ib: kb[ib])(k, idx)
        vsel = jax.vmap(lambda vb, ib: vb[ib])(v, idx)
        s = jnp.einsum('bqhd,bqkhd->bhqk', qb, ksel).astype(jnp.float32) * (dh ** -0.5)
        s = jnp.where(jnp.isfinite(vals)[:, None], s, -jnp.inf)
        p = jax.nn.softmax(s, axis=-1).astype(v.dtype)
        return jnp.einsum('bhqk,bqkhd->bqhd', p, vsel)

    qbs = q.reshape(B, nb, Q_BLOCK, H, dh).transpose(1, 0, 2, 3, 4)
    iqbs = iq.reshape(B, nb, Q_BLOCK, IDX_HEADS, IDX_DIM).transpose(1, 0, 2, 3, 4)
    iwbs = iw.reshape(B, nb, Q_BLOCK, IDX_HEADS).transpose(1, 0, 2, 3)
    starts = jnp.arange(nb, dtype=jnp.int32) * Q_BLOCK
    out = lax.map(block, (qbs, iqbs, iwbs, starts))
    return out.transpose(1, 0, 2, 3, 4).reshape(B, T, H * dh)


def gla_mixer(q, k, v, log_a):
    B, T, H, dk = q.shape
    dv = v.shape[-1]
    N, C = T // CHUNK, CHUNK
    qf = q.astype(jnp.float32).reshape(B, N, C, H, dk) * (dk ** -0.5)
    kf = k.astype(jnp.float32).reshape(B, N, C, H, dk)
    vf = v.astype(jnp.float32).reshape(B, N, C, H, dv)
    b = jnp.cumsum(log_a.reshape(B, N, C, H, dk), axis=2)
    qg = qf * jnp.exp(b)
    kg = kf * jnp.exp(-b)
    causal = jnp.tril(jnp.ones((C, C), dtype=bool))
    A = jnp.where(causal, jnp.einsum('bnihd,bnjhd->bnhij', qg, kg), 0.0)
    o_intra = jnp.einsum('bnhij,bnjhv->bnihv', A, vf)
    b_last = b[:, :, -1]
    kd = kf * jnp.exp(b_last[:, :, None] - b)
    kv = jnp.einsum('bnjhd,bnjhv->bnhdv', kd, vf)
    decay = jnp.exp(b_last)

    def step(S, inp):
        dec, kvn = inp
        return dec[..., None] * S + kvn, S

    S0 = jnp.zeros((B, H, dk, dv), jnp.float32)
    _, S_prev = lax.scan(step, S0, (decay.transpose(1, 0, 2, 3), kv.transpose(1, 0, 2, 3, 4)))
    S_prev = S_prev.transpose(1, 0, 2, 3, 4)
    o_inter = jnp.einsum('bnihd,bnhdv->bnihv', qg, S_prev)
    return (o_intra + o_inter).reshape(B, T, H, dv)


def hier_moe(x, w_gr, b_gr, w_er, b_er, w_e_in, w_e_out):
    B, T, D = x.shape
    xt = x.reshape(-1, D)
    Ntok = xt.shape[0]
    glog = (xt @ w_gr + b_gr).astype(jnp.float32)
    gsel = jnp.argmax(glog, axis=-1)
    pg = jnp.take_along_axis(jax.nn.softmax(glog, axis=-1), gsel[:, None], axis=-1)
    elog = (xt @ w_er + b_er).astype(jnp.float32).reshape(Ntok, N_GROUPS, EXPERTS_PER_GROUP)
    elog_g = jnp.take_along_axis(elog, gsel[:, None, None], axis=1)[:, 0]
    tv, te = lax.top_k(elog_g, TOP_K_INNER)
    gate = jax.nn.softmax(tv, axis=-1) * pg
    eid = (gsel[:, None] * EXPERTS_PER_GROUP + te).reshape(-1)
    tok = jnp.repeat(jnp.arange(Ntok), TOP_K_INNER)
    order = jnp.argsort(eid)
    tok_s = tok[order]
    gate_s = gate.reshape(-1)[order]
    sizes = jnp.bincount(eid, length=N_EXPERTS).astype(jnp.int32)
    xs = xt[tok_s]
    h = lax.ragged_dot(xs, w_e_in, sizes)
    hg, hu = jnp.split(h, 2, axis=-1)
    y = lax.ragged_dot(jax.nn.silu(hg) * hu, w_e_out, sizes)
    out = jnp.zeros_like(xt).at[tok_s].add(y * gate_s[:, None].astype(y.dtype))
    return out.reshape(B, T, D)


def setup_inputs(seed: int = 0) -> dict:
    key = jax.random.key(seed)
    ks = jax.random.split(key, 16)

    def nrm(k, shape, scale):
        return jax.random.normal(k, shape, jnp.float32) * scale

    return {
        'x': nrm(ks[0], (BATCH, SEQ, D_MODEL), 1.0),
        'w_in': nrm(ks[1], (DEPTH, D_MODEL, IN_WIDTH), D_MODEL ** -0.5),
        'w_gla_gate': nrm(ks[2], (DEPTH, GLA_GATE_RANK, GLA_HEADS * GLA_DK), GLA_GATE_RANK ** -0.5),
        'b_gla_gate': nrm(ks[3], (DEPTH, GLA_HEADS * GLA_DK), 0.1),
        'g_gla_norm': 1.0 + nrm(ks[4], (DEPTH, GLA_DV), 0.02),
        'w_out': nrm(ks[5], (DEPTH, D_MODEL, D_MODEL), D_MODEL ** -0.5 * DEEPNORM_BETA),
        'ln1_g': 1.0 + nrm(ks[6], (DEPTH, D_MODEL), 0.02),
        'ln1_b': nrm(ks[7], (DEPTH, D_MODEL), 0.02),
        'w_group_router': nrm(ks[8], (DEPTH, D_MODEL, N_GROUPS), D_MODEL ** -0.5),
        'b_group_router': nrm(ks[9], (DEPTH, N_GROUPS), 0.01),
        'w_expert_router': nrm(ks[10], (DEPTH, D_MODEL, N_EXPERTS), D_MODEL ** -0.5),
        'b_expert_router': nrm(ks[11], (DEPTH, N_EXPERTS), 0.01),
        'w_expert_in': nrm(ks[12], (DEPTH, N_EXPERTS, D_MODEL, 2 * D_EXPERT), D_MODEL ** -0.5),
        'w_expert_out': nrm(ks[13], (DEPTH, N_EXPERTS, D_EXPERT, D_MODEL), D_EXPERT ** -0.5 * DEEPNORM_BETA),
        'ln2_g': 1.0 + nrm(ks[14], (DEPTH, D_MODEL), 0.02),
        'ln2_b': nrm(ks[15], (DEPTH, D_MODEL), 0.02),
    }


def reference(x, w_in, w_gla_gate, b_gla_gate, g_gla_norm, w_out, ln1_g, ln1_b,
              w_group_router, b_group_router, w_expert_router, b_expert_router,
              w_expert_in, w_expert_out, ln2_g, ln2_b):
    B, T, D = x.shape
    pos = jnp.arange(T)
    split_points = np.cumsum(np.array(IN_SIZES))[:-1].tolist()
    h = x
    for l in range(DEPTH):
        proj = h @ w_in[l]
        aq, ak, av, iq, ik, iw, bq, bk, bv, br, bg = jnp.split(proj, split_points, axis=-1)
        aq = rope(aq.reshape(B, T, DSA_HEADS, DSA_HEAD_DIM), pos)
        ak = rope(ak.reshape(B, T, DSA_HEADS, DSA_HEAD_DIM), pos)
        av = av.reshape(B, T, DSA_HEADS, DSA_HEAD_DIM)
        iq = rope(iq.reshape(B, T, IDX_HEADS, IDX_DIM), pos)
        ik = rope(ik[:, :, None, :], pos)[:, :, 0]
        iw = iw * IDX_SCALE
        ya = dsa_mixer(aq, ak, av, iq, ik, iw)
        log_a = jax.nn.log_sigmoid((bg @ w_gla_gate[l] + b_gla_gate[l]).astype(jnp.float32)) / GLA_TAU
        ob = gla_mixer(bq.reshape(B, T, GLA_HEADS, GLA_DK), bk.reshape(B, T, GLA_HEADS, GLA_DK),
                       bv.reshape(B, T, GLA_HEADS, GLA_DV), log_a.reshape(B, T, GLA_HEADS, GLA_DK))
        ob = ob * lax.rsqrt(jnp.mean(jnp.square(ob), axis=-1, keepdims=True) + LN_EPS) * g_gla_norm[l]
        yb = (ob.reshape(B, T, GLA_WIDTH) * jax.nn.silu(br.astype(jnp.float32))).astype(h.dtype)
        mix = jnp.concatenate([ya, yb], axis=-1) @ w_out[l]
        h = layer_norm(DEEPNORM_ALPHA * h + mix, ln1_g[l], ln1_b[l])
        ffn = hier_moe(h, w_group_router[l], b_group_router[l], w_expert_router[l], b_expert_router[l],
                       w_expert_in[l], w_expert_out[l])
        h = layer_norm(DEEPNORM_ALPHA * h + ffn, ln2_g[l], ln2_b[l])
    return h
```

```python
import functools

import numpy as np
import jax
import jax.numpy as jnp
from jax import lax
from jax.experimental import pallas as pl
from jax.experimental.pallas import tpu as pltpu

F32 = jnp.float32
BF16 = jnp.bfloat16

D_MODEL = 1024
CHUNK = 64
ROPE_THETA = 10000.0
LN_EPS = 1e-5
DSA_WIDTH = 512
DSA_HEAD_DIM = 64
DSA_HEADS = 8
IDX_HEADS = 8
IDX_DIM = 32
IDX_SCALE = (IDX_HEADS * IDX_DIM) ** -0.5
DSA_TOPK_MAX = 256
GLA_WIDTH = 512
GLA_HEADS = 4
GLA_DV = 128
GLA_DK = 64
GLA_GATE_RANK = 16
GLA_TAU = 16.0
N_GROUPS = 4
EXPERTS_PER_GROUP = 8
N_EXPERTS = 32
D_EXPERT = 512
DEEPNORM_ALPHA = 2.0 ** 0.25

LANES = 128
NEG_BIG = -1e30
VMEM_LIMIT = 56 * 1024 * 1024

TOK_TILE = 256
DSA_TILE = 256
GLA_BLOCK = 256
MOE_TILE = 256
RANK_TILE = 512
BISECT_CAP = 400


def _dot(a, b):
    return jnp.dot(a, b, preferred_element_type=F32)


def _dot_nt(a, b):
    return lax.dot_general(a, b, (((1,), (1,)), ((), ())), preferred_element_type=F32)


def _dot_tn(a, b):
    return lax.dot_general(a, b, (((0,), (0,)), ((), ())), preferred_element_type=F32)


def _params(sem):
    return pltpu.CompilerParams(dimension_semantics=sem, vmem_limit_bytes=VMEM_LIMIT)


def _rope_slab(slab, cos, sin, first_half, half):
    swapped = jnp.where(first_half, pltpu.roll(slab, LANES - half, 1), pltpu.roll(slab, half, 1))
    return slab * cos + swapped * sin


def _in_proj_kernel(x_ref, wa_ref, wi_ref, wb_ref, wg_ref, wgate_ref, bgate_ref,
                    cosa_ref, sina_ref, cosi_ref, sini_ref,
                    q_ref, k_ref, v_ref, iq_ref, ikw_ref, bq_ref, bk_ref, bv_ref, br_ref, la_ref):
    tm = x_ref.shape[0]
    xb = x_ref[...].astype(BF16)
    lane = lax.broadcasted_iota(jnp.int32, (tm, LANES), 1)

    a = _dot(xb, wa_ref[...])
    cosa, sina = cosa_ref[...], sina_ref[...]
    first_a = (lane & (DSA_HEAD_DIM - 1)) < DSA_HEAD_DIM // 2
    for c in range(DSA_WIDTH // LANES):
        sl = slice(c * LANES, (c + 1) * LANES)
        q_ref[:, sl] = _rope_slab(a[:, sl], cosa, sina, first_a, DSA_HEAD_DIM // 2).astype(BF16)
        ks = slice(DSA_WIDTH + c * LANES, DSA_WIDTH + (c + 1) * LANES)
        k_ref[:, sl] = _rope_slab(a[:, ks], cosa, sina, first_a, DSA_HEAD_DIM // 2).astype(BF16)
    v_ref[...] = a[:, 2 * DSA_WIDTH:3 * DSA_WIDTH].astype(BF16)

    ii = _dot(xb, wi_ref[...])
    cosi, sini = cosi_ref[...], sini_ref[...]
    first_i = (lane & (IDX_DIM - 1)) < IDX_DIM // 2
    for c in range(2):
        sl = slice(c * LANES, (c + 1) * LANES)
        iq_ref[:, sl] = _rope_slab(ii[:, sl], cosi, sini, first_i, IDX_DIM // 2).astype(BF16)
    last = ii[:, 2 * LANES:3 * LANES]
    ikw_ref[...] = jnp.where(lane < IDX_DIM, _rope_slab(last, cosi, sini, first_i, IDX_DIM // 2), last)

    b = _dot(xb, wb_ref[...])
    bq_ref[...] = b[:, 0:256]
    bk_ref[...] = b[:, 256:512]
    bv_ref[...] = b[:, 512:1024]
    br_ref[...] = b[:, 1024:1536]

    g = _dot(xb, wg_ref[...])
    z = _dot(g.astype(BF16), wgate_ref[...]) + bgate_ref[...]
    log_sig = jnp.minimum(z, 0.0) - jnp.log(1.0 + jnp.exp(-jnp.abs(z)))
    la_ref[...] = log_sig * (1.0 / GLA_TAU)


def _in_proj(x2, wa, wi, wb, wg, wgate, bgate, cosa, sina, cosi, sini, T):
    N = x2.shape[0]
    tm = TOK_TILE
    nt = T // tm
    row = lambda i: (i, 0)
    const = lambda i: (0, 0)
    pos = lambda i: (i % nt, 0)
    outs = [
        (DSA_WIDTH, BF16), (DSA_WIDTH, BF16), (DSA_WIDTH, BF16), (IDX_HEADS * IDX_DIM, BF16), (LANES, F32),
        (256, F32), (256, F32), (512, F32), (512, F32), (256, F32),
    ]
    return pl.pallas_call(
        _in_proj_kernel,
        grid=(N // tm,),
        in_specs=[
            pl.BlockSpec((tm, D_MODEL), row),
            pl.BlockSpec(wa.shape, const), pl.BlockSpec(wi.shape, const), pl.BlockSpec(wb.shape, const),
            pl.BlockSpec(wg.shape, const), pl.BlockSpec(wgate.shape, const), pl.BlockSpec(bgate.shape, const),
            pl.BlockSpec((tm, LANES), pos), pl.BlockSpec((tm, LANES), pos),
            pl.BlockSpec((tm, LANES), pos), pl.BlockSpec((tm, LANES), pos),
        ],
        out_specs=[pl.BlockSpec((tm, w), row) for w, _ in outs],
        out_shape=[jax.ShapeDtypeStruct((N, w), dt) for w, dt in outs],
        compiler_params=_params(("parallel",)),
    )(x2, wa, wi, wb, wg, wgate, bgate, cosa, sina, cosi, sini)


def _dsa_kernel(iq_ref, ikt_ref, iw_ref, q_ref, kt_ref, v_ref, o_ref,
                s_scr, iwb_scr, qm_scr, st_scr, m_scr, l_scr, acc_scr, *, topk, seq_len):
    tq = DSA_TILE
    tk = DSA_TILE
    nsl = tk // LANES
    i = pl.program_id(1)
    kf = jnp.float32(topk)

    lane = lax.broadcasted_iota(jnp.int32, (tq, LANES), 1)
    rowi = lax.broadcasted_iota(jnp.int32, (tq, LANES), 0)

    iw = iw_ref[0]
    for h in range(IDX_HEADS):
        iwb_scr[h] = jnp.broadcast_to(iw[:, h:h + 1], (tq, LANES))
    for h in range(DSA_HEADS):
        slab = q_ref[0, :, (h // 2) * LANES:(h // 2 + 1) * LANES]
        mine = (lane >= DSA_HEAD_DIM) if (h % 2) else (lane < DSA_HEAD_DIM)
        qm_scr[h] = jnp.where(mine, slab, jnp.zeros_like(slab))

    iq_all = iq_ref[0].reshape(IDX_HEADS * tq, IDX_DIM)

    def score_tile(j):
        r = _dot(iq_all, ikt_ref[0, j])
        slabs = []
        for c in range(nsl):
            acc = None
            for h in range(IDX_HEADS):
                term = jnp.maximum(r[h * tq:(h + 1) * tq, c * LANES:(c + 1) * LANES], 0.0) * iwb_scr[h]
                acc = term if acc is None else acc + term
            slabs.append(acc)
        return slabs

    def score_body(j, carry):
        rmax, rmin = carry
        slabs = score_tile(j)
        for c in range(nsl):
            s_scr[j, :, c * LANES:(c + 1) * LANES] = slabs[c]
            rmax = jnp.maximum(rmax, slabs[c])
            rmin = jnp.minimum(rmin, slabs[c])
        return rmax, rmin

    init = (jnp.full((tq, LANES), -jnp.inf, F32), jnp.full((tq, LANES), jnp.inf, F32))
    rmax, rmin = lax.fori_loop(0, i, score_body, init)
    slabs = score_tile(i)
    for c in range(nsl):
        adm = ((lane + c * LANES) >> 6) <= (rowi >> 6)
        s_scr[i, :, c * LANES:(c + 1) * LANES] = jnp.where(adm, slabs[c], -jnp.inf)
        rmax = jnp.maximum(rmax, jnp.where(adm, slabs[c], -jnp.inf))
        rmin = jnp.minimum(rmin, jnp.where(adm, slabs[c], jnp.inf))
    rmax = jnp.broadcast_to(jnp.max(rmax, axis=1, keepdims=True), (tq, LANES))
    rmin = jnp.broadcast_to(jnp.min(rmin, axis=1, keepdims=True), (tq, LANES))

    ones_sq = jnp.ones((LANES, LANES), BF16)

    def count_rows(pred):
        def body(j, acc):
            tile = s_scr[j]
            for c in range(nsl):
                acc = acc + jnp.where(pred(tile[:, c * LANES:(c + 1) * LANES], j, c), 1.0, 0.0)
            return acc
        acc = lax.fori_loop(0, i + 1, body, jnp.zeros((tq, LANES), F32))
        return _dot(acc.astype(BF16), ones_sq)

    n_adm = (((rowi >> 6) + 1 + i * (tq // CHUNK)) * CHUNK).astype(F32)
    st_scr[0] = rmin
    st_scr[1] = rmax + (jnp.abs(rmax) * (2.0 ** -10) + 1e-30)
    st_scr[2] = n_adm
    st_scr[3] = jnp.where(n_adm <= kf, 1.0, 0.0)

    def bis_cond(carry):
        it, pending = carry
        return jnp.logical_and(pending > 0.0, it < BISECT_CAP)

    def bis_body(carry):
        it, _ = carry
        lo, hi, clo, done = st_scr[0], st_scr[1], st_scr[2], st_scr[3]
        mid = lo + (hi - lo) * 0.5
        stuck = jnp.logical_or(mid <= lo, mid >= hi)
        cnt = count_rows(lambda t, j, c: t >= mid)
        live = jnp.logical_and(done == 0.0, jnp.logical_not(stuck))
        up = jnp.logical_and(live, cnt >= kf)
        dn = jnp.logical_and(live, cnt < kf)
        lo = jnp.where(up, mid, lo)
        clo = jnp.where(up, cnt, clo)
        hi = jnp.where(dn, mid, hi)
        done = jnp.where(jnp.logical_or(stuck, clo == kf), 1.0, done)
        st_scr[0], st_scr[1], st_scr[2], st_scr[3] = lo, hi, clo, done
        return it + 1, jnp.max(1.0 - done)

    lax.while_loop(bis_cond, bis_body, (jnp.int32(0), jnp.max(1.0 - st_scr[3])))
    thr = st_scr[0]
    clo = st_scr[2]

    @pl.when(jnp.max(clo) > kf)
    def _():
        tie = clo > kf
        cgt = count_rows(lambda t, j, c: t > thr)
        need = kf - cgt
        n_it = int(np.ceil(np.log2(seq_len))) + 1

        def col_index(j, c):
            return (lane + c * LANES + j * tk).astype(F32)

        def jb(_, carry):
            jlo, jhi = carry
            jm = jnp.floor((jlo + jhi) * 0.5)
            f = count_rows(lambda t, j, c: jnp.logical_and(t == thr, col_index(j, c) <= jm))
            ok = f >= need
            return jnp.where(ok, jlo, jm), jnp.where(ok, jm, jhi)

        _, jcut = lax.fori_loop(0, n_it, jb, (jnp.full((tq, LANES), -1.0, F32),
                                              jnp.full((tq, LANES), float(seq_len - 1), F32)))

        def drop(j, _):
            for c in range(nsl):
                sl = slice(c * LANES, (c + 1) * LANES)
                t = s_scr[j, :, sl]
                kill = jnp.logical_and(tie, jnp.logical_and(t == thr, col_index(j, c) > jcut))
                s_scr[j, :, sl] = jnp.where(kill, -jnp.inf, t)
            return 0

        lax.fori_loop(0, i + 1, drop, 0)

    for h in range(DSA_HEADS):
        m_scr[h] = jnp.full((tq, LANES), NEG_BIG, F32)
        l_scr[h] = jnp.zeros((tq, LANES), F32)
        acc_scr[h] = jnp.zeros((tq, LANES), F32)
    thr_t = jnp.concatenate([thr] * nsl, axis=1)

    def att_body(j, _):
        bias = jnp.where(s_scr[j] >= thr_t, 0.0, NEG_BIG)
        row0 = pl.multiple_of(j * tk, tk)
        for h in range(DSA_HEADS):
            sl = slice((h // 2) * LANES, (h // 2 + 1) * LANES)
            s = _dot(qm_scr[h], kt_ref[0, j, sl, :]) + bias
            m_prev = m_scr[h]
            m_new = jnp.maximum(m_prev, jnp.max(s, axis=1, keepdims=True))
            p = jnp.exp(s - jnp.concatenate([m_new] * nsl, axis=1))
            corr = jnp.exp(m_prev - m_new)
            l_scr[h] = corr * l_scr[h] + jnp.sum(p, axis=1, keepdims=True)
            pv = _dot(p.astype(BF16), v_ref[0, pl.ds(row0, tk), sl])
            acc_scr[h] = acc_scr[h] * corr + pv
            m_scr[h] = m_new
        return 0

    lax.fori_loop(0, i + 1, att_body, 0)

    for sp in range(DSA_HEADS // 2):
        even = acc_scr[2 * sp] / l_scr[2 * sp]
        odd = acc_scr[2 * sp + 1] / l_scr[2 * sp + 1]
        o_ref[0, :, sp * LANES:(sp + 1) * LANES] = jnp.where(lane < DSA_HEAD_DIM, even, odd).astype(o_ref.dtype)


def _dsa(iqh, ikt, iw, q, kt, v):
    B, T, _ = q.shape
    tq = DSA_TILE
    nq = T // tq
    topk = min(DSA_TOPK_MAX, T // 4)
    kern = functools.partial(_dsa_kernel, topk=topk, seq_len=T)
    return pl.pallas_call(
        kern,
        grid=(B, nq),
        in_specs=[
            pl.BlockSpec((1, IDX_HEADS, tq, IDX_DIM), lambda b, i: (b, 0, i, 0)),
            pl.BlockSpec((1, nq, IDX_DIM, tq), lambda b, i: (b, 0, 0, 0)),
            pl.BlockSpec((1, tq, IDX_HEADS), lambda b, i: (b, i, 0)),
            pl.BlockSpec((1, tq, DSA_WIDTH), lambda b, i: (b, i, 0)),
            pl.BlockSpec((1, nq, DSA_WIDTH, tq), lambda b, i: (b, 0, 0, 0)),
            pl.BlockSpec((1, T, DSA_WIDTH), lambda b, i: (b, 0, 0)),
        ],
        out_specs=pl.BlockSpec((1, tq, DSA_WIDTH), lambda b, i: (b, i, 0)),
        out_shape=jax.ShapeDtypeStruct((B, T, DSA_WIDTH), BF16),
        scratch_shapes=[
            pltpu.VMEM((nq, tq, tq), F32),
            pltpu.VMEM((IDX_HEADS, tq, LANES), F32),
            pltpu.VMEM((DSA_HEADS, tq, LANES), BF16),
            pltpu.VMEM((4, tq, LANES), F32),
            pltpu.VMEM((DSA_HEADS, tq, LANES), F32),
            pltpu.VMEM((DSA_HEADS, tq, LANES), F32),
            pltpu.VMEM((DSA_HEADS, tq, LANES), F32),
        ],
        compiler_params=_params(("parallel", "arbitrary")),
    )(iqh, ikt, iw, q, kt, v)


def _gla_kernel(q_ref, k_ref, v_ref, r_ref, la_ref, g_ref, o_ref, s_scr):
    rows = GLA_BLOCK
    nch = rows // CHUNK
    npair = GLA_HEADS // 2

    @pl.when(pl.program_id(1) == 0)
    def _():
        s_scr[...] = jnp.zeros_like(s_scr)

    ri = lax.broadcasted_iota(jnp.int32, (rows, rows), 0)
    ci = lax.broadcasted_iota(jnp.int32, (rows, rows), 1)
    causal = jnp.logical_and((ri >> 6) == (ci >> 6), ci <= ri)
    tri = jnp.where(causal, 1.0, 0.0).astype(BF16)

    la = la_ref[0]
    la_hi = la.astype(BF16)
    rem = la - la_hi.astype(F32)
    la_mid = rem.astype(BF16)
    la_lo = (rem - la_mid.astype(F32)).astype(BF16)
    b = _dot(tri, la_hi) + _dot(tri, la_mid) + _dot(tri, la_lo)
    b_last = jnp.concatenate(
        [jnp.broadcast_to(b[(c + 1) * CHUNK - 1:(c + 1) * CHUNK, :], (CHUNK, b.shape[1])) for c in range(nch)], axis=0)

    q = q_ref[0]
    k = k_ref[0]
    qg = q * jnp.exp(b)
    kg = (k * jnp.exp(-b)).astype(BF16)
    kd = (k * jnp.exp(b_last - b)).astype(BF16)
    decay_rows = jnp.exp(b_last)

    lane = lax.broadcasted_iota(jnp.int32, (rows, LANES), 1)
    eye = lax.broadcasted_iota(jnp.int32, (LANES, LANES), 0) == lax.broadcasted_iota(jnp.int32, (LANES, LANES), 1)
    top_rows = lax.broadcasted_iota(jnp.int32, (LANES, GLA_DV), 0) < GLA_DK

    for pr in range(npair):
        sl = slice(pr * LANES, (pr + 1) * LANES)
        qg_p = qg[:, sl]
        kg_p = kg[:, sl]
        kd_p = kd[:, sl]
        heads = (2 * pr, 2 * pr + 1)
        qg_h = [jnp.where(lane < GLA_DK, qg_p, 0.0).astype(BF16), jnp.where(lane >= GLA_DK, qg_p, 0.0).astype(BF16)]
        v_h = [v_ref[0, :, h * GLA_DV:(h + 1) * GLA_DV].astype(BF16) for h in heads]
        o_h = []
        for t in range(2):
            a = jnp.where(causal, _dot_nt(qg_h[t], kg_p), 0.0)
            o_h.append(_dot(a.astype(BF16), v_h[t]))
        state = s_scr[pr]
        inter = [[], []]
        for c in range(nch):
            rs = slice(c * CHUNK, (c + 1) * CHUNK)
            sb = state.astype(BF16)
            for t in range(2):
                inter[t].append(_dot(qg_h[t][rs], sb))
            kv = jnp.where(top_rows, _dot_tn(kd_p[rs], v_h[0][rs]), _dot_tn(kd_p[rs], v_h[1][rs]))
            drow = jnp.broadcast_to(decay_rows[c * CHUNK:c * CHUNK + 1, sl], (LANES, LANES))
            dcol = jnp.sum(jnp.where(eye, drow, 0.0), axis=1, keepdims=True)
            state = dcol * state + kv
        s_scr[pr] = state
        for t in range(2):
            h = heads[t]
            o = o_h[t] + jnp.concatenate(inter[t], axis=0)
            ms = jnp.mean(o * o, axis=1, keepdims=True)
            on = o * lax.rsqrt(ms + LN_EPS) * g_ref[...]
            r = r_ref[0, :, h * GLA_DV:(h + 1) * GLA_DV]
            o_ref[0, :, h * GLA_DV:(h + 1) * GLA_DV] = (on * (r / (1.0 + jnp.exp(-r)))).astype(o_ref.dtype)


def _gla(bq, bk, bv, br, la, g):
    B, T, _ = bq.shape
    rows = GLA_BLOCK
    blk = lambda w: pl.BlockSpec((1, rows, w), lambda b, i: (b, i, 0))
    return pl.pallas_call(
        _gla_kernel,
        grid=(B, T // rows),
        in_specs=[blk(256), blk(256), blk(512), blk(512), blk(256), pl.BlockSpec((1, GLA_DV), lambda b, i: (0, 0))],
        out_specs=blk(GLA_WIDTH),
        out_shape=jax.ShapeDtypeStruct((B, T, GLA_WIDTH), BF16),
        scratch_shapes=[pltpu.VMEM((GLA_HEADS // 2, 2 * GLA_DK, GLA_DV), F32)],
        compiler_params=_params(("parallel", "arbitrary")),
    )(bq, bk, bv, br, la, g)


def _layer_norm(x, g, b):
    mu = jnp.mean(x, axis=1, keepdims=True)
    xc = x - mu
    var = jnp.mean(xc * xc, axis=1, keepdims=True)
    return xc * lax.rsqrt(var + LN_EPS) * g + b


def _out_proj_kernel(x_ref, ya_ref, yb_ref, wo_ref, g_ref, b_ref, wr_hi_ref, wr_lo_ref, br_ref, h_ref, route_ref):
    tm = x_ref.shape[0]
    mix = _dot(ya_ref[...], wo_ref[0:DSA_WIDTH, :]) + _dot(yb_ref[...], wo_ref[DSA_WIDTH:D_MODEL, :])
    h = _layer_norm(DEEPNORM_ALPHA * x_ref[...] + mix, g_ref[...], b_ref[...])
    h_ref[...] = h

    h_hi = h.astype(BF16)
    h_lo = (h - h_hi.astype(F32)).astype(BF16)
    logits = _dot(h_hi, wr_hi_ref[...]) + _dot(h_lo, wr_hi_ref[...]) + _dot(h_hi, wr_lo_ref[...]) + br_ref[...]

    lane = lax.broadcasted_iota(jnp.int32, (tm, LANES), 1)
    lanef = lane.astype(F32)
    gl = jnp.where(lane < N_GROUPS, logits, -jnp.inf)
    gmax = jnp.max(gl, axis=1, keepdims=True)
    gsel = jnp.min(jnp.where(gl == gmax, lanef, 1e9), axis=1, keepdims=True)
    pg = 1.0 / jnp.sum(jnp.exp(gl - gmax), axis=1, keepdims=True)
    egrp = ((lane - N_GROUPS) >> 3).astype(F32)
    in_grp = jnp.logical_and(jnp.logical_and(lane >= N_GROUPS, lane < N_GROUPS + N_EXPERTS), egrp == gsel)
    el = jnp.where(in_grp, logits, -jnp.inf)
    t1 = jnp.max(el, axis=1, keepdims=True)
    i1 = jnp.min(jnp.where(el == t1, lanef, 1e9), axis=1, keepdims=True)
    el2 = jnp.where(lanef == i1, -jnp.inf, el)
    t2 = jnp.max(el2, axis=1, keepdims=True)
    i2 = jnp.min(jnp.where(el2 == t2, lanef, 1e9), axis=1, keepdims=True)
    e21 = jnp.exp(t2 - t1)
    g1 = pg / (1.0 + e21)
    g2 = pg * e21 / (1.0 + e21)
    route = jnp.where(lane == 0, i1 - N_GROUPS, 0.0)
    route = jnp.where(lane == 1, i2 - N_GROUPS, route)
    route = jnp.where(lane == 2, g1, route)
    route = jnp.where(lane == 3, g2, route)
    route_ref[...] = route


def _out_proj(x2, ya, yb, wo, g1, b1, wr_hi, wr_lo, br):
    N = x2.shape[0]
    tm = TOK_TILE
    row = lambda i: (i, 0)
    const = lambda i: (0, 0)
    return pl.pallas_call(
        _out_proj_kernel,
        grid=(N // tm,),
        in_specs=[
            pl.BlockSpec((tm, D_MODEL), row), pl.BlockSpec((tm, DSA_WIDTH), row), pl.BlockSpec((tm, GLA_WIDTH), row),
            pl.BlockSpec(wo.shape, const), pl.BlockSpec(g1.shape, const), pl.BlockSpec(b1.shape, const),
            pl.BlockSpec(wr_hi.shape, const), pl.BlockSpec(wr_lo.shape, const), pl.BlockSpec(br.shape, const),
        ],
        out_specs=[pl.BlockSpec((tm, D_MODEL), row), pl.BlockSpec((tm, LANES), row)],
        out_shape=[jax.ShapeDtypeStruct((N, D_MODEL), F32), jax.ShapeDtypeStruct((N, LANES), F32)],
        compiler_params=_params(("parallel",)),
    )(x2, ya, yb, wo, g1, b1, wr_hi, wr_lo, br)


def _rank_kernel(route_ref, rank_ref, cnt_ref, carry_scr):
    tm = route_ref.shape[0]

    @pl.when(pl.program_id(0) == 0)
    def _():
        carry_scr[...] = jnp.zeros_like(carry_scr)

    route = route_ref[...]
    lanef = lax.broadcasted_iota(jnp.int32, (tm, LANES), 1).astype(F32)
    e1 = route[:, 0:1]
    e2 = route[:, 1:2]
    hit1 = lanef == e1
    hit2 = lanef == e2
    onehot = jnp.where(jnp.logical_or(hit1, hit2), 1.0, 0.0).astype(BF16)
    ri = lax.broadcasted_iota(jnp.int32, (tm, tm), 0)
    ci = lax.broadcasted_iota(jnp.int32, (tm, tm), 1)
    before = jnp.where(ci < ri, 1.0, 0.0).astype(BF16)
    prefix = _dot(before, onehot) + carry_scr[0:1, :]
    r1 = jnp.sum(jnp.where(hit1, prefix, 0.0), axis=1, keepdims=True)
    r2 = jnp.sum(jnp.where(hit2, prefix, 0.0), axis=1, keepdims=True)
    rank_ref[...] = jnp.where(lanef == 0.0, r1, jnp.where(lanef == 1.0, r2, 0.0))
    total = _dot(jnp.ones((8, tm), BF16), onehot)
    carry_scr[...] = carry_scr[...] + total
    cnt_ref[...] = carry_scr[...]


def _rank(route):
    N = route.shape[0]
    tm = RANK_TILE
    return pl.pallas_call(
        _rank_kernel,
        grid=(N // tm,),
        in_specs=[pl.BlockSpec((tm, LANES), lambda i: (i, 0))],
        out_specs=[pl.BlockSpec((tm, LANES), lambda i: (i, 0)), pl.BlockSpec((8, LANES), lambda i: (0, 0))],
        out_shape=[jax.ShapeDtypeStruct((N, LANES), F32), jax.ShapeDtypeStruct((8, LANES), F32)],
        scratch_shapes=[pltpu.VMEM((8, LANES), F32)],
        compiler_params=_params(("arbitrary",)),
    )(route)


def _dispatch_kernel(pos_ref, h_ref, xs_in_ref, xs_ref, sem):
    del xs_in_ref
    tm = h_ref.shape[0]
    base = pl.program_id(0) * tm

    def row_copy(r, slot):
        p = pos_ref[2 * (base + r) + slot]
        return pltpu.make_async_copy(h_ref.at[pl.ds(r, 1), :], xs_ref.at[pl.ds(p, 1), :], sem)

    def start(r, _):
        row_copy(r, 0).start()
        row_copy(r, 1).start()
        return 0

    def wait(r, _):
        row_copy(r, 0).wait()
        row_copy(r, 1).wait()
        return 0

    lax.fori_loop(0, tm, start, 0)
    lax.fori_loop(0, tm, wait, 0)


def _dispatch(pos_flat, h, xs_init):
    N = h.shape[0]
    tm = TOK_TILE
    grid_spec = pltpu.PrefetchScalarGridSpec(
        num_scalar_prefetch=1,
        grid=(N // tm,),
        in_specs=[pl.BlockSpec((tm, D_MODEL), lambda i, pos: (i, 0)), pl.BlockSpec(memory_space=pl.ANY)],
        out_specs=pl.BlockSpec(memory_space=pl.ANY),
        scratch_shapes=[pltpu.SemaphoreType.DMA(())],
    )
    return pl.pallas_call(
        _dispatch_kernel,
        grid_spec=grid_spec,
        out_shape=jax.ShapeDtypeStruct(xs_init.shape, xs_init.dtype),
        input_output_aliases={2: 0},
        compiler_params=_params(("arbitrary",)),
    )(pos_flat, h, xs_init)


def _ffn_kernel(te_ref, na_ref, x_ref, wi_ref, wo_ref, y_ref):
    active = pl.program_id(0) < na_ref[0]

    @pl.when(active)
    def _():
        hid = _dot(x_ref[...].astype(BF16), wi_ref[0])
        hg = hid[:, :D_EXPERT]
        hu = hid[:, D_EXPERT:]
        act = (hg / (1.0 + jnp.exp(-hg))) * hu
        y_ref[...] = _dot(act.astype(BF16), wo_ref[0])

    @pl.when(jnp.logical_not(active))
    def _():
        y_ref[...] = jnp.zeros_like(y_ref)


def _ffn(tile_expert, n_active, xs, w_e_in, w_e_out):
    R = xs.shape[0]
    tm = MOE_TILE

    def live(i, te, na):
        return jnp.minimum(i, na[0] - 1)

    grid_spec = pltpu.PrefetchScalarGridSpec(
        num_scalar_prefetch=2,
        grid=(R // tm,),
        in_specs=[
            pl.BlockSpec((tm, D_MODEL), lambda i, te, na: (live(i, te, na), 0)),
            pl.BlockSpec((1, D_MODEL, 2 * D_EXPERT), lambda i, te, na: (te[live(i, te, na)], 0, 0)),
            pl.BlockSpec((1, D_EXPERT, D_MODEL), lambda i, te, na: (te[live(i, te, na)], 0, 0)),
        ],
        out_specs=pl.BlockSpec((tm, D_MODEL), lambda i, te, na: (i, 0)),
    )
    return pl.pallas_call(
        _ffn_kernel,
        grid_spec=grid_spec,
        out_shape=jax.ShapeDtypeStruct((R, D_MODEL), F32),
        compiler_params=_params(("arbitrary",)),
    )(tile_expert, n_active, xs, w_e_in, w_e_out)


def _combine_kernel(pos_ref, h_ref, route_ref, g_ref, b_ref, ys_ref, o_ref, buf, sem):
    tm = h_ref.shape[0]
    base = pl.program_id(0) * tm

    def row_copy(r, slot):
        p = pos_ref[2 * (base + r) + slot]
        return pltpu.make_async_copy(ys_ref.at[pl.ds(p, 1), :], buf.at[slot, pl.ds(r, 1), :], sem)

    def start(r, _):
        row_copy(r, 0).start()
        row_copy(r, 1).start()
        return 0

    def wait(r, _):
        row_copy(r, 0).wait()
        row_copy(r, 1).wait()
        return 0

    lax.fori_loop(0, tm, start, 0)
    lax.fori_loop(0, tm, wait, 0)
    route = route_ref[...]
    ffn = buf[0] * route[:, 2:3] + buf[1] * route[:, 3:4]
    o_ref[...] = _layer_norm(DEEPNORM_ALPHA * h_ref[...] + ffn, g_ref[...], b_ref[...])


def _combine(pos_flat, h, route, g2, b2, ys):
    N = h.shape[0]
    tm = TOK_TILE
    row = lambda i, pos: (i, 0)
    const = lambda i, pos: (0, 0)
    grid_spec = pltpu.PrefetchScalarGridSpec(
        num_scalar_prefetch=1,
        grid=(N // tm,),
        in_specs=[
            pl.BlockSpec((tm, D_MODEL), row), pl.BlockSpec((tm, LANES), row),
            pl.BlockSpec(g2.shape, const), pl.BlockSpec(b2.shape, const),
            pl.BlockSpec(memory_space=pl.ANY),
        ],
        out_specs=pl.BlockSpec((tm, D_MODEL), row),
        scratch_shapes=[pltpu.VMEM((2, tm, D_MODEL), F32), pltpu.SemaphoreType.DMA(())],
    )
    return pl.pallas_call(
        _combine_kernel,
        grid_spec=grid_spec,
        out_shape=jax.ShapeDtypeStruct((N, D_MODEL), F32),
        compiler_params=_params(("arbitrary",)),
    )(pos_flat, h, route, g2, b2, ys)


def _rope_tables(T, dim):
    half = dim // 2
    inv = 1.0 / (ROPE_THETA ** (jnp.arange(half, dtype=F32) / half))
    ang = jnp.arange(T).astype(F32)[:, None] * inv[None, :]
    cos = jnp.cos(ang)
    sin = jnp.sin(ang)
    reps = LANES // dim
    cos_t = jnp.tile(jnp.concatenate([cos, cos], axis=1), (1, reps))
    sin_t = jnp.tile(jnp.concatenate([-sin, sin], axis=1), (1, reps))
    return cos_t, sin_t


def _pad_cols(w, width):
    return jnp.pad(w, ((0, 0), (0, width - w.shape[1])))


def _layer(x, w_in, w_gla_gate, b_gla_gate, g_gla_norm, w_out, ln1_g, ln1_b,
           w_gr, b_gr, w_er, b_er, w_e_in, w_e_out, ln2_g, ln2_b):
    B, T, D = x.shape
    N = B * T
    assert D == D_MODEL and T % DSA_TILE == 0 and N % RANK_TILE == 0
    x2 = x.reshape(N, D)

    sizes = (512, 512, 512, 256, 32, 8, 256, 256, 512, 512, 16)
    offs = np.concatenate([[0], np.cumsum(sizes)])
    col = lambda k: w_in[:, offs[k]:offs[k + 1]]
    wa = jnp.concatenate([col(0) * (DSA_HEAD_DIM ** -0.5), col(1), col(2)], axis=1).astype(BF16)
    wi = _pad_cols(jnp.concatenate([col(3), col(4), col(5) * IDX_SCALE], axis=1), 3 * LANES).astype(BF16)
    wb = jnp.concatenate([col(6) * (GLA_DK ** -0.5), col(7), col(8), col(9)], axis=1).astype(BF16)
    wg = _pad_cols(col(10), LANES).astype(BF16)
    wgate = jnp.pad(w_gla_gate, ((0, LANES - GLA_GATE_RANK), (0, 0))).astype(BF16)
    bgate = b_gla_gate.reshape(1, -1)
    cosa, sina = _rope_tables(T, DSA_HEAD_DIM)
    cosi, sini = _rope_tables(T, IDX_DIM)

    q, k, v, iq, ikw, bq, bk, bv, br, la = _in_proj(x2, wa, wi, wb, wg, wgate, bgate, cosa, sina, cosi, sini, T)

    nq = T // DSA_TILE
    iqh = iq.reshape(B, T, IDX_HEADS, IDX_DIM).transpose(0, 2, 1, 3)
    ikt = ikw[:, :IDX_DIM].astype(BF16).reshape(B, nq, DSA_TILE, IDX_DIM).transpose(0, 1, 3, 2)
    iw = ikw[:, IDX_DIM:IDX_DIM + IDX_HEADS].reshape(B, T, IDX_HEADS)
    kt = k.reshape(B, nq, DSA_TILE, DSA_WIDTH).transpose(0, 1, 3, 2)
    ya = _dsa(iqh, ikt, iw, q.reshape(B, T, DSA_WIDTH), kt, v.reshape(B, T, DSA_WIDTH))

    r3 = lambda a: a.reshape(B, T, a.shape[-1])
    yb = _gla(r3(bq), r3(bk), r3(bv), r3(br), r3(la), g_gla_norm.reshape(1, GLA_DV))

    wr = _pad_cols(jnp.concatenate([w_gr, w_er], axis=1), LANES)
    wr_hi = wr.astype(BF16)
    wr_lo = (wr - wr_hi.astype(F32)).astype(BF16)
    brt = _pad_cols(jnp.concatenate([b_gr, b_er]).reshape(1, -1), LANES)
    h, route = _out_proj(x2, ya.reshape(N, DSA_WIDTH), yb.reshape(N, GLA_WIDTH), w_out.astype(BF16),
                         ln1_g.reshape(1, D), ln1_b.reshape(1, D), wr_hi, wr_lo, brt)

    rank, cnt = _rank(route)
    counts = cnt[0, :N_EXPERTS].astype(jnp.int32)
    padded = ((counts + MOE_TILE - 1) // MOE_TILE) * MOE_TILE
    ends = jnp.cumsum(padded)
    starts = ends - padded
    eid = route[:, 0:2].astype(jnp.int32)
    pos = (starts[eid] + rank[:, 0:2].astype(jnp.int32)).reshape(-1)
    n_rows = 2 * N + N_EXPERTS * MOE_TILE
    n_tiles = n_rows // MOE_TILE
    tile_start = jnp.arange(n_tiles, dtype=jnp.int32) * MOE_TILE
    tile_expert = jnp.minimum(jnp.sum(tile_start[:, None] >= ends[None, :], axis=1), N_EXPERTS - 1).astype(jnp.int32)
    n_active = (ends[-1] // MOE_TILE).astype(jnp.int32).reshape(1)

    xs = _dispatch(pos, h, jnp.zeros((n_rows, D), F32))
    ys = _ffn(tile_expert, n_active, xs, w_e_in.astype(BF16), w_e_out.astype(BF16))
    out = _combine(pos, h, route, ln2_g.reshape(1, D), ln2_b.reshape(1, D), ys)
    return out.reshape(B, T, D)


def kernel(x, w_in, w_gla_gate, b_gla_gate, g_gla_norm, w_out, ln1_g, ln1_b, w_group_router, b_group_router,
           w_expert_router, b_expert_router, w_expert_in, w_expert_out, ln2_g, ln2_b):
    h = x
    for l in range(w_in.shape[0]):
        h = _layer(h, w_in[l], w_gla_gate[l], b_gla_gate[l], g_gla_norm[l], w_out[l], ln1_g[l], ln1_b[l],
                   w_group_router[l], b_group_router[l], w_expert_router[l], b_expert_router[l],
                   w_expert_in[l], w_expert_out[l], ln2_g[l], ln2_b[l])
    return h
```

```python
import functools

import numpy as np
import jax
import jax.numpy as jnp
from jax import lax
from jax.experimental import pallas as pl
from jax.experimental.pallas import tpu as pltpu

F32 = jnp.float32
BF16 = jnp.bfloat16

D_MODEL = 1024
CHUNK = 64
ROPE_THETA = 10000.0
LN_EPS = 1e-5
DSA_WIDTH = 512
DSA_HEAD_DIM = 64
DSA_HEADS = 8
IDX_HEADS = 8
IDX_DIM = 32
IDX_SCALE = (IDX_HEADS * IDX_DIM) ** -0.5
DSA_TOPK_MAX = 256
GLA_WIDTH = 512
GLA_HEADS = 4
GLA_DV = 128
GLA_DK = 64
GLA_GATE_RANK = 16
GLA_TAU = 16.0
N_GROUPS = 4
EXPERTS_PER_GROUP = 8
N_EXPERTS = 32
D_EXPERT = 512
DEEPNORM_ALPHA = 2.0 ** 0.25

LANES = 128
SUBLANES = 8
TINY = 2.0 ** -126
NEG_BIG = -1e30
VMEM_LIMIT = 56 * 1024 * 1024

TOK_TILE = 256
DSA_TILE = 256
GLA_BLOCK = 256
MOE_TILE = 256
RANK_TILE = 512
BISECT_ARITH = 40
BISECT_CAP = 80


def _dot(a, b):
    return jnp.dot(a, b, preferred_element_type=F32)


def _dot_nt(a, b):
    return lax.dot_general(a, b, (((1,), (1,)), ((), ())), preferred_element_type=F32)


def _dot_tn(a, b):
    return lax.dot_general(a, b, (((0,), (0,)), ((), ())), preferred_element_type=F32)


def _params(sem):
    return pltpu.CompilerParams(dimension_semantics=sem, vmem_limit_bytes=VMEM_LIMIT)


def _rope_slab(slab, cos, sin, first_half, half):
    swapped = jnp.where(first_half, pltpu.roll(slab, LANES - half, 1), pltpu.roll(slab, half, 1))
    return slab * cos + swapped * sin


def _in_proj_kernel(x_ref, wa_ref, wi_ref, wb_ref, wg_ref, wgate_ref, bgate_ref,
                    cosa_ref, sina_ref, cosi_ref, sini_ref,
                    q_ref, k_ref, v_ref, iq_ref, ikw_ref, bq_ref, bk_ref, bv_ref, br_ref, la_ref):
    tm = x_ref.shape[0]
    xb = x_ref[...].astype(BF16)
    lane = lax.broadcasted_iota(jnp.int32, (tm, LANES), 1)

    a = _dot(xb, wa_ref[...])
    cosa, sina = cosa_ref[...], sina_ref[...]
    first_a = (lane & (DSA_HEAD_DIM - 1)) < DSA_HEAD_DIM // 2
    for c in range(DSA_WIDTH // LANES):
        sl = slice(c * LANES, (c + 1) * LANES)
        q_ref[:, sl] = _rope_slab(a[:, sl], cosa, sina, first_a, DSA_HEAD_DIM // 2).astype(BF16)
        ks = slice(DSA_WIDTH + c * LANES, DSA_WIDTH + (c + 1) * LANES)
        k_ref[:, sl] = _rope_slab(a[:, ks], cosa, sina, first_a, DSA_HEAD_DIM // 2).astype(BF16)
    v_ref[...] = a[:, 2 * DSA_WIDTH:3 * DSA_WIDTH].astype(BF16)

    ii = _dot(xb, wi_ref[...])
    cosi, sini = cosi_ref[...], sini_ref[...]
    first_i = (lane & (IDX_DIM - 1)) < IDX_DIM // 2
    for c in range(2):
        sl = slice(c * LANES, (c + 1) * LANES)
        iq_ref[:, sl] = _rope_slab(ii[:, sl], cosi, sini, first_i, IDX_DIM // 2).astype(BF16)
    last = ii[:, 2 * LANES:3 * LANES]
    ikw_ref[...] = jnp.where(lane < IDX_DIM, _rope_slab(last, cosi, sini, first_i, IDX_DIM // 2), last)

    b = _dot(xb, wb_ref[...])
    bq_ref[...] = b[:, 0:256]
    bk_ref[...] = b[:, 256:512]
    bv_ref[...] = b[:, 512:1024]
    br_ref[...] = b[:, 1024:1536]

    g = _dot(xb, wg_ref[...])
    z = _dot(g.astype(BF16), wgate_ref[...]) + bgate_ref[...]
    log_sig = jnp.minimum(z, 0.0) - jnp.log(1.0 + jnp.exp(-jnp.abs(z)))
    la_ref[...] = log_sig * (1.0 / GLA_TAU)


def _in_proj(x2, wa, wi, wb, wg, wgate, bgate, cosa, sina, cosi, sini, T):
    N = x2.shape[0]
    tm = TOK_TILE
    nt = T // tm
    row = lambda i: (i, 0)
    const = lambda i: (0, 0)
    pos = lambda i: (i % nt, 0)
    outs = [
        (DSA_WIDTH, BF16), (DSA_WIDTH, BF16), (DSA_WIDTH, BF16), (IDX_HEADS * IDX_DIM, BF16), (LANES, F32),
        (256, F32), (256, F32), (512, F32), (512, F32), (256, F32),
    ]
    return pl.pallas_call(
        _in_proj_kernel,
        grid=(N // tm,),
        in_specs=[
            pl.BlockSpec((tm, D_MODEL), row),
            pl.BlockSpec(wa.shape, const), pl.BlockSpec(wi.shape, const), pl.BlockSpec(wb.shape, const),
            pl.BlockSpec(wg.shape, const), pl.BlockSpec(wgate.shape, const), pl.BlockSpec(bgate.shape, const),
            pl.BlockSpec((tm, LANES), pos), pl.BlockSpec((tm, LANES), pos),
            pl.BlockSpec((tm, LANES), pos), pl.BlockSpec((tm, LANES), pos),
        ],
        out_specs=[pl.BlockSpec((tm, w), row) for w, _ in outs],
        out_shape=[jax.ShapeDtypeStruct((N, w), dt) for w, dt in outs],
        compiler_params=_params(("parallel",)),
    )(x2, wa, wi, wb, wg, wgate, bgate, cosa, sina, cosi, sini)


def _dsa_kernel(iqt_ref, ik_ref, iwt_ref, q_ref, kt_ref, v_ref, o_ref,
                s_scr, qm_scr, st_scr, m_scr, l_scr, acc_scr, *, topk, seq_len):
    tq = DSA_TILE
    tk = DSA_TILE
    grp = tk // SUBLANES
    nsl = tk // LANES
    i = pl.program_id(1)
    kf = jnp.float32(topk)

    iqt = iqt_ref[0, 0]
    iwt = iwt_ref[0]
    krow = lax.broadcasted_iota(jnp.int32, (tk, tq), 0)
    qcol = lax.broadcasted_iota(jnp.int32, (tk, tq), 1)
    qcol8 = lax.broadcasted_iota(jnp.int32, (SUBLANES, tq), 1)

    def fold(x, op):
        return op(x.reshape(grp, SUBLANES, tq), axis=0)

    def spread(x):
        return jnp.broadcast_to(x, (SUBLANES, tq))

    def score_tile(j):
        keys = ik_ref[0, pl.ds(pl.multiple_of(j * tk, tk), tk), :]
        rt = _dot(keys, iqt)
        acc = None
        for h in range(IDX_HEADS):
            term = jnp.maximum(rt[:, h * tq:(h + 1) * tq], 0.0) * iwt[h:h + 1, :]
            acc = term if acc is None else acc + term
        return acc

    def score_body(j, carry):
        rmax, rmin = carry
        s = score_tile(j)
        s_scr[j] = s
        return jnp.maximum(rmax, fold(s, jnp.max)), jnp.minimum(rmin, fold(s, jnp.min))

    init = (jnp.full((SUBLANES, tq), -jnp.inf, F32), jnp.full((SUBLANES, tq), jnp.inf, F32))
    rmax, rmin = lax.fori_loop(0, i, score_body, init)
    s = score_tile(i)
    adm = (krow >> 6) <= (qcol >> 6)
    s_scr[i] = jnp.where(adm, s, -jnp.inf)
    rmax = jnp.maximum(rmax, fold(jnp.where(adm, s, -jnp.inf), jnp.max))
    rmin = jnp.minimum(rmin, fold(jnp.where(adm, s, jnp.inf), jnp.min))
    rmax = spread(jnp.max(rmax, axis=0, keepdims=True))
    rmin = spread(jnp.min(rmin, axis=0, keepdims=True))

    def count(preds):
        def body(j, accs):
            t3 = s_scr[j].reshape(grp, SUBLANES, tq)
            return tuple(a + jnp.sum(jnp.where(p(t3, j), 1.0, 0.0), axis=0) for a, p in zip(accs, preds))
        accs = lax.fori_loop(0, i + 1, body, tuple(jnp.zeros((SUBLANES, tq), F32) for _ in preds))
        return [spread(jnp.sum(a, axis=0, keepdims=True)) for a in accs]

    n_adm = (((qcol8 >> 6) + 1 + i * (tq // CHUNK)) * CHUNK).astype(F32)
    cpos, cnn = count([lambda t, j: t > 0.0, lambda t, j: t >= 0.0])
    search = n_adm > kf
    positive = cpos >= kf
    negative = cnn < kf
    lo = jnp.where(positive, TINY, jnp.where(negative, rmin, 0.0))
    clo = jnp.where(positive, cpos, jnp.where(negative, n_adm, cnn))
    st_scr[0] = jnp.where(search, lo, rmin)
    st_scr[1] = jnp.where(negative, -TINY, rmax + (jnp.abs(rmax) * (2.0 ** -10) + TINY))
    st_scr[2] = jnp.where(search, clo, n_adm)
    zero_thr = jnp.logical_and(jnp.logical_not(positive), jnp.logical_not(negative))
    st_scr[3] = jnp.where(jnp.logical_or(jnp.logical_not(search), jnp.logical_or(zero_thr, clo == kf)), 1.0, 0.0)

    def bis_cond(carry):
        it, pending = carry
        return jnp.logical_and(pending > 0.0, it < BISECT_CAP)

    def bis_body(carry):
        it, _ = carry
        lo, hi, clo, done = st_scr[0], st_scr[1], st_scr[2], st_scr[3]
        a = lax.bitcast_convert_type(jnp.abs(lo), jnp.int32)
        b = lax.bitcast_convert_type(jnp.abs(hi), jnp.int32)
        geo = lax.bitcast_convert_type(a + ((b - a) >> 1), F32)
        geo = jnp.where(hi > 0.0, geo, -geo)
        mid = jnp.where(it < BISECT_ARITH, lo + (hi - lo) * 0.5, geo)
        stuck = jnp.logical_or(mid <= lo, mid >= hi)
        cnt, = count([lambda t, j: t >= mid[None]])
        live = jnp.logical_and(done == 0.0, jnp.logical_not(stuck))
        up = jnp.logical_and(live, cnt >= kf)
        dn = jnp.logical_and(live, cnt < kf)
        lo = jnp.where(up, mid, lo)
        clo = jnp.where(up, cnt, clo)
        hi = jnp.where(dn, mid, hi)
        done = jnp.where(jnp.logical_or(stuck, clo == kf), 1.0, done)
        st_scr[0], st_scr[1], st_scr[2], st_scr[3] = lo, hi, clo, done
        return it + 1, jnp.max(1.0 - done)

    lax.while_loop(bis_cond, bis_body, (jnp.int32(0), jnp.max(1.0 - st_scr[3])))
    thr = st_scr[0]
    clo = st_scr[2]

    @pl.when(jnp.max(clo) > kf)
    def _():
        tie = clo > kf
        cgt, = count([lambda t, j: t > thr[None]])
        need = kf - cgt
        n_it = int(np.ceil(np.log2(seq_len))) + 1
        kidx = (lax.broadcasted_iota(jnp.int32, (grp, SUBLANES, tq), 0) * SUBLANES
                + lax.broadcasted_iota(jnp.int32, (grp, SUBLANES, tq), 1)).astype(F32)

        def jb(_, carry):
            jlo, jhi = carry
            jm = jnp.floor((jlo + jhi) * 0.5)
            f, = count([lambda t, j: jnp.logical_and(t == thr[None], kidx + (j * tk).astype(F32) <= jm[None])])
            ok = f >= need
            return jnp.where(ok, jlo, jm), jnp.where(ok, jm, jhi)

        _, jcut = lax.fori_loop(0, n_it, jb, (jnp.full((SUBLANES, tq), -1.0, F32),
                                              jnp.full((SUBLANES, tq), float(seq_len - 1), F32)))

        def drop(j, _):
            t3 = s_scr[j].reshape(grp, SUBLANES, tq)
            late = kidx + (j * tk).astype(F32) > jcut[None]
            kill = jnp.logical_and(tie[None], jnp.logical_and(t3 == thr[None], late))
            s_scr[j] = jnp.where(kill, -jnp.inf, t3).reshape(tk, tq)
            return 0

        lax.fori_loop(0, i + 1, drop, 0)

    lane = lax.broadcasted_iota(jnp.int32, (tq, LANES), 1)
    for h in range(DSA_HEADS):
        slab = q_ref[0, :, (h // 2) * LANES:(h // 2 + 1) * LANES]
        mine = (lane >= DSA_HEAD_DIM) if (h % 2) else (lane < DSA_HEAD_DIM)
        qm_scr[h] = jnp.where(mine, slab, jnp.zeros_like(slab))
        m_scr[h] = jnp.full((tq, LANES), NEG_BIG, F32)
        l_scr[h] = jnp.zeros((tq, LANES), F32)
        acc_scr[h] = jnp.zeros((tq, LANES), F32)

    def att_body(j, _):
        t3 = s_scr[j].reshape(grp, SUBLANES, tq)
        bias = jnp.where(t3 >= thr[None], 0.0, NEG_BIG).reshape(tk, tq).T
        row0 = pl.multiple_of(j * tk, tk)
        for h in range(DSA_HEADS):
            sl = slice((h // 2) * LANES, (h // 2 + 1) * LANES)
            s = _dot(qm_scr[h], kt_ref[0, j, sl, :]) + bias
            m_prev = m_scr[h]
            m_new = jnp.maximum(m_prev, jnp.max(s, axis=1, keepdims=True))
            p = jnp.exp(s - jnp.concatenate([m_new] * nsl, axis=1))
            corr = jnp.exp(m_prev - m_new)
            l_scr[h] = corr * l_scr[h] + jnp.sum(p, axis=1, keepdims=True)
            pv = _dot(p.astype(BF16), v_ref[0, pl.ds(row0, tk), sl])
            acc_scr[h] = acc_scr[h] * corr + pv
            m_scr[h] = m_new
        return 0

    lax.fori_loop(0, i + 1, att_body, 0)

    for sp in range(DSA_HEADS // 2):
        even = acc_scr[2 * sp] / l_scr[2 * sp]
        odd = acc_scr[2 * sp + 1] / l_scr[2 * sp + 1]
        o_ref[0, :, sp * LANES:(sp + 1) * LANES] = jnp.where(lane < DSA_HEAD_DIM, even, odd).astype(o_ref.dtype)


def _dsa(iqt, ik, iwt, q, kt, v):
    B, T, _ = q.shape
    tq = DSA_TILE
    nq = T // tq
    topk = min(DSA_TOPK_MAX, T // 4)
    kern = functools.partial(_dsa_kernel, topk=topk, seq_len=T)
    return pl.pallas_call(
        kern,
        grid=(B, nq),
        in_specs=[
            pl.BlockSpec((1, 1, IDX_DIM, IDX_HEADS * tq), lambda b, i: (b, i, 0, 0)),
            pl.BlockSpec((1, T, IDX_DIM), lambda b, i: (b, 0, 0)),
            pl.BlockSpec((1, IDX_HEADS, tq), lambda b, i: (b, 0, i)),
            pl.BlockSpec((1, tq, DSA_WIDTH), lambda b, i: (b, i, 0)),
            pl.BlockSpec((1, nq, DSA_WIDTH, tq), lambda b, i: (b, 0, 0, 0)),
            pl.BlockSpec((1, T, DSA_WIDTH), lambda b, i: (b, 0, 0)),
        ],
        out_specs=pl.BlockSpec((1, tq, DSA_WIDTH), lambda b, i: (b, i, 0)),
        out_shape=jax.ShapeDtypeStruct((B, T, DSA_WIDTH), BF16),
        scratch_shapes=[
            pltpu.VMEM((nq, tq, tq), F32),
            pltpu.VMEM((DSA_HEADS, tq, LANES), BF16),
            pltpu.VMEM((4, SUBLANES, tq), F32),
            pltpu.VMEM((DSA_HEADS, tq, LANES), F32),
            pltpu.VMEM((DSA_HEADS, tq, LANES), F32),
            pltpu.VMEM((DSA_HEADS, tq, LANES), F32),
        ],
        compiler_params=_params(("parallel", "arbitrary")),
    )(iqt, ik, iwt, q, kt, v)


def _gla_kernel(q_ref, k_ref, v_ref, r_ref, la_ref, g_ref, o_ref, s_scr):
    rows = GLA_BLOCK
    nch = rows // CHUNK
    npair = GLA_HEADS // 2

    @pl.when(pl.program_id(1) == 0)
    def _():
        s_scr[...] = jnp.zeros_like(s_scr)

    ri = lax.broadcasted_iota(jnp.int32, (rows, rows), 0)
    ci = lax.broadcasted_iota(jnp.int32, (rows, rows), 1)
    causal = jnp.logical_and((ri >> 6) == (ci >> 6), ci <= ri)
    tri = jnp.where(causal, 1.0, 0.0).astype(BF16)

    la = la_ref[0]
    la_hi = la.astype(BF16)
    rem = la - la_hi.astype(F32)
    la_mid = rem.astype(BF16)
    la_lo = (rem - la_mid.astype(F32)).astype(BF16)
    b = _dot(tri, la_hi) + _dot(tri, la_mid) + _dot(tri, la_lo)
    b_last = jnp.concatenate(
        [jnp.broadcast_to(b[(c + 1) * CHUNK - 1:(c + 1) * CHUNK, :], (CHUNK, b.shape[1])) for c in range(nch)], axis=0)

    q = q_ref[0]
    k = k_ref[0]
    qg = q * jnp.exp(b)
    kg = (k * jnp.exp(-b)).astype(BF16)
    kd = (k * jnp.exp(b_last - b)).astype(BF16)
    decay_rows = jnp.exp(b_last)

    lane = lax.broadcasted_iota(jnp.int32, (rows, LANES), 1)
    eye = lax.broadcasted_iota(jnp.int32, (LANES, LANES), 0) == lax.broadcasted_iota(jnp.int32, (LANES, LANES), 1)
    top_rows = lax.broadcasted_iota(jnp.int32, (LANES, GLA_DV), 0) < GLA_DK

    for pr in range(npair):
        sl = slice(pr * LANES, (pr + 1) * LANES)
        qg_p = qg[:, sl]
        kg_p = kg[:, sl]
        kd_p = kd[:, sl]
        heads = (2 * pr, 2 * pr + 1)
        qg_h = [jnp.where(lane < GLA_DK, qg_p, 0.0).astype(BF16), jnp.where(lane >= GLA_DK, qg_p, 0.0).astype(BF16)]
        v_h = [v_ref[0, :, h * GLA_DV:(h + 1) * GLA_DV].astype(BF16) for h in heads]
        o_h = []
        for t in range(2):
            a = jnp.where(causal, _dot_nt(qg_h[t], kg_p), 0.0)
            o_h.append(_dot(a.astype(BF16), v_h[t]))
        state = s_scr[pr]
        inter = [[], []]
        for c in range(nch):
            rs = slice(c * CHUNK, (c + 1) * CHUNK)
            sb = state.astype(BF16)
            for t in range(2):
                inter[t].append(_dot(qg_h[t][rs], sb))
            kv = jnp.where(top_rows, _dot_tn(kd_p[rs], v_h[0][rs]), _dot_tn(kd_p[rs], v_h[1][rs]))
            drow = jnp.broadcast_to(decay_rows[c * CHUNK:c * CHUNK + 1, sl], (LANES, LANES))
            dcol = jnp.sum(jnp.where(eye, drow, 0.0), axis=1, keepdims=True)
            state = dcol * state + kv
        s_scr[pr] = state
        for t in range(2):
            h = heads[t]
            o = o_h[t] + jnp.concatenate(inter[t], axis=0)
            ms = jnp.mean(o * o, axis=1, keepdims=True)
            on = o * lax.rsqrt(ms + LN_EPS) * g_ref[...]
            r = r_ref[0, :, h * GLA_DV:(h + 1) * GLA_DV]
            o_ref[0, :, h * GLA_DV:(h + 1) * GLA_DV] = (on * (r / (1.0 + jnp.exp(-r)))).astype(o_ref.dtype)


def _gla(bq, bk, bv, br, la, g):
    B, T, _ = bq.shape
    rows = GLA_BLOCK
    blk = lambda w: pl.BlockSpec((1, rows, w), lambda b, i: (b, i, 0))
    return pl.pallas_call(
        _gla_kernel,
        grid=(B, T // rows),
        in_specs=[blk(256), blk(256), blk(512), blk(512), blk(256), pl.BlockSpec((1, GLA_DV), lambda b, i: (0, 0))],
        out_specs=blk(GLA_WIDTH),
        out_shape=jax.ShapeDtypeStruct((B, T, GLA_WIDTH), BF16),
        scratch_shapes=[pltpu.VMEM((GLA_HEADS // 2, 2 * GLA_DK, GLA_DV), F32)],
        compiler_params=_params(("parallel", "arbitrary")),
    )(bq, bk, bv, br, la, g)


def _layer_norm(x, g, b):
    mu = jnp.mean(x, axis=1, keepdims=True)
    xc = x - mu
    var = jnp.mean(xc * xc, axis=1, keepdims=True)
    return xc * lax.rsqrt(var + LN_EPS) * g + b


def _out_proj_kernel(x_ref, ya_ref, yb_ref, wo_ref, g_ref, b_ref, wr_hi_ref, wr_lo_ref, br_ref, h_ref, route_ref):
    tm = x_ref.shape[0]
    mix = _dot(ya_ref[...], wo_ref[0:DSA_WIDTH, :]) + _dot(yb_ref[...], wo_ref[DSA_WIDTH:D_MODEL, :])
    h = _layer_norm(DEEPNORM_ALPHA * x_ref[...] + mix, g_ref[...], b_ref[...])
    h_ref[...] = h

    h_hi = h.astype(BF16)
    h_lo = (h - h_hi.astype(F32)).astype(BF16)
    logits = _dot(h_hi, wr_hi_ref[...]) + _dot(h_lo, wr_hi_ref[...]) + _dot(h_hi, wr_lo_ref[...]) + br_ref[...]

    lane = lax.broadcasted_iota(jnp.int32, (tm, LANES), 1)
    lanef = lane.astype(F32)
    gl = jnp.where(lane < N_GROUPS, logits, -jnp.inf)
    gmax = jnp.max(gl, axis=1, keepdims=True)
    gsel = jnp.min(jnp.where(gl == gmax, lanef, 1e9), axis=1, keepdims=True)
    pg = 1.0 / jnp.sum(jnp.exp(gl - gmax), axis=1, keepdims=True)
    egrp = ((lane - N_GROUPS) >> 3).astype(F32)
    in_grp = jnp.logical_and(jnp.logical_and(lane >= N_GROUPS, lane < N_GROUPS + N_EXPERTS), egrp == gsel)
    el = jnp.where(in_grp, logits, -jnp.inf)
    t1 = jnp.max(el, axis=1, keepdims=True)
    i1 = jnp.min(jnp.where(el == t1, lanef, 1e9), axis=1, keepdims=True)
    el2 = jnp.where(lanef == i1, -jnp.inf, el)
    t2 = jnp.max(el2, axis=1, keepdims=True)
    i2 = jnp.min(jnp.where(el2 == t2, lanef, 1e9), axis=1, keepdims=True)
    e21 = jnp.exp(t2 - t1)
    g1 = pg / (1.0 + e21)
    g2 = pg * e21 / (1.0 + e21)
    route = jnp.where(lane == 0, i1 - N_GROUPS, 0.0)
    route = jnp.where(lane == 1, i2 - N_GROUPS, route)
    route = jnp.where(lane == 2, g1, route)
    route = jnp.where(lane == 3, g2, route)
    route_ref[...] = route


def _out_proj(x2, ya, yb, wo, g1, b1, wr_hi, wr_lo, br):
    N = x2.shape[0]
    tm = TOK_TILE
    row = lambda i: (i, 0)
    const = lambda i: (0, 0)
    return pl.pallas_call(
        _out_proj_kernel,
        grid=(N // tm,),
        in_specs=[
            pl.BlockSpec((tm, D_MODEL), row), pl.BlockSpec((tm, DSA_WIDTH), row), pl.BlockSpec((tm, GLA_WIDTH), row),
            pl.BlockSpec(wo.shape, const), pl.BlockSpec(g1.shape, const), pl.BlockSpec(b1.shape, const),
            pl.BlockSpec(wr_hi.shape, const), pl.BlockSpec(wr_lo.shape, const), pl.BlockSpec(br.shape, const),
        ],
        out_specs=[pl.BlockSpec((tm, D_MODEL), row), pl.BlockSpec((tm, LANES), row)],
        out_shape=[jax.ShapeDtypeStruct((N, D_MODEL), F32), jax.ShapeDtypeStruct((N, LANES), F32)],
        compiler_params=_params(("parallel",)),
    )(x2, ya, yb, wo, g1, b1, wr_hi, wr_lo, br)


def _rank_kernel(route_ref, rank_ref, cnt_ref, carry_scr):
    tm = route_ref.shape[0]

    @pl.when(pl.program_id(0) == 0)
    def _():
        carry_scr[...] = jnp.zeros_like(carry_scr)

    route = route_ref[...]
    lanef = lax.broadcasted_iota(jnp.int32, (tm, LANES), 1).astype(F32)
    e1 = route[:, 0:1]
    e2 = route[:, 1:2]
    hit1 = lanef == e1
    hit2 = lanef == e2
    onehot = jnp.where(jnp.logical_or(hit1, hit2), 1.0, 0.0).astype(BF16)
    ri = lax.broadcasted_iota(jnp.int32, (tm, tm), 0)
    ci = lax.broadcasted_iota(jnp.int32, (tm, tm), 1)
    before = jnp.where(ci < ri, 1.0, 0.0).astype(BF16)
    prefix = _dot(before, onehot) + carry_scr[0:1, :]
    r1 = jnp.sum(jnp.where(hit1, prefix, 0.0), axis=1, keepdims=True)
    r2 = jnp.sum(jnp.where(hit2, prefix, 0.0), axis=1, keepdims=True)
    rank_ref[...] = jnp.where(lanef == 0.0, r1, jnp.where(lanef == 1.0, r2, 0.0))
    total = _dot(jnp.ones((8, tm), BF16), onehot)
    carry_scr[...] = carry_scr[...] + total
    cnt_ref[...] = carry_scr[...]


def _rank(route):
    N = route.shape[0]
    tm = RANK_TILE
    return pl.pallas_call(
        _rank_kernel,
        grid=(N // tm,),
        in_specs=[pl.BlockSpec((tm, LANES), lambda i: (i, 0))],
        out_specs=[pl.BlockSpec((tm, LANES), lambda i: (i, 0)), pl.BlockSpec((8, LANES), lambda i: (0, 0))],
        out_shape=[jax.ShapeDtypeStruct((N, LANES), F32), jax.ShapeDtypeStruct((8, LANES), F32)],
        scratch_shapes=[pltpu.VMEM((8, LANES), F32)],
        compiler_params=_params(("arbitrary",)),
    )(route)


def _dispatch_kernel(pos_ref, h_ref, xs_in_ref, xs_ref, sem):
    del xs_in_ref
    tm = h_ref.shape[0]
    base = pl.program_id(0) * tm

    def row_copy(r, slot):
        p = pos_ref[2 * (base + r) + slot]
        return pltpu.make_async_copy(h_ref.at[pl.ds(r, 1), :], xs_ref.at[pl.ds(p, 1), :], sem)

    def start(r, _):
        row_copy(r, 0).start()
        row_copy(r, 1).start()
        return 0

    def wait(r, _):
        row_copy(r, 0).wait()
        row_copy(r, 1).wait()
        return 0

    lax.fori_loop(0, tm, start, 0)
    lax.fori_loop(0, tm, wait, 0)


def _dispatch(pos_flat, h, xs_init):
    N = h.shape[0]
    tm = TOK_TILE
    grid_spec = pltpu.PrefetchScalarGridSpec(
        num_scalar_prefetch=1,
        grid=(N // tm,),
        in_specs=[pl.BlockSpec((tm, D_MODEL), lambda i, pos: (i, 0)), pl.BlockSpec(memory_space=pl.ANY)],
        out_specs=pl.BlockSpec(memory_space=pl.ANY),
        scratch_shapes=[pltpu.SemaphoreType.DMA(())],
    )
    return pl.pallas_call(
        _dispatch_kernel,
        grid_spec=grid_spec,
        out_shape=jax.ShapeDtypeStruct(xs_init.shape, xs_init.dtype),
        input_output_aliases={2: 0},
        compiler_params=_params(("arbitrary",)),
    )(pos_flat, h, xs_init)


def _ffn_kernel(te_ref, na_ref, x_ref, wi_ref, wo_ref, y_ref):
    active = pl.program_id(0) < na_ref[0]

    @pl.when(active)
    def _():
        hid = _dot(x_ref[...].astype(BF16), wi_ref[0])
        hg = hid[:, :D_EXPERT]
        hu = hid[:, D_EXPERT:]
        act = (hg / (1.0 + jnp.exp(-hg))) * hu
        y_ref[...] = _dot(act.astype(BF16), wo_ref[0])

    @pl.when(jnp.logical_not(active))
    def _():
        y_ref[...] = jnp.zeros_like(y_ref)


def _ffn(tile_expert, n_active, xs, w_e_in, w_e_out):
    R = xs.shape[0]
    tm = MOE_TILE

    def live(i, te, na):
        return jnp.minimum(i, na[0] - 1)

    grid_spec = pltpu.PrefetchScalarGridSpec(
        num_scalar_prefetch=2,
        grid=(R // tm,),
        in_specs=[
            pl.BlockSpec((tm, D_MODEL), lambda i, te, na: (live(i, te, na), 0)),
            pl.BlockSpec((1, D_MODEL, 2 * D_EXPERT), lambda i, te, na: (te[live(i, te, na)], 0, 0)),
            pl.BlockSpec((1, D_EXPERT, D_MODEL), lambda i, te, na: (te[live(i, te, na)], 0, 0)),
        ],
        out_specs=pl.BlockSpec((tm, D_MODEL), lambda i, te, na: (i, 0)),
    )
    return pl.pallas_call(
        _ffn_kernel,
        grid_spec=grid_spec,
        out_shape=jax.ShapeDtypeStruct((R, D_MODEL), F32),
        compiler_params=_params(("arbitrary",)),
    )(tile_expert, n_active, xs, w_e_in, w_e_out)


def _combine_kernel(pos_ref, h_ref, route_ref, g_ref, b_ref, ys_ref, o_ref, buf, sem):
    tm = h_ref.shape[0]
    base = pl.program_id(0) * tm

    def row_copy(r, slot):
        p = pos_ref[2 * (base + r) + slot]
        return pltpu.make_async_copy(ys_ref.at[pl.ds(p, 1), :], buf.at[slot, pl.ds(r, 1), :], sem)

    def start(r, _):
        row_copy(r, 0).start()
        row_copy(r, 1).start()
        return 0

    def wait(r, _):
        row_copy(r, 0).wait()
        row_copy(r, 1).wait()
        return 0

    lax.fori_loop(0, tm, start, 0)
    lax.fori_loop(0, tm, wait, 0)
    route = route_ref[...]
    ffn = buf[0] * route[:, 2:3] + buf[1] * route[:, 3:4]
    o_ref[...] = _layer_norm(DEEPNORM_ALPHA * h_ref[...] + ffn, g_ref[...], b_ref[...])


def _combine(pos_flat, h, route, g2, b2, ys):
    N = h.shape[0]
    tm = TOK_TILE
    row = lambda i, pos: (i, 0)
    const = lambda i, pos: (0, 0)
    grid_spec = pltpu.PrefetchScalarGridSpec(
        num_scalar_prefetch=1,
        grid=(N // tm,),
        in_specs=[
            pl.BlockSpec((tm, D_MODEL), row), pl.BlockSpec((tm, LANES), row),
            pl.BlockSpec(g2.shape, const), pl.BlockSpec(b2.shape, const),
            pl.BlockSpec(memory_space=pl.ANY),
        ],
        out_specs=pl.BlockSpec((tm, D_MODEL), row),
        scratch_shapes=[pltpu.VMEM((2, tm, D_MODEL), F32), pltpu.SemaphoreType.DMA(())],
    )
    return pl.pallas_call(
        _combine_kernel,
        grid_spec=grid_spec,
        out_shape=jax.ShapeDtypeStruct((N, D_MODEL), F32),
        compiler_params=_params(("arbitrary",)),
    )(pos_flat, h, route, g2, b2, ys)


def _rope_tables(T, dim):
    half = dim // 2
    inv = 1.0 / (ROPE_THETA ** (jnp.arange(half, dtype=F32) / half))
    ang = jnp.arange(T).astype(F32)[:, None] * inv[None, :]
    cos = jnp.cos(ang)
    sin = jnp.sin(ang)
    reps = LANES // dim
    cos_t = jnp.tile(jnp.concatenate([cos, cos], axis=1), (1, reps))
    sin_t = jnp.tile(jnp.concatenate([-sin, sin], axis=1), (1, reps))
    return cos_t, sin_t


def _pad_cols(w, width):
    return jnp.pad(w, ((0, 0), (0, width - w.shape[1])))


def _layer(x, w_in, w_gla_gate, b_gla_gate, g_gla_norm, w_out, ln1_g, ln1_b,
           w_gr, b_gr, w_er, b_er, w_e_in, w_e_out, ln2_g, ln2_b):
    B, T, D = x.shape
    N = B * T
    assert D == D_MODEL and T % DSA_TILE == 0 and N % RANK_TILE == 0
    x2 = x.reshape(N, D)

    sizes = (512, 512, 512, 256, 32, 8, 256, 256, 512, 512, 16)
    offs = np.concatenate([[0], np.cumsum(sizes)])
    col = lambda k: w_in[:, offs[k]:offs[k + 1]]
    wa = jnp.concatenate([col(0) * (DSA_HEAD_DIM ** -0.5), col(1), col(2)], axis=1).astype(BF16)
    wi = _pad_cols(jnp.concatenate([col(3), col(4), col(5) * IDX_SCALE], axis=1), 3 * LANES).astype(BF16)
    wb = jnp.concatenate([col(6) * (GLA_DK ** -0.5), col(7), col(8), col(9)], axis=1).astype(BF16)
    wg = _pad_cols(col(10), LANES).astype(BF16)
    wgate = jnp.pad(w_gla_gate, ((0, LANES - GLA_GATE_RANK), (0, 0))).astype(BF16)
    bgate = b_gla_gate.reshape(1, -1)
    cosa, sina = _rope_tables(T, DSA_HEAD_DIM)
    cosi, sini = _rope_tables(T, IDX_DIM)

    q, k, v, iq, ikw, bq, bk, bv, br, la = _in_proj(x2, wa, wi, wb, wg, wgate, bgate, cosa, sina, cosi, sini, T)

    nq = T // DSA_TILE
    iqt = iq.reshape(B, nq, DSA_TILE, IDX_HEADS, IDX_DIM).transpose(0, 1, 4, 3, 2)
    iqt = iqt.reshape(B, nq, IDX_DIM, IDX_HEADS * DSA_TILE)
    ik = ikw[:, :IDX_DIM].astype(BF16).reshape(B, T, IDX_DIM)
    iwt = ikw[:, IDX_DIM:IDX_DIM + IDX_HEADS].reshape(B, T, IDX_HEADS).transpose(0, 2, 1)
    kt = k.reshape(B, nq, DSA_TILE, DSA_WIDTH).transpose(0, 1, 3, 2)
    ya = _dsa(iqt, ik, iwt, q.reshape(B, T, DSA_WIDTH), kt, v.reshape(B, T, DSA_WIDTH))

    r3 = lambda a: a.reshape(B, T, a.shape[-1])
    yb = _gla(r3(bq), r3(bk), r3(bv), r3(br), r3(la), g_gla_norm.reshape(1, GLA_DV))

    wr = _pad_cols(jnp.concatenate([w_gr, w_er], axis=1), LANES)
    wr_hi = wr.astype(BF16)
    wr_lo = (wr - wr_hi.astype(F32)).astype(BF16)
    brt = _pad_cols(jnp.concatenate([b_gr, b_er]).reshape(1, -1), LANES)
    h, route = _out_proj(x2, ya.reshape(N, DSA_WIDTH), yb.reshape(N, GLA_WIDTH), w_out.astype(BF16),
                         ln1_g.reshape(1, D), ln1_b.reshape(1, D), wr_hi, wr_lo, brt)

    rank, cnt = _rank(route)
    counts = cnt[0, :N_EXPERTS].astype(jnp.int32)
    padded = ((counts + MOE_TILE - 1) // MOE_TILE) * MOE_TILE
    ends = jnp.cumsum(padded)
    starts = ends - padded
    eid = route[:, 0:2].astype(jnp.int32)
    pos = (starts[eid] + rank[:, 0:2].astype(jnp.int32)).reshape(-1)
    n_rows = 2 * N + N_EXPERTS * MOE_TILE
    n_tiles = n_rows // MOE_TILE
    tile_start = jnp.arange(n_tiles, dtype=jnp.int32) * MOE_TILE
    tile_expert = jnp.minimum(jnp.sum(tile_start[:, None] >= ends[None, :], axis=1), N_EXPERTS - 1).astype(jnp.int32)
    n_active = (ends[-1] // MOE_TILE).astype(jnp.int32).reshape(1)

    xs = _dispatch(pos, h, jnp.zeros((n_rows, D), F32))
    ys = _ffn(tile_expert, n_active, xs, w_e_in.astype(BF16), w_e_out.astype(BF16))
    out = _combine(pos, h, route, ln2_g.reshape(1, D), ln2_b.reshape(1, D), ys)
    return out.reshape(B, T, D)


def kernel(x, w_in, w_gla_gate, b_gla_gate, g_gla_norm, w_out, ln1_g, ln1_b, w_group_router, b_group_router,
           w_expert_router, b_expert_router, w_expert_in, w_expert_out, ln2_g, ln2_b):
    h = x
    for l in range(w_in.shape[0]):
        h = _layer(h, w_in[l], w_gla_gate[l], b_gla_gate[l], g_gla_norm[l], w_out[l], ln1_g[l], ln1_b[l],
                   w_group_router[l], b_group_router[l], w_expert_router[l], b_expert_router[l],
                   w_expert_in[l], w_expert_out[l], ln2_g[l], ln2_b[l])
    return h
```

```python
import functools

import numpy as np
import jax
import jax.numpy as jnp
from jax import lax
from jax.experimental import pallas as pl
from jax.experimental.pallas import tpu as pltpu

F32 = jnp.float32
BF16 = jnp.bfloat16

D_MODEL = 1024
CHUNK = 64
ROPE_THETA = 10000.0
LN_EPS = 1e-5
DSA_WIDTH = 512
DSA_HEAD_DIM = 64
DSA_HEADS = 8
IDX_HEADS = 8
IDX_DIM = 32
IDX_SCALE = (IDX_HEADS * IDX_DIM) ** -0.5
DSA_TOPK_MAX = 256
GLA_WIDTH = 512
GLA_HEADS = 4
GLA_DV = 128
GLA_DK = 64
GLA_GATE_RANK = 16
GLA_TAU = 16.0
N_GROUPS = 4
EXPERTS_PER_GROUP = 8
N_EXPERTS = 32
D_EXPERT = 512
DEEPNORM_ALPHA = 2.0 ** 0.25

LANES = 128
SUBLANES = 8
TINY = 2.0 ** -126
NEG_BIG = -1e30
VMEM_LIMIT = 56 * 1024 * 1024

TOK_TILE = 256
DSA_TILE = 256
GLA_BLOCK = 256
MOE_TILE = 256
RANK_TILE = 512
BISECT_ARITH = 40
BISECT_CAP = 80


def _dot(a, b):
    return jnp.dot(a, b, preferred_element_type=F32)


def _dot_nt(a, b):
    return lax.dot_general(a, b, (((1,), (1,)), ((), ())), preferred_element_type=F32)


def _dot_tn(a, b):
    return lax.dot_general(a, b, (((0,), (0,)), ((), ())), preferred_element_type=F32)


def _params(sem):
    return pltpu.CompilerParams(dimension_semantics=sem, vmem_limit_bytes=VMEM_LIMIT)


def _rope_slab(slab, cos, sin, first_half, half):
    swapped = jnp.where(first_half, pltpu.roll(slab, LANES - half, 1), pltpu.roll(slab, half, 1))
    return slab * cos + swapped * sin


def _in_proj_kernel(x_ref, wa_ref, wi_ref, wb_ref, wg_ref, wgate_ref, bgate_ref,
                    cosa_ref, sina_ref, cosi_ref, sini_ref,
                    q_ref, k_ref, v_ref, iq_ref, ikw_ref, bq_ref, bk_ref, bv_ref, br_ref, la_ref):
    tm = x_ref.shape[0]
    xb = x_ref[...].astype(BF16)
    lane = lax.broadcasted_iota(jnp.int32, (tm, LANES), 1)

    a = _dot(xb, wa_ref[...])
    cosa, sina = cosa_ref[...], sina_ref[...]
    first_a = (lane & (DSA_HEAD_DIM - 1)) < DSA_HEAD_DIM // 2
    for c in range(DSA_WIDTH // LANES):
        sl = slice(c * LANES, (c + 1) * LANES)
        q_ref[:, sl] = _rope_slab(a[:, sl], cosa, sina, first_a, DSA_HEAD_DIM // 2).astype(BF16)
        ks = slice(DSA_WIDTH + c * LANES, DSA_WIDTH + (c + 1) * LANES)
        k_ref[:, sl] = _rope_slab(a[:, ks], cosa, sina, first_a, DSA_HEAD_DIM // 2).astype(BF16)
    v_ref[...] = a[:, 2 * DSA_WIDTH:3 * DSA_WIDTH].astype(BF16)

    ii = _dot(xb, wi_ref[...])
    cosi, sini = cosi_ref[...], sini_ref[...]
    first_i = (lane & (IDX_DIM - 1)) < IDX_DIM // 2
    for c in range(2):
        sl = slice(c * LANES, (c + 1) * LANES)
        iq_ref[:, sl] = _rope_slab(ii[:, sl], cosi, sini, first_i, IDX_DIM // 2).astype(BF16)
    last = ii[:, 2 * LANES:3 * LANES]
    ikw_ref[...] = jnp.where(lane < IDX_DIM, _rope_slab(last, cosi, sini, first_i, IDX_DIM // 2), last)

    b = _dot(xb, wb_ref[...])
    bq_ref[...] = b[:, 0:256]
    bk_ref[...] = b[:, 256:512]
    bv_ref[...] = b[:, 512:1024]
    br_ref[...] = b[:, 1024:1536]

    g = _dot(xb, wg_ref[...])
    z = _dot(g.astype(BF16), wgate_ref[...]) + bgate_ref[...]
    log_sig = jnp.minimum(z, 0.0) - jnp.log(1.0 + jnp.exp(-jnp.abs(z)))
    la_ref[...] = log_sig * (1.0 / GLA_TAU)


def _in_proj(x2, wa, wi, wb, wg, wgate, bgate, cosa, sina, cosi, sini, T):
    N = x2.shape[0]
    tm = TOK_TILE
    nt = T // tm
    row = lambda i: (i, 0)
    const = lambda i: (0, 0)
    pos = lambda i: (i % nt, 0)
    outs = [
        (DSA_WIDTH, BF16), (DSA_WIDTH, BF16), (DSA_WIDTH, BF16), (IDX_HEADS * IDX_DIM, BF16), (LANES, F32),
        (256, F32), (256, F32), (512, F32), (512, F32), (256, F32),
    ]
    return pl.pallas_call(
        _in_proj_kernel,
        grid=(N // tm,),
        in_specs=[
            pl.BlockSpec((tm, D_MODEL), row),
            pl.BlockSpec(wa.shape, const), pl.BlockSpec(wi.shape, const), pl.BlockSpec(wb.shape, const),
            pl.BlockSpec(wg.shape, const), pl.BlockSpec(wgate.shape, const), pl.BlockSpec(bgate.shape, const),
            pl.BlockSpec((tm, LANES), pos), pl.BlockSpec((tm, LANES), pos),
            pl.BlockSpec((tm, LANES), pos), pl.BlockSpec((tm, LANES), pos),
        ],
        out_specs=[pl.BlockSpec((tm, w), row) for w, _ in outs],
        out_shape=[jax.ShapeDtypeStruct((N, w), dt) for w, dt in outs],
        compiler_params=_params(("parallel",)),
    )(x2, wa, wi, wb, wg, wgate, bgate, cosa, sina, cosi, sini)


def _dsa_kernel(iqt_ref, ik_ref, iwt_ref, qt_ref, k_ref, vt_ref, o_ref,
                s_scr, qm_scr, st_scr, m_scr, l_scr, acc_scr, sn_scr, *, topk, seq_len):
    tq = DSA_TILE
    tk = DSA_TILE
    grp = tk // SUBLANES
    i = pl.program_id(1)
    kf = jnp.float32(topk)

    iqt = iqt_ref[0, 0]
    iwt = iwt_ref[0]
    krow = lax.broadcasted_iota(jnp.int32, (tk, tq), 0)
    qcol = lax.broadcasted_iota(jnp.int32, (tk, tq), 1)
    qcol8 = lax.broadcasted_iota(jnp.int32, (SUBLANES, tq), 1)

    def tree(x3, op):
        q4 = grp // 4
        return op(jnp.stack([op(x3[a * q4:(a + 1) * q4], axis=0) for a in range(4)]), axis=0)

    def fold(x, op):
        return tree(x.reshape(grp, SUBLANES, tq), op)

    def spread(x):
        return jnp.broadcast_to(x, (SUBLANES, tq))

    def score_tile(j):
        keys = ik_ref[0, pl.ds(pl.multiple_of(j * tk, tk), tk), :]
        rt = _dot(keys, iqt)
        acc = None
        for h in range(IDX_HEADS):
            term = jnp.maximum(rt[:, h * tq:(h + 1) * tq], 0.0) * iwt[h:h + 1, :]
            acc = term if acc is None else acc + term
        return acc

    def score_body(j, carry):
        rmax, rmin = carry
        s = score_tile(j)
        s_scr[j] = s
        return jnp.maximum(rmax, fold(s, jnp.max)), jnp.minimum(rmin, fold(s, jnp.min))

    init = (jnp.full((SUBLANES, tq), -jnp.inf, F32), jnp.full((SUBLANES, tq), jnp.inf, F32))
    rmax, rmin = lax.fori_loop(0, i, score_body, init)
    s = score_tile(i)
    adm = (krow >> 6) <= (qcol >> 6)
    s_scr[i] = jnp.where(adm, s, -jnp.inf)
    rmax = jnp.maximum(rmax, fold(jnp.where(adm, s, -jnp.inf), jnp.max))
    rmin = jnp.minimum(rmin, fold(jnp.where(adm, s, jnp.inf), jnp.min))
    rmax = spread(jnp.max(rmax, axis=0, keepdims=True))
    rmin = spread(jnp.min(rmin, axis=0, keepdims=True))

    def count(preds):
        def body(j, accs):
            t3 = s_scr[j].reshape(grp, SUBLANES, tq)
            return tuple(a + tree(jnp.where(p(t3, j), 1.0, 0.0), jnp.sum) for a, p in zip(accs, preds))
        accs = lax.fori_loop(0, i + 1, body, tuple(jnp.zeros((SUBLANES, tq), F32) for _ in preds))
        return [spread(jnp.sum(a, axis=0, keepdims=True)) for a in accs]

    n_adm = (((qcol8 >> 6) + 1 + i * (tq // CHUNK)) * CHUNK).astype(F32)
    cpos, cnn = count([lambda t, j: t > 0.0, lambda t, j: t >= 0.0])
    search = n_adm > kf
    positive = cpos >= kf
    negative = cnn < kf
    lo = jnp.where(positive, TINY, jnp.where(negative, rmin, 0.0))
    clo = jnp.where(positive, cpos, jnp.where(negative, n_adm, cnn))
    st_scr[0] = jnp.where(search, lo, rmin)
    st_scr[1] = jnp.where(negative, -TINY, rmax + (jnp.abs(rmax) * (2.0 ** -10) + TINY))
    st_scr[2] = jnp.where(search, clo, n_adm)
    zero_thr = jnp.logical_and(jnp.logical_not(positive), jnp.logical_not(negative))
    st_scr[3] = jnp.where(jnp.logical_or(jnp.logical_not(search), jnp.logical_or(zero_thr, clo == kf)), 1.0, 0.0)

    def bis_cond(carry):
        it, pending = carry
        return jnp.logical_and(pending > 0.0, it < BISECT_CAP)

    def bis_body(carry):
        it, _ = carry
        lo, hi, clo, done = st_scr[0], st_scr[1], st_scr[2], st_scr[3]
        a = lax.bitcast_convert_type(jnp.abs(lo), jnp.int32)
        b = lax.bitcast_convert_type(jnp.abs(hi), jnp.int32)
        geo = lax.bitcast_convert_type(a + ((b - a) >> 1), F32)
        geo = jnp.where(hi > 0.0, geo, -geo)
        mid = jnp.where(it < BISECT_ARITH, lo + (hi - lo) * 0.5, geo)
        stuck = jnp.logical_or(mid <= lo, mid >= hi)
        cnt, = count([lambda t, j: t >= mid[None]])
        live = jnp.logical_and(done == 0.0, jnp.logical_not(stuck))
        up = jnp.logical_and(live, cnt >= kf)
        dn = jnp.logical_and(live, cnt < kf)
        lo = jnp.where(up, mid, lo)
        clo = jnp.where(up, cnt, clo)
        hi = jnp.where(dn, mid, hi)
        done = jnp.where(jnp.logical_or(stuck, clo == kf), 1.0, done)
        st_scr[0], st_scr[1], st_scr[2], st_scr[3] = lo, hi, clo, done
        return it + 1, jnp.max(1.0 - done)

    lax.while_loop(bis_cond, bis_body, (jnp.int32(0), jnp.max(1.0 - st_scr[3])))
    thr = st_scr[0]
    clo = st_scr[2]

    @pl.when(jnp.max(clo) > kf)
    def _():
        tie = clo > kf
        cgt, = count([lambda t, j: t > thr[None]])
        need = kf - cgt
        n_it = int(np.ceil(np.log2(seq_len))) + 1
        kidx = (lax.broadcasted_iota(jnp.int32, (grp, SUBLANES, tq), 0) * SUBLANES
                + lax.broadcasted_iota(jnp.int32, (grp, SUBLANES, tq), 1)).astype(F32)

        def jb(_, carry):
            jlo, jhi = carry
            jm = jnp.floor((jlo + jhi) * 0.5)
            f, = count([lambda t, j: jnp.logical_and(t == thr[None], kidx + (j * tk).astype(F32) <= jm[None])])
            ok = f >= need
            return jnp.where(ok, jlo, jm), jnp.where(ok, jm, jhi)

        _, jcut = lax.fori_loop(0, n_it, jb, (jnp.full((SUBLANES, tq), -1.0, F32),
                                              jnp.full((SUBLANES, tq), float(seq_len - 1), F32)))

        def drop(j, _):
            t3 = s_scr[j].reshape(grp, SUBLANES, tq)
            late = kidx + (j * tk).astype(F32) > jcut[None]
            kill = jnp.logical_and(tie[None], jnp.logical_and(t3 == thr[None], late))
            s_scr[j] = jnp.where(kill, -jnp.inf, t3).reshape(tk, tq)
            return 0

        lax.fori_loop(0, i + 1, drop, 0)

    hrow = lax.broadcasted_iota(jnp.int32, (LANES, tq), 0)
    for h in range(DSA_HEADS):
        slab = qt_ref[0, (h // 2) * LANES:(h // 2 + 1) * LANES, :]
        mine = (hrow >= DSA_HEAD_DIM) if (h % 2) else (hrow < DSA_HEAD_DIM)
        qm_scr[h] = jnp.where(mine, slab, jnp.zeros_like(slab))
        m_scr[h] = jnp.full((SUBLANES, tq), NEG_BIG, F32)
        l_scr[h] = jnp.zeros((SUBLANES, tq), F32)
        acc_scr[h] = jnp.zeros((LANES, tq), F32)

    def slab(h):
        return slice((h // 2) * LANES, (h // 2 + 1) * LANES)

    def tile_operands(j):
        t3 = s_scr[j].reshape(grp, SUBLANES, tq)
        bias = jnp.where(t3 >= thr[None], 0.0, NEG_BIG).reshape(tk, tq)
        return k_ref[0, pl.ds(pl.multiple_of(j * tk, tk), tk), :], bias

    def masked_scores(keys, bias, h):
        return _dot(keys[:, slab(h)], qm_scr[h]) + bias

    keys0, bias0 = tile_operands(0)
    for h in range(DSA_HEADS):
        sn_scr[h] = masked_scores(keys0, bias0, h)

    def att_body(j, _):
        keys_n, bias_n = tile_operands(jnp.minimum(j + 1, i))
        for h in range(DSA_HEADS):
            s3 = sn_scr[h].reshape(grp, SUBLANES, tq)
            m_prev = m_scr[h]
            m_new = jnp.maximum(m_prev, spread(jnp.max(tree(s3, jnp.max), axis=0, keepdims=True)))
            p3 = jnp.exp(s3 - m_new[None])
            corr = jnp.exp(m_prev - m_new)
            l_scr[h] = corr * l_scr[h] + tree(p3, jnp.sum)
            pv = _dot(vt_ref[0, j, slab(h), :], p3.reshape(tk, tq).astype(BF16))
            acc = acc_scr[h].reshape(LANES // SUBLANES, SUBLANES, tq) * corr[None]
            acc_scr[h] = acc.reshape(LANES, tq) + pv
            m_scr[h] = m_new
            sn_scr[h] = masked_scores(keys_n, bias_n, h)
        return 0

    lax.fori_loop(0, i + 1, att_body, 0)

    def normalised(h):
        inv = 1.0 / spread(jnp.sum(l_scr[h], axis=0, keepdims=True))
        return (acc_scr[h].reshape(LANES // SUBLANES, SUBLANES, tq) * inv[None]).reshape(LANES, tq)

    for sp in range(DSA_HEADS // 2):
        pair = jnp.where(hrow < DSA_HEAD_DIM, normalised(2 * sp), normalised(2 * sp + 1))
        o_ref[0, sp * LANES:(sp + 1) * LANES, :] = pair.astype(o_ref.dtype)


def _dsa(iqt, ik, iwt, qt, k, vt):
    B, T, _ = k.shape
    tq = DSA_TILE
    nq = T // tq
    topk = min(DSA_TOPK_MAX, T // 4)
    kern = functools.partial(_dsa_kernel, topk=topk, seq_len=T)
    return pl.pallas_call(
        kern,
        grid=(B, nq),
        in_specs=[
            pl.BlockSpec((1, 1, IDX_DIM, IDX_HEADS * tq), lambda b, i: (b, i, 0, 0)),
            pl.BlockSpec((1, T, IDX_DIM), lambda b, i: (b, 0, 0)),
            pl.BlockSpec((1, IDX_HEADS, tq), lambda b, i: (b, 0, i)),
            pl.BlockSpec((1, DSA_WIDTH, tq), lambda b, i: (b, 0, i)),
            pl.BlockSpec((1, T, DSA_WIDTH), lambda b, i: (b, 0, 0)),
            pl.BlockSpec((1, nq, DSA_WIDTH, tq), lambda b, i: (b, 0, 0, 0)),
        ],
        out_specs=pl.BlockSpec((1, DSA_WIDTH, tq), lambda b, i: (b, 0, i)),
        out_shape=jax.ShapeDtypeStruct((B, DSA_WIDTH, T), BF16),
        scratch_shapes=[
            pltpu.VMEM((nq, tq, tq), F32),
            pltpu.VMEM((DSA_HEADS, LANES, tq), BF16),
            pltpu.VMEM((4, SUBLANES, tq), F32),
            pltpu.VMEM((DSA_HEADS, SUBLANES, tq), F32),
            pltpu.VMEM((DSA_HEADS, SUBLANES, tq), F32),
            pltpu.VMEM((DSA_HEADS, LANES, tq), F32),
            pltpu.VMEM((DSA_HEADS, tq, tq), F32),
        ],
        compiler_params=_params(("parallel", "arbitrary")),
    )(iqt, ik, iwt, qt, k, vt)


def _gla_kernel(q_ref, k_ref, v_ref, r_ref, la_ref, g_ref, o_ref, s_scr):
    rows = GLA_BLOCK
    nch = rows // CHUNK
    npair = GLA_HEADS // 2

    @pl.when(pl.program_id(1) == 0)
    def _():
        s_scr[...] = jnp.zeros_like(s_scr)

    ri = lax.broadcasted_iota(jnp.int32, (rows, rows), 0)
    ci = lax.broadcasted_iota(jnp.int32, (rows, rows), 1)
    causal = jnp.logical_and((ri >> 6) == (ci >> 6), ci <= ri)
    tri = jnp.where(causal, 1.0, 0.0).astype(BF16)

    la = la_ref[0]
    la_hi = la.astype(BF16)
    rem = la - la_hi.astype(F32)
    la_mid = rem.astype(BF16)
    la_lo = (rem - la_mid.astype(F32)).astype(BF16)
    b = _dot(tri, la_hi) + _dot(tri, la_mid) + _dot(tri, la_lo)
    b_last = jnp.concatenate(
        [jnp.broadcast_to(b[(c + 1) * CHUNK - 1:(c + 1) * CHUNK, :], (CHUNK, b.shape[1])) for c in range(nch)], axis=0)

    q = q_ref[0]
    k = k_ref[0]
    qg = q * jnp.exp(b)
    kg = (k * jnp.exp(-b)).astype(BF16)
    kd = (k * jnp.exp(b_last - b)).astype(BF16)
    decay_rows = jnp.exp(b_last)

    lane = lax.broadcasted_iota(jnp.int32, (rows, LANES), 1)
    eye = lax.broadcasted_iota(jnp.int32, (LANES, LANES), 0) == lax.broadcasted_iota(jnp.int32, (LANES, LANES), 1)
    top_rows = lax.broadcasted_iota(jnp.int32, (LANES, GLA_DV), 0) < GLA_DK

    for pr in range(npair):
        sl = slice(pr * LANES, (pr + 1) * LANES)
        qg_p = qg[:, sl]
        kg_p = kg[:, sl]
        kd_p = kd[:, sl]
        heads = (2 * pr, 2 * pr + 1)
        qg_h = [jnp.where(lane < GLA_DK, qg_p, 0.0).astype(BF16), jnp.where(lane >= GLA_DK, qg_p, 0.0).astype(BF16)]
        v_h = [v_ref[0, :, h * GLA_DV:(h + 1) * GLA_DV].astype(BF16) for h in heads]
        o_h = []
        for t in range(2):
            a = jnp.where(causal, _dot_nt(qg_h[t], kg_p), 0.0)
            o_h.append(_dot(a.astype(BF16), v_h[t]))
        state = s_scr[pr]
        inter = [[], []]
        for c in range(nch):
            rs = slice(c * CHUNK, (c + 1) * CHUNK)
            sb = state.astype(BF16)
            for t in range(2):
                inter[t].append(_dot(qg_h[t][rs], sb))
            kv = jnp.where(top_rows, _dot_tn(kd_p[rs], v_h[0][rs]), _dot_tn(kd_p[rs], v_h[1][rs]))
            drow = jnp.broadcast_to(decay_rows[c * CHUNK:c * CHUNK + 1, sl], (LANES, LANES))
            dcol = jnp.sum(jnp.where(eye, drow, 0.0), axis=1, keepdims=True)
            state = dcol * state + kv
        s_scr[pr] = state
        for t in range(2):
            h = heads[t]
            o = o_h[t] + jnp.concatenate(inter[t], axis=0)
            ms = jnp.mean(o * o, axis=1, keepdims=True)
            on = o * lax.rsqrt(ms + LN_EPS) * g_ref[...]
            r = r_ref[0, :, h * GLA_DV:(h + 1) * GLA_DV]
            o_ref[0, :, h * GLA_DV:(h + 1) * GLA_DV] = (on * (r / (1.0 + jnp.exp(-r)))).astype(o_ref.dtype)


def _gla(bq, bk, bv, br, la, g):
    B, T, _ = bq.shape
    rows = GLA_BLOCK
    blk = lambda w: pl.BlockSpec((1, rows, w), lambda b, i: (b, i, 0))
    return pl.pallas_call(
        _gla_kernel,
        grid=(B, T // rows),
        in_specs=[blk(256), blk(256), blk(512), blk(512), blk(256), pl.BlockSpec((1, GLA_DV), lambda b, i: (0, 0))],
        out_specs=blk(GLA_WIDTH),
        out_shape=jax.ShapeDtypeStruct((B, T, GLA_WIDTH), BF16),
        scratch_shapes=[pltpu.VMEM((GLA_HEADS // 2, 2 * GLA_DK, GLA_DV), F32)],
        compiler_params=_params(("parallel", "arbitrary")),
    )(bq, bk, bv, br, la, g)


def _layer_norm(x, g, b):
    mu = jnp.mean(x, axis=1, keepdims=True)
    xc = x - mu
    var = jnp.mean(xc * xc, axis=1, keepdims=True)
    return xc * lax.rsqrt(var + LN_EPS) * g + b


def _out_proj_kernel(x_ref, ya_ref, yb_ref, wo_ref, g_ref, b_ref, wr_hi_ref, wr_lo_ref, br_ref, h_ref, route_ref):
    tm = x_ref.shape[0]
    mix = _dot(ya_ref[...], wo_ref[0:DSA_WIDTH, :]) + _dot(yb_ref[...], wo_ref[DSA_WIDTH:D_MODEL, :])
    h = _layer_norm(DEEPNORM_ALPHA * x_ref[...] + mix, g_ref[...], b_ref[...])
    h_ref[...] = h

    h_hi = h.astype(BF16)
    h_lo = (h - h_hi.astype(F32)).astype(BF16)
    logits = _dot(h_hi, wr_hi_ref[...]) + _dot(h_lo, wr_hi_ref[...]) + _dot(h_hi, wr_lo_ref[...]) + br_ref[...]

    lane = lax.broadcasted_iota(jnp.int32, (tm, LANES), 1)
    lanef = lane.astype(F32)
    gl = jnp.where(lane < N_GROUPS, logits, -jnp.inf)
    gmax = jnp.max(gl, axis=1, keepdims=True)
    gsel = jnp.min(jnp.where(gl == gmax, lanef, 1e9), axis=1, keepdims=True)
    pg = 1.0 / jnp.sum(jnp.exp(gl - gmax), axis=1, keepdims=True)
    egrp = ((lane - N_GROUPS) >> 3).astype(F32)
    in_grp = jnp.logical_and(jnp.logical_and(lane >= N_GROUPS, lane < N_GROUPS + N_EXPERTS), egrp == gsel)
    el = jnp.where(in_grp, logits, -jnp.inf)
    t1 = jnp.max(el, axis=1, keepdims=True)
    i1 = jnp.min(jnp.where(el == t1, lanef, 1e9), axis=1, keepdims=True)
    el2 = jnp.where(lanef == i1, -jnp.inf, el)
    t2 = jnp.max(el2, axis=1, keepdims=True)
    i2 = jnp.min(jnp.where(el2 == t2, lanef, 1e9), axis=1, keepdims=True)
    e21 = jnp.exp(t2 - t1)
    g1 = pg / (1.0 + e21)
    g2 = pg * e21 / (1.0 + e21)
    route = jnp.where(lane == 0, i1 - N_GROUPS, 0.0)
    route = jnp.where(lane == 1, i2 - N_GROUPS, route)
    route = jnp.where(lane == 2, g1, route)
    route = jnp.where(lane == 3, g2, route)
    route_ref[...] = route


def _out_proj(x2, ya, yb, wo, g1, b1, wr_hi, wr_lo, br):
    N = x2.shape[0]
    tm = TOK_TILE
    row = lambda i: (i, 0)
    const = lambda i: (0, 0)
    return pl.pallas_call(
        _out_proj_kernel,
        grid=(N // tm,),
        in_specs=[
            pl.BlockSpec((tm, D_MODEL), row), pl.BlockSpec((tm, DSA_WIDTH), row), pl.BlockSpec((tm, GLA_WIDTH), row),
            pl.BlockSpec(wo.shape, const), pl.BlockSpec(g1.shape, const), pl.BlockSpec(b1.shape, const),
            pl.BlockSpec(wr_hi.shape, const), pl.BlockSpec(wr_lo.shape, const), pl.BlockSpec(br.shape, const),
        ],
        out_specs=[pl.BlockSpec((tm, D_MODEL), row), pl.BlockSpec((tm, LANES), row)],
        out_shape=[jax.ShapeDtypeStruct((N, D_MODEL), F32), jax.ShapeDtypeStruct((N, LANES), F32)],
        compiler_params=_params(("parallel",)),
    )(x2, ya, yb, wo, g1, b1, wr_hi, wr_lo, br)


def _rank_kernel(route_ref, rank_ref, cnt_ref, carry_scr):
    tm = route_ref.shape[0]

    @pl.when(pl.program_id(0) == 0)
    def _():
        carry_scr[...] = jnp.zeros_like(carry_scr)

    route = route_ref[...]
    lanef = lax.broadcasted_iota(jnp.int32, (tm, LANES), 1).astype(F32)
    e1 = route[:, 0:1]
    e2 = route[:, 1:2]
    hit1 = lanef == e1
    hit2 = lanef == e2
    onehot = jnp.where(jnp.logical_or(hit1, hit2), 1.0, 0.0).astype(BF16)
    ri = lax.broadcasted_iota(jnp.int32, (tm, tm), 0)
    ci = lax.broadcasted_iota(jnp.int32, (tm, tm), 1)
    before = jnp.where(ci < ri, 1.0, 0.0).astype(BF16)
    prefix = _dot(before, onehot) + carry_scr[0:1, :]
    r1 = jnp.sum(jnp.where(hit1, prefix, 0.0), axis=1, keepdims=True)
    r2 = jnp.sum(jnp.where(hit2, prefix, 0.0), axis=1, keepdims=True)
    rank_ref[...] = jnp.where(lanef == 0.0, r1, jnp.where(lanef == 1.0, r2, 0.0))
    total = _dot(jnp.ones((8, tm), BF16), onehot)
    carry_scr[...] = carry_scr[...] + total
    cnt_ref[...] = carry_scr[...]


def _rank(route):
    N = route.shape[0]
    tm = RANK_TILE
    return pl.pallas_call(
        _rank_kernel,
        grid=(N // tm,),
        in_specs=[pl.BlockSpec((tm, LANES), lambda i: (i, 0))],
        out_specs=[pl.BlockSpec((tm, LANES), lambda i: (i, 0)), pl.BlockSpec((8, LANES), lambda i: (0, 0))],
        out_shape=[jax.ShapeDtypeStruct((N, LANES), F32), jax.ShapeDtypeStruct((8, LANES), F32)],
        scratch_shapes=[pltpu.VMEM((8, LANES), F32)],
        compiler_params=_params(("arbitrary",)),
    )(route)


def _dispatch_kernel(pos_ref, h_ref, xs_in_ref, xs_ref, sem):
    del xs_in_ref
    tm = h_ref.shape[0]
    base = pl.program_id(0) * tm

    def row_copy(r, slot):
        p = pos_ref[2 * (base + r) + slot]
        return pltpu.make_async_copy(h_ref.at[pl.ds(r, 1), :], xs_ref.at[pl.ds(p, 1), :], sem)

    def start(r, _):
        row_copy(r, 0).start()
        row_copy(r, 1).start()
        return 0

    def wait(r, _):
        row_copy(r, 0).wait()
        row_copy(r, 1).wait()
        return 0

    lax.fori_loop(0, tm, start, 0)
    lax.fori_loop(0, tm, wait, 0)


def _dispatch(pos_flat, h, xs_init):
    N = h.shape[0]
    tm = TOK_TILE
    grid_spec = pltpu.PrefetchScalarGridSpec(
        num_scalar_prefetch=1,
        grid=(N // tm,),
        in_specs=[pl.BlockSpec((tm, D_MODEL), lambda i, pos: (i, 0)), pl.BlockSpec(memory_space=pl.ANY)],
        out_specs=pl.BlockSpec(memory_space=pl.ANY),
        scratch_shapes=[pltpu.SemaphoreType.DMA(())],
    )
    return pl.pallas_call(
        _dispatch_kernel,
        grid_spec=grid_spec,
        out_shape=jax.ShapeDtypeStruct(xs_init.shape, xs_init.dtype),
        input_output_aliases={2: 0},
        compiler_params=_params(("arbitrary",)),
    )(pos_flat, h, xs_init)


def _ffn_kernel(te_ref, na_ref, x_ref, wi_ref, wo_ref, y_ref):
    active = pl.program_id(0) < na_ref[0]

    @pl.when(active)
    def _():
        hid = _dot(x_ref[...].astype(BF16), wi_ref[0])
        hg = hid[:, :D_EXPERT]
        hu = hid[:, D_EXPERT:]
        act = (hg / (1.0 + jnp.exp(-hg))) * hu
        y_ref[...] = _dot(act.astype(BF16), wo_ref[0])

    @pl.when(jnp.logical_not(active))
    def _():
        y_ref[...] = jnp.zeros_like(y_ref)


def _ffn(tile_expert, n_active, xs, w_e_in, w_e_out):
    R = xs.shape[0]
    tm = MOE_TILE

    def live(i, te, na):
        return jnp.minimum(i, na[0] - 1)

    grid_spec = pltpu.PrefetchScalarGridSpec(
        num_scalar_prefetch=2,
        grid=(R // tm,),
        in_specs=[
            pl.BlockSpec((tm, D_MODEL), lambda i, te, na: (live(i, te, na), 0)),
            pl.BlockSpec((1, D_MODEL, 2 * D_EXPERT), lambda i, te, na: (te[live(i, te, na)], 0, 0)),
            pl.BlockSpec((1, D_EXPERT, D_MODEL), lambda i, te, na: (te[live(i, te, na)], 0, 0)),
        ],
        out_specs=pl.BlockSpec((tm, D_MODEL), lambda i, te, na: (i, 0)),
    )
    return pl.pallas_call(
        _ffn_kernel,
        grid_spec=grid_spec,
        out_shape=jax.ShapeDtypeStruct((R, D_MODEL), F32),
        compiler_params=_params(("arbitrary",)),
    )(tile_expert, n_active, xs, w_e_in, w_e_out)


def _combine_kernel(pos_ref, h_ref, route_ref, g_ref, b_ref, ys_ref, o_ref, buf, sem):
    tm = h_ref.shape[0]
    base = pl.program_id(0) * tm

    def row_copy(r, slot):
        p = pos_ref[2 * (base + r) + slot]
        return pltpu.make_async_copy(ys_ref.at[pl.ds(p, 1), :], buf.at[slot, pl.ds(r, 1), :], sem)

    def start(r, _):
        row_copy(r, 0).start()
        row_copy(r, 1).start()
        return 0

    def wait(r, _):
        row_copy(r, 0).wait()
        row_copy(r, 1).wait()
        return 0

    lax.fori_loop(0, tm, start, 0)
    lax.fori_loop(0, tm, wait, 0)
    route = route_ref[...]
    ffn = buf[0] * route[:, 2:3] + buf[1] * route[:, 3:4]
    o_ref[...] = _layer_norm(DEEPNORM_ALPHA * h_ref[...] + ffn, g_ref[...], b_ref[...])


def _combine(pos_flat, h, route, g2, b2, ys):
    N = h.shape[0]
    tm = TOK_TILE
    row = lambda i, pos: (i, 0)
    const = lambda i, pos: (0, 0)
    grid_spec = pltpu.PrefetchScalarGridSpec(
        num_scalar_prefetch=1,
        grid=(N // tm,),
        in_specs=[
            pl.BlockSpec((tm, D_MODEL), row), pl.BlockSpec((tm, LANES), row),
            pl.BlockSpec(g2.shape, const), pl.BlockSpec(b2.shape, const),
            pl.BlockSpec(memory_space=pl.ANY),
        ],
        out_specs=pl.BlockSpec((tm, D_MODEL), row),
        scratch_shapes=[pltpu.VMEM((2, tm, D_MODEL), F32), pltpu.SemaphoreType.DMA(())],
    )
    return pl.pallas_call(
        _combine_kernel,
        grid_spec=grid_spec,
        out_shape=jax.ShapeDtypeStruct((N, D_MODEL), F32),
        compiler_params=_params(("arbitrary",)),
    )(pos_flat, h, route, g2, b2, ys)


def _rope_tables(T, dim):
    half = dim // 2
    inv = 1.0 / (ROPE_THETA ** (jnp.arange(half, dtype=F32) / half))
    ang = jnp.arange(T).astype(F32)[:, None] * inv[None, :]
    cos = jnp.cos(ang)
    sin = jnp.sin(ang)
    reps = LANES // dim
    cos_t = jnp.tile(jnp.concatenate([cos, cos], axis=1), (1, reps))
    sin_t = jnp.tile(jnp.concatenate([-sin, sin], axis=1), (1, reps))
    return cos_t, sin_t


def _pad_cols(w, width):
    return jnp.pad(w, ((0, 0), (0, width - w.shape[1])))


def _layer(x, w_in, w_gla_gate, b_gla_gate, g_gla_norm, w_out, ln1_g, ln1_b,
           w_gr, b_gr, w_er, b_er, w_e_in, w_e_out, ln2_g, ln2_b):
    B, T, D = x.shape
    N = B * T
    assert D == D_MODEL and T % DSA_TILE == 0 and N % RANK_TILE == 0
    x2 = x.reshape(N, D)

    sizes = (512, 512, 512, 256, 32, 8, 256, 256, 512, 512, 16)
    offs = np.concatenate([[0], np.cumsum(sizes)])
    col = lambda k: w_in[:, offs[k]:offs[k + 1]]
    wa = jnp.concatenate([col(0) * (DSA_HEAD_DIM ** -0.5), col(1), col(2)], axis=1).astype(BF16)
    wi = _pad_cols(jnp.concatenate([col(3), col(4), col(5) * IDX_SCALE], axis=1), 3 * LANES).astype(BF16)
    wb = jnp.concatenate([col(6) * (GLA_DK ** -0.5), col(7), col(8), col(9)], axis=1).astype(BF16)
    wg = _pad_cols(col(10), LANES).astype(BF16)
    wgate = jnp.pad(w_gla_gate, ((0, LANES - GLA_GATE_RANK), (0, 0))).astype(BF16)
    bgate = b_gla_gate.reshape(1, -1)
    cosa, sina = _rope_tables(T, DSA_HEAD_DIM)
    cosi, sini = _rope_tables(T, IDX_DIM)

    q, k, v, iq, ikw, bq, bk, bv, br, la = _in_proj(x2, wa, wi, wb, wg, wgate, bgate, cosa, sina, cosi, sini, T)

    nq = T // DSA_TILE
    iqt = iq.reshape(B, nq, DSA_TILE, IDX_HEADS, IDX_DIM).transpose(0, 1, 4, 3, 2)
    iqt = iqt.reshape(B, nq, IDX_DIM, IDX_HEADS * DSA_TILE)
    ik = ikw[:, :IDX_DIM].astype(BF16).reshape(B, T, IDX_DIM)
    iwt = ikw[:, IDX_DIM:IDX_DIM + IDX_HEADS].reshape(B, T, IDX_HEADS).transpose(0, 2, 1)
    qt = q.reshape(B, T, DSA_WIDTH).transpose(0, 2, 1)
    vt = v.reshape(B, nq, DSA_TILE, DSA_WIDTH).transpose(0, 1, 3, 2)
    ya = _dsa(iqt, ik, iwt, qt, k.reshape(B, T, DSA_WIDTH), vt).transpose(0, 2, 1)

    r3 = lambda a: a.reshape(B, T, a.shape[-1])
    yb = _gla(r3(bq), r3(bk), r3(bv), r3(br), r3(la), g_gla_norm.reshape(1, GLA_DV))

    wr = _pad_cols(jnp.concatenate([w_gr, w_er], axis=1), LANES)
    wr_hi = wr.astype(BF16)
    wr_lo = (wr - wr_hi.astype(F32)).astype(BF16)
    brt = _pad_cols(jnp.concatenate([b_gr, b_er]).reshape(1, -1), LANES)
    h, route = _out_proj(x2, ya.reshape(N, DSA_WIDTH), yb.reshape(N, GLA_WIDTH), w_out.astype(BF16),
                         ln1_g.reshape(1, D), ln1_b.reshape(1, D), wr_hi, wr_lo, brt)

    rank, cnt = _rank(route)
    counts = cnt[0, :N_EXPERTS].astype(jnp.int32)
    padded = ((counts + MOE_TILE - 1) // MOE_TILE) * MOE_TILE
    ends = jnp.cumsum(padded)
    starts = ends - padded
    eid = route[:, 0:2].astype(jnp.int32)
    pos = (starts[eid] + rank[:, 0:2].astype(jnp.int32)).reshape(-1)
    n_rows = 2 * N + N_EXPERTS * MOE_TILE
    n_tiles = n_rows // MOE_TILE
    tile_start = jnp.arange(n_tiles, dtype=jnp.int32) * MOE_TILE
    tile_expert = jnp.minimum(jnp.sum(tile_start[:, None] >= ends[None, :], axis=1), N_EXPERTS - 1).astype(jnp.int32)
    n_active = (ends[-1] // MOE_TILE).astype(jnp.int32).reshape(1)

    xs = _dispatch(pos, h, jnp.zeros((n_rows, D), F32))
    ys = _ffn(tile_expert, n_active, xs, w_e_in.astype(BF16), w_e_out.astype(BF16))
    out = _combine(pos, h, route, ln2_g.reshape(1, D), ln2_b.reshape(1, D), ys)
    return out.reshape(B, T, D)


def kernel(x, w_in, w_gla_gate, b_gla_gate, g_gla_norm, w_out, ln1_g, ln1_b, w_group_router, b_group_router,
           w_expert_router, b_expert_router, w_expert_in, w_expert_out, ln2_g, ln2_b):
    h = x
    for l in range(w_in.shape[0]):
        h = _layer(h, w_in[l], w_gla_gate[l], b_gla_gate[l], g_gla_norm[l], w_out[l], ln1_g[l], ln1_b[l],
                   w_group_router[l], b_group_router[l], w_expert_router[l], b_expert_router[l],
                   w_expert_in[l], w_expert_out[l], ln2_g[l], ln2_b[l])
    return h
```

```python
import functools

import numpy as np
import jax
import jax.numpy as jnp
from jax import lax
from jax.experimental import pallas as pl
from jax.experimental.pallas import tpu as pltpu

F32 = jnp.float32
BF16 = jnp.bfloat16

D_MODEL = 1024
CHUNK = 64
ROPE_THETA = 10000.0
LN_EPS = 1e-5
DSA_WIDTH = 512
DSA_HEAD_DIM = 64
DSA_HEADS = 8
IDX_HEADS = 8
IDX_DIM = 32
IDX_SCALE = (IDX_HEADS * IDX_DIM) ** -0.5
DSA_TOPK_MAX = 256
GLA_WIDTH = 512
GLA_HEADS = 4
GLA_DV = 128
GLA_DK = 64
GLA_GATE_RANK = 16
GLA_TAU = 16.0
N_GROUPS = 4
EXPERTS_PER_GROUP = 8
N_EXPERTS = 32
D_EXPERT = 512
DEEPNORM_ALPHA = 2.0 ** 0.25

LANES = 128
SUBLANES = 8
TINY = 2.0 ** -126
NEG_BIG = -1e30
VMEM_LIMIT = 56 * 1024 * 1024

TOK_TILE = 256
DSA_TILE = 256
GLA_BLOCK = 256
MOE_TILE = 256
RANK_TILE = 512
DISPATCH_TILE = 1024
DMA_UNROLL = 8
LOG2E = 1.4426950408889634
BISECT_ARITH = 40
BISECT_CAP = 80


def _dot(a, b):
    return jnp.dot(a, b, preferred_element_type=F32)


def _dot_nt(a, b):
    return lax.dot_general(a, b, (((1,), (1,)), ((), ())), preferred_element_type=F32)


def _dot_tn(a, b):
    return lax.dot_general(a, b, (((0,), (0,)), ((), ())), preferred_element_type=F32)


def _params(sem):
    return pltpu.CompilerParams(dimension_semantics=sem, vmem_limit_bytes=VMEM_LIMIT)


def _rope_slab(slab, cos, sin, first_half, half):
    swapped = jnp.where(first_half, pltpu.roll(slab, LANES - half, 1), pltpu.roll(slab, half, 1))
    return slab * cos + swapped * sin


def _in_proj_kernel(x_ref, wa_ref, wi_ref, wb_ref, wg_ref, wgate_ref, bgate_ref,
                    cosa_ref, sina_ref, cosi_ref, sini_ref,
                    q_ref, k_ref, v_ref, iq_ref, ikw_ref, bq_ref, bk_ref, bv_ref, br_ref, la_ref):
    tm = x_ref.shape[0]
    xb = x_ref[...].astype(BF16)
    lane = lax.broadcasted_iota(jnp.int32, (tm, LANES), 1)

    a = _dot(xb, wa_ref[...])
    cosa, sina = cosa_ref[...], sina_ref[...]
    first_a = (lane & (DSA_HEAD_DIM - 1)) < DSA_HEAD_DIM // 2
    for c in range(DSA_WIDTH // LANES):
        sl = slice(c * LANES, (c + 1) * LANES)
        q_ref[:, sl] = _rope_slab(a[:, sl], cosa, sina, first_a, DSA_HEAD_DIM // 2).astype(BF16)
        ks = slice(DSA_WIDTH + c * LANES, DSA_WIDTH + (c + 1) * LANES)
        k_ref[:, sl] = _rope_slab(a[:, ks], cosa, sina, first_a, DSA_HEAD_DIM // 2).astype(BF16)
    v_ref[...] = a[:, 2 * DSA_WIDTH:3 * DSA_WIDTH].astype(BF16)

    ii = _dot(xb, wi_ref[...])
    cosi, sini = cosi_ref[...], sini_ref[...]
    first_i = (lane & (IDX_DIM - 1)) < IDX_DIM // 2
    for c in range(2):
        sl = slice(c * LANES, (c + 1) * LANES)
        iq_ref[:, sl] = _rope_slab(ii[:, sl], cosi, sini, first_i, IDX_DIM // 2).astype(BF16)
    last = ii[:, 2 * LANES:3 * LANES]
    ikw_ref[...] = jnp.where(lane < IDX_DIM, _rope_slab(last, cosi, sini, first_i, IDX_DIM // 2), last)

    b = _dot(xb, wb_ref[...])
    bq_ref[...] = b[:, 0:256]
    bk_ref[...] = b[:, 256:512]
    bv_ref[...] = b[:, 512:1024]
    br_ref[...] = b[:, 1024:1536]

    g = _dot(xb, wg_ref[...])
    z = _dot(g.astype(BF16), wgate_ref[...]) + bgate_ref[...]
    log_sig = jnp.minimum(z, 0.0) - jnp.log(1.0 + jnp.exp(-jnp.abs(z)))
    la_ref[...] = log_sig * (1.0 / GLA_TAU)


def _in_proj(x2, wa, wi, wb, wg, wgate, bgate, cosa, sina, cosi, sini, T):
    N = x2.shape[0]
    tm = TOK_TILE
    nt = T // tm
    row = lambda i: (i, 0)
    const = lambda i: (0, 0)
    pos = lambda i: (i % nt, 0)
    outs = [
        (DSA_WIDTH, BF16), (DSA_WIDTH, BF16), (DSA_WIDTH, BF16), (IDX_HEADS * IDX_DIM, BF16), (LANES, F32),
        (256, F32), (256, F32), (512, F32), (512, F32), (256, F32),
    ]
    return pl.pallas_call(
        _in_proj_kernel,
        grid=(N // tm,),
        in_specs=[
            pl.BlockSpec((tm, D_MODEL), row),
            pl.BlockSpec(wa.shape, const), pl.BlockSpec(wi.shape, const), pl.BlockSpec(wb.shape, const),
            pl.BlockSpec(wg.shape, const), pl.BlockSpec(wgate.shape, const), pl.BlockSpec(bgate.shape, const),
            pl.BlockSpec((tm, LANES), pos), pl.BlockSpec((tm, LANES), pos),
            pl.BlockSpec((tm, LANES), pos), pl.BlockSpec((tm, LANES), pos),
        ],
        out_specs=[pl.BlockSpec((tm, w), row) for w, _ in outs],
        out_shape=[jax.ShapeDtypeStruct((N, w), dt) for w, dt in outs],
        compiler_params=_params(("parallel",)),
    )(x2, wa, wi, wb, wg, wgate, bgate, cosa, sina, cosi, sini)


def _dsa_kernel(iqt_ref, ik_ref, iwt_ref, qt_ref, k_ref, vt_ref, o_ref,
                s_scr, qm_scr, st_scr, m_scr, l_scr, acc_scr, sn_scr, *, topk, seq_len):
    tq = DSA_TILE
    tk = DSA_TILE
    grp = tk // SUBLANES
    i = pl.program_id(1)
    kf = jnp.float32(topk)

    iqt = iqt_ref[0, 0]
    iwt = iwt_ref[0]
    krow = lax.broadcasted_iota(jnp.int32, (tk, tq), 0)
    qcol = lax.broadcasted_iota(jnp.int32, (tk, tq), 1)
    qcol8 = lax.broadcasted_iota(jnp.int32, (SUBLANES, tq), 1)

    def tree(x3, op):
        q4 = grp // 4
        return op(jnp.stack([op(x3[a * q4:(a + 1) * q4], axis=0) for a in range(4)]), axis=0)

    def fold(x, op):
        return tree(x.reshape(grp, SUBLANES, tq), op)

    def spread(x):
        return jnp.broadcast_to(x, (SUBLANES, tq))

    def score_tile(j):
        keys = ik_ref[0, pl.ds(pl.multiple_of(j * tk, tk), tk), :]
        rt = _dot(keys, iqt)
        acc = None
        for h in range(IDX_HEADS):
            term = jnp.maximum(rt[:, h * tq:(h + 1) * tq], 0.0) * iwt[h:h + 1, :]
            acc = term if acc is None else acc + term
        return acc

    def score_body(j, carry):
        rmax, rmin = carry
        s = score_tile(j)
        s_scr[j] = s
        return jnp.maximum(rmax, fold(s, jnp.max)), jnp.minimum(rmin, fold(s, jnp.min))

    init = (jnp.full((SUBLANES, tq), -jnp.inf, F32), jnp.full((SUBLANES, tq), jnp.inf, F32))
    rmax, rmin = lax.fori_loop(0, i, score_body, init)
    s = score_tile(i)
    adm = (krow >> 6) <= (qcol >> 6)
    s_scr[i] = jnp.where(adm, s, -jnp.inf)
    rmax = jnp.maximum(rmax, fold(jnp.where(adm, s, -jnp.inf), jnp.max))
    rmin = jnp.minimum(rmin, fold(jnp.where(adm, s, jnp.inf), jnp.min))
    rmax = spread(jnp.max(rmax, axis=0, keepdims=True))
    rmin = spread(jnp.min(rmin, axis=0, keepdims=True))

    def count(preds):
        def body(j, accs):
            t3 = s_scr[j].reshape(grp, SUBLANES, tq)
            return tuple(a + tree(jnp.where(p(t3, j), 1.0, 0.0), jnp.sum) for a, p in zip(accs, preds))
        accs = lax.fori_loop(0, i + 1, body, tuple(jnp.zeros((SUBLANES, tq), F32) for _ in preds))
        return [spread(jnp.sum(a, axis=0, keepdims=True)) for a in accs]

    n_adm = (((qcol8 >> 6) + 1 + i * (tq // CHUNK)) * CHUNK).astype(F32)
    cpos, cnn = count([lambda t, j: t > 0.0, lambda t, j: t >= 0.0])
    search = n_adm > kf
    positive = cpos >= kf
    negative = cnn < kf
    lo = jnp.where(positive, TINY, jnp.where(negative, rmin, 0.0))
    clo = jnp.where(positive, cpos, jnp.where(negative, n_adm, cnn))
    st_scr[0] = jnp.where(search, lo, rmin)
    st_scr[1] = jnp.where(negative, -TINY, rmax + (jnp.abs(rmax) * (2.0 ** -10) + TINY))
    st_scr[2] = jnp.where(search, clo, n_adm)
    zero_thr = jnp.logical_and(jnp.logical_not(positive), jnp.logical_not(negative))
    st_scr[3] = jnp.where(jnp.logical_or(jnp.logical_not(search), jnp.logical_or(zero_thr, clo == kf)), 1.0, 0.0)

    def bis_cond(carry):
        it, pending = carry
        return jnp.logical_and(pending > 0.0, it < BISECT_CAP)

    def bis_body(carry):
        it, _ = carry
        lo, hi, clo, done = st_scr[0], st_scr[1], st_scr[2], st_scr[3]
        a = lax.bitcast_convert_type(jnp.abs(lo), jnp.int32)
        b = lax.bitcast_convert_type(jnp.abs(hi), jnp.int32)
        geo = lax.bitcast_convert_type(a + ((b - a) >> 1), F32)
        geo = jnp.where(hi > 0.0, geo, -geo)
        mid = jnp.where(it < BISECT_ARITH, lo + (hi - lo) * 0.5, geo)
        stuck = jnp.logical_or(mid <= lo, mid >= hi)
        cnt, = count([lambda t, j: t >= mid[None]])
        live = jnp.logical_and(done == 0.0, jnp.logical_not(stuck))
        up = jnp.logical_and(live, cnt >= kf)
        dn = jnp.logical_and(live, cnt < kf)
        lo = jnp.where(up, mid, lo)
        clo = jnp.where(up, cnt, clo)
        hi = jnp.where(dn, mid, hi)
        done = jnp.where(jnp.logical_or(stuck, clo == kf), 1.0, done)
        st_scr[0], st_scr[1], st_scr[2], st_scr[3] = lo, hi, clo, done
        return it + 1, jnp.max(1.0 - done)

    lax.while_loop(bis_cond, bis_body, (jnp.int32(0), jnp.max(1.0 - st_scr[3])))
    thr = st_scr[0]
    clo = st_scr[2]

    @pl.when(jnp.max(clo) > kf)
    def _():
        tie = clo > kf
        cgt, = count([lambda t, j: t > thr[None]])
        need = kf - cgt
        n_it = int(np.ceil(np.log2(seq_len))) + 1
        kidx = (lax.broadcasted_iota(jnp.int32, (grp, SUBLANES, tq), 0) * SUBLANES
                + lax.broadcasted_iota(jnp.int32, (grp, SUBLANES, tq), 1)).astype(F32)

        def jb(_, carry):
            jlo, jhi = carry
            jm = jnp.floor((jlo + jhi) * 0.5)
            f, = count([lambda t, j: jnp.logical_and(t == thr[None], kidx + (j * tk).astype(F32) <= jm[None])])
            ok = f >= need
            return jnp.where(ok, jlo, jm), jnp.where(ok, jm, jhi)

        _, jcut = lax.fori_loop(0, n_it, jb, (jnp.full((SUBLANES, tq), -1.0, F32),
                                              jnp.full((SUBLANES, tq), float(seq_len - 1), F32)))

        def drop(j, _):
            t3 = s_scr[j].reshape(grp, SUBLANES, tq)
            late = kidx + (j * tk).astype(F32) > jcut[None]
            kill = jnp.logical_and(tie[None], jnp.logical_and(t3 == thr[None], late))
            s_scr[j] = jnp.where(kill, -jnp.inf, t3).reshape(tk, tq)
            return 0

        lax.fori_loop(0, i + 1, drop, 0)

    hrow = lax.broadcasted_iota(jnp.int32, (LANES, tq), 0)
    for h in range(DSA_HEADS):
        slab = qt_ref[0, (h // 2) * LANES:(h // 2 + 1) * LANES, :]
        mine = (hrow >= DSA_HEAD_DIM) if (h % 2) else (hrow < DSA_HEAD_DIM)
        qm_scr[h] = jnp.where(mine, slab, jnp.zeros_like(slab))
        m_scr[h] = jnp.full((SUBLANES, tq), NEG_BIG, F32)
        l_scr[h] = jnp.zeros((SUBLANES, tq), F32)
        acc_scr[h] = jnp.zeros((LANES, tq), F32)

    def slab(h):
        return slice((h // 2) * LANES, (h // 2 + 1) * LANES)

    def tile_operands(j):
        t3 = s_scr[j].reshape(grp, SUBLANES, tq)
        bias = jnp.where(t3 >= thr[None], 0.0, NEG_BIG).reshape(tk, tq)
        return k_ref[0, pl.ds(pl.multiple_of(j * tk, tk), tk), :], bias

    def masked_scores(keys, bias, h):
        return _dot(keys[:, slab(h)], qm_scr[h]) + bias

    keys0, bias0 = tile_operands(0)
    for h in range(DSA_HEADS):
        sn_scr[h] = masked_scores(keys0, bias0, h)

    def att_body(j, _):
        keys_n, bias_n = tile_operands(jnp.minimum(j + 1, i))
        for h in range(DSA_HEADS):
            s3 = sn_scr[h].reshape(grp, SUBLANES, tq)
            m_prev = m_scr[h]
            m_new = jnp.maximum(m_prev, spread(jnp.max(tree(s3, jnp.max), axis=0, keepdims=True)))
            p3 = jnp.exp2(s3 - m_new[None])
            corr = jnp.exp2(m_prev - m_new)
            l_scr[h] = corr * l_scr[h] + tree(p3, jnp.sum)
            pv = _dot(vt_ref[0, j, slab(h), :], p3.reshape(tk, tq).astype(BF16))
            acc = acc_scr[h].reshape(LANES // SUBLANES, SUBLANES, tq) * corr[None]
            acc_scr[h] = acc.reshape(LANES, tq) + pv
            m_scr[h] = m_new
            sn_scr[h] = masked_scores(keys_n, bias_n, h)
        return 0

    lax.fori_loop(0, i + 1, att_body, 0)

    def normalised(h):
        inv = 1.0 / spread(jnp.sum(l_scr[h], axis=0, keepdims=True))
        return (acc_scr[h].reshape(LANES // SUBLANES, SUBLANES, tq) * inv[None]).reshape(LANES, tq)

    for sp in range(DSA_HEADS // 2):
        pair = jnp.where(hrow < DSA_HEAD_DIM, normalised(2 * sp), normalised(2 * sp + 1))
        o_ref[0, sp * LANES:(sp + 1) * LANES, :] = pair.astype(o_ref.dtype)


def _dsa(iqt, ik, iwt, qt, k, vt):
    B, T, _ = k.shape
    tq = DSA_TILE
    nq = T // tq
    topk = min(DSA_TOPK_MAX, T // 4)
    kern = functools.partial(_dsa_kernel, topk=topk, seq_len=T)
    return pl.pallas_call(
        kern,
        grid=(B, nq),
        in_specs=[
            pl.BlockSpec((1, 1, IDX_DIM, IDX_HEADS * tq), lambda b, i: (b, i, 0, 0)),
            pl.BlockSpec((1, T, IDX_DIM), lambda b, i: (b, 0, 0)),
            pl.BlockSpec((1, IDX_HEADS, tq), lambda b, i: (b, 0, i)),
            pl.BlockSpec((1, DSA_WIDTH, tq), lambda b, i: (b, 0, i)),
            pl.BlockSpec((1, T, DSA_WIDTH), lambda b, i: (b, 0, 0)),
            pl.BlockSpec((1, nq, DSA_WIDTH, tq), lambda b, i: (b, 0, 0, 0)),
        ],
        out_specs=pl.BlockSpec((1, DSA_WIDTH, tq), lambda b, i: (b, 0, i)),
        out_shape=jax.ShapeDtypeStruct((B, DSA_WIDTH, T), BF16),
        scratch_shapes=[
            pltpu.VMEM((nq, tq, tq), F32),
            pltpu.VMEM((DSA_HEADS, LANES, tq), BF16),
            pltpu.VMEM((4, SUBLANES, tq), F32),
            pltpu.VMEM((DSA_HEADS, SUBLANES, tq), F32),
            pltpu.VMEM((DSA_HEADS, SUBLANES, tq), F32),
            pltpu.VMEM((DSA_HEADS, LANES, tq), F32),
            pltpu.VMEM((DSA_HEADS, tq, tq), F32),
        ],
        compiler_params=_params(("parallel", "arbitrary")),
    )(iqt, ik, iwt, qt, k, vt)


def _gla_kernel(q_ref, k_ref, v_ref, r_ref, la_ref, g_ref, o_ref, s_scr):
    rows = GLA_BLOCK
    nch = rows // CHUNK
    npair = GLA_HEADS // 2

    @pl.when(pl.program_id(1) == 0)
    def _():
        s_scr[...] = jnp.zeros_like(s_scr)

    ri = lax.broadcasted_iota(jnp.int32, (rows, rows), 0)
    ci = lax.broadcasted_iota(jnp.int32, (rows, rows), 1)
    causal = jnp.logical_and((ri >> 6) == (ci >> 6), ci <= ri)
    tri = jnp.where(causal, 1.0, 0.0).astype(BF16)

    la = la_ref[0]
    la_hi = la.astype(BF16)
    rem = la - la_hi.astype(F32)
    la_mid = rem.astype(BF16)
    la_lo = (rem - la_mid.astype(F32)).astype(BF16)
    b = _dot(tri, la_hi) + _dot(tri, la_mid) + _dot(tri, la_lo)
    b_last = jnp.concatenate(
        [jnp.broadcast_to(b[(c + 1) * CHUNK - 1:(c + 1) * CHUNK, :], (CHUNK, b.shape[1])) for c in range(nch)], axis=0)

    q = q_ref[0]
    k = k_ref[0]
    qg = q * jnp.exp(b)
    kg = (k * jnp.exp(-b)).astype(BF16)
    kd = (k * jnp.exp(b_last - b)).astype(BF16)
    decay_rows = jnp.exp(b_last)

    lane = lax.broadcasted_iota(jnp.int32, (rows, LANES), 1)
    eye = lax.broadcasted_iota(jnp.int32, (LANES, LANES), 0) == lax.broadcasted_iota(jnp.int32, (LANES, LANES), 1)
    top_rows = lax.broadcasted_iota(jnp.int32, (LANES, GLA_DV), 0) < GLA_DK

    for pr in range(npair):
        sl = slice(pr * LANES, (pr + 1) * LANES)
        qg_p = qg[:, sl]
        kg_p = kg[:, sl]
        kd_p = kd[:, sl]
        heads = (2 * pr, 2 * pr + 1)
        qg_h = [jnp.where(lane < GLA_DK, qg_p, 0.0).astype(BF16), jnp.where(lane >= GLA_DK, qg_p, 0.0).astype(BF16)]
        v_h = [v_ref[0, :, h * GLA_DV:(h + 1) * GLA_DV].astype(BF16) for h in heads]
        o_h = []
        for t in range(2):
            a = jnp.where(causal, _dot_nt(qg_h[t], kg_p), 0.0)
            o_h.append(_dot(a.astype(BF16), v_h[t]))
        state = s_scr[pr]
        inter = [[], []]
        for c in range(nch):
            rs = slice(c * CHUNK, (c + 1) * CHUNK)
            sb = state.astype(BF16)
            for t in range(2):
                inter[t].append(_dot(qg_h[t][rs], sb))
            kv = jnp.where(top_rows, _dot_tn(kd_p[rs], v_h[0][rs]), _dot_tn(kd_p[rs], v_h[1][rs]))
            drow = jnp.broadcast_to(decay_rows[c * CHUNK:c * CHUNK + 1, sl], (LANES, LANES))
            dcol = jnp.sum(jnp.where(eye, drow, 0.0), axis=1, keepdims=True)
            state = dcol * state + kv
        s_scr[pr] = state
        for t in range(2):
            h = heads[t]
            o = o_h[t] + jnp.concatenate(inter[t], axis=0)
            ms = jnp.mean(o * o, axis=1, keepdims=True)
            on = o * lax.rsqrt(ms + LN_EPS) * g_ref[...]
            r = r_ref[0, :, h * GLA_DV:(h + 1) * GLA_DV]
            o_ref[0, :, h * GLA_DV:(h + 1) * GLA_DV] = (on * (r / (1.0 + jnp.exp(-r)))).astype(o_ref.dtype)


def _gla(bq, bk, bv, br, la, g):
    B, T, _ = bq.shape
    rows = GLA_BLOCK
    blk = lambda w: pl.BlockSpec((1, rows, w), lambda b, i: (b, i, 0))
    return pl.pallas_call(
        _gla_kernel,
        grid=(B, T // rows),
        in_specs=[blk(256), blk(256), blk(512), blk(512), blk(256), pl.BlockSpec((1, GLA_DV), lambda b, i: (0, 0))],
        out_specs=blk(GLA_WIDTH),
        out_shape=jax.ShapeDtypeStruct((B, T, GLA_WIDTH), BF16),
        scratch_shapes=[pltpu.VMEM((GLA_HEADS // 2, 2 * GLA_DK, GLA_DV), F32)],
        compiler_params=_params(("parallel", "arbitrary")),
    )(bq, bk, bv, br, la, g)


def _layer_norm(x, g, b):
    mu = jnp.mean(x, axis=1, keepdims=True)
    xc = x - mu
    var = jnp.mean(xc * xc, axis=1, keepdims=True)
    return xc * lax.rsqrt(var + LN_EPS) * g + b


def _out_proj_kernel(x_ref, ya_ref, yb_ref, wo_ref, g_ref, b_ref, wr_hi_ref, wr_lo_ref, br_ref, h_ref, route_ref):
    tm = x_ref.shape[0]
    mix = _dot(ya_ref[...], wo_ref[0:DSA_WIDTH, :]) + _dot(yb_ref[...], wo_ref[DSA_WIDTH:D_MODEL, :])
    h = _layer_norm(DEEPNORM_ALPHA * x_ref[...] + mix, g_ref[...], b_ref[...])
    h_ref[...] = h

    h_hi = h.astype(BF16)
    h_lo = (h - h_hi.astype(F32)).astype(BF16)
    logits = _dot(h_hi, wr_hi_ref[...]) + _dot(h_lo, wr_hi_ref[...]) + _dot(h_hi, wr_lo_ref[...]) + br_ref[...]

    lane = lax.broadcasted_iota(jnp.int32, (tm, LANES), 1)
    lanef = lane.astype(F32)
    gl = jnp.where(lane < N_GROUPS, logits, -jnp.inf)
    gmax = jnp.max(gl, axis=1, keepdims=True)
    gsel = jnp.min(jnp.where(gl == gmax, lanef, 1e9), axis=1, keepdims=True)
    pg = 1.0 / jnp.sum(jnp.exp(gl - gmax), axis=1, keepdims=True)
    egrp = ((lane - N_GROUPS) >> 3).astype(F32)
    in_grp = jnp.logical_and(jnp.logical_and(lane >= N_GROUPS, lane < N_GROUPS + N_EXPERTS), egrp == gsel)
    el = jnp.where(in_grp, logits, -jnp.inf)
    t1 = jnp.max(el, axis=1, keepdims=True)
    i1 = jnp.min(jnp.where(el == t1, lanef, 1e9), axis=1, keepdims=True)
    el2 = jnp.where(lanef == i1, -jnp.inf, el)
    t2 = jnp.max(el2, axis=1, keepdims=True)
    i2 = jnp.min(jnp.where(el2 == t2, lanef, 1e9), axis=1, keepdims=True)
    e21 = jnp.exp(t2 - t1)
    g1 = pg / (1.0 + e21)
    g2 = pg * e21 / (1.0 + e21)
    route = jnp.where(lane == 0, i1 - N_GROUPS, 0.0)
    route = jnp.where(lane == 1, i2 - N_GROUPS, route)
    route = jnp.where(lane == 2, g1, route)
    route = jnp.where(lane == 3, g2, route)
    route_ref[...] = route


def _out_proj(x2, ya, yb, wo, g1, b1, wr_hi, wr_lo, br):
    N = x2.shape[0]
    tm = TOK_TILE
    row = lambda i: (i, 0)
    const = lambda i: (0, 0)
    return pl.pallas_call(
        _out_proj_kernel,
        grid=(N // tm,),
        in_specs=[
            pl.BlockSpec((tm, D_MODEL), row), pl.BlockSpec((tm, DSA_WIDTH), row), pl.BlockSpec((tm, GLA_WIDTH), row),
            pl.BlockSpec(wo.shape, const), pl.BlockSpec(g1.shape, const), pl.BlockSpec(b1.shape, const),
            pl.BlockSpec(wr_hi.shape, const), pl.BlockSpec(wr_lo.shape, const), pl.BlockSpec(br.shape, const),
        ],
        out_specs=[pl.BlockSpec((tm, D_MODEL), row), pl.BlockSpec((tm, LANES), row)],
        out_shape=[jax.ShapeDtypeStruct((N, D_MODEL), F32), jax.ShapeDtypeStruct((N, LANES), F32)],
        compiler_params=_params(("parallel",)),
    )(x2, ya, yb, wo, g1, b1, wr_hi, wr_lo, br)


def _rank_kernel(route_ref, rank_ref, cnt_ref, carry_scr):
    tm = route_ref.shape[0]

    @pl.when(pl.program_id(0) == 0)
    def _():
        carry_scr[...] = jnp.zeros_like(carry_scr)

    route = route_ref[...]
    lanef = lax.broadcasted_iota(jnp.int32, (tm, LANES), 1).astype(F32)
    e1 = route[:, 0:1]
    e2 = route[:, 1:2]
    hit1 = lanef == e1
    hit2 = lanef == e2
    onehot = jnp.where(jnp.logical_or(hit1, hit2), 1.0, 0.0).astype(BF16)
    ri = lax.broadcasted_iota(jnp.int32, (tm, tm), 0)
    ci = lax.broadcasted_iota(jnp.int32, (tm, tm), 1)
    before = jnp.where(ci < ri, 1.0, 0.0).astype(BF16)
    prefix = _dot(before, onehot) + carry_scr[0:1, :]
    r1 = jnp.sum(jnp.where(hit1, prefix, 0.0), axis=1, keepdims=True)
    r2 = jnp.sum(jnp.where(hit2, prefix, 0.0), axis=1, keepdims=True)
    rank_ref[...] = jnp.where(lanef == 0.0, r1, jnp.where(lanef == 1.0, r2, 0.0))
    total = _dot(jnp.ones((8, tm), BF16), onehot)
    carry_scr[...] = carry_scr[...] + total
    cnt_ref[...] = carry_scr[...]


def _rank(route):
    N = route.shape[0]
    tm = RANK_TILE
    return pl.pallas_call(
        _rank_kernel,
        grid=(N // tm,),
        in_specs=[pl.BlockSpec((tm, LANES), lambda i: (i, 0))],
        out_specs=[pl.BlockSpec((tm, LANES), lambda i: (i, 0)), pl.BlockSpec((8, LANES), lambda i: (0, 0))],
        out_shape=[jax.ShapeDtypeStruct((N, LANES), F32), jax.ShapeDtypeStruct((8, LANES), F32)],
        scratch_shapes=[pltpu.VMEM((8, LANES), F32)],
        compiler_params=_params(("arbitrary",)),
    )(route)


def _dispatch_kernel(pos_ref, h_ref, xs_in_ref, xs_ref, sem):
    del xs_in_ref
    tm = DISPATCH_TILE
    i = pl.program_id(0)

    def row_copy(step, r, slot):
        t = step * tm + r
        p = pos_ref[2 * t + slot]
        return pltpu.make_async_copy(h_ref.at[pl.ds(t, 1), :], xs_ref.at[pl.ds(p, 1), :], sem.at[step % 2])

    def for_rows(fn):
        def body(r, _):
            fn(r, 0)
            fn(r, 1)
            return 0
        lax.fori_loop(0, tm, body, 0, unroll=DMA_UNROLL)

    for_rows(lambda r, slot: row_copy(i, r, slot).start())

    @pl.when(i > 0)
    def _():
        for_rows(lambda r, slot: row_copy(i - 1, r, slot).wait())

    @pl.when(i == pl.num_programs(0) - 1)
    def _():
        for_rows(lambda r, slot: row_copy(i, r, slot).wait())


def _dispatch(pos_flat, h, xs_init):
    N = h.shape[0]
    grid_spec = pltpu.PrefetchScalarGridSpec(
        num_scalar_prefetch=1,
        grid=(N // DISPATCH_TILE,),
        in_specs=[pl.BlockSpec(memory_space=pl.ANY), pl.BlockSpec(memory_space=pl.ANY)],
        out_specs=pl.BlockSpec(memory_space=pl.ANY),
        scratch_shapes=[pltpu.SemaphoreType.DMA((2,))],
    )
    return pl.pallas_call(
        _dispatch_kernel,
        grid_spec=grid_spec,
        out_shape=jax.ShapeDtypeStruct(xs_init.shape, xs_init.dtype),
        input_output_aliases={2: 0},
        compiler_params=_params(("arbitrary",)),
    )(pos_flat, h, xs_init)


def _ffn_kernel(te_ref, na_ref, x_ref, wi_ref, wo_ref, y_ref, wi_bf, wo_bf):
    i = pl.program_id(0)
    active = i < na_ref[0]
    fresh = jnp.logical_or(i == 0, te_ref[i] != te_ref[jnp.maximum(i - 1, 0)])

    @pl.when(jnp.logical_and(active, fresh))
    def _():
        wi_bf[...] = wi_ref[0].astype(BF16)
        wo_bf[...] = wo_ref[0].astype(BF16)

    @pl.when(active)
    def _():
        hid = _dot(x_ref[...].astype(BF16), wi_bf[...])
        hg = hid[:, :D_EXPERT]
        hu = hid[:, D_EXPERT:]
        act = (hg / (1.0 + jnp.exp(-hg))) * hu
        y_ref[...] = _dot(act.astype(BF16), wo_bf[...])

    @pl.when(jnp.logical_not(active))
    def _():
        y_ref[...] = jnp.zeros_like(y_ref)


def _ffn(tile_expert, n_active, xs, w_e_in, w_e_out):
    R = xs.shape[0]
    tm = MOE_TILE

    def live(i, te, na):
        return jnp.minimum(i, na[0] - 1)

    grid_spec = pltpu.PrefetchScalarGridSpec(
        num_scalar_prefetch=2,
        grid=(R // tm,),
        in_specs=[
            pl.BlockSpec((tm, D_MODEL), lambda i, te, na: (live(i, te, na), 0)),
            pl.BlockSpec((1, D_MODEL, 2 * D_EXPERT), lambda i, te, na: (te[live(i, te, na)], 0, 0)),
            pl.BlockSpec((1, D_EXPERT, D_MODEL), lambda i, te, na: (te[live(i, te, na)], 0, 0)),
        ],
        out_specs=pl.BlockSpec((tm, D_MODEL), lambda i, te, na: (i, 0)),
        scratch_shapes=[pltpu.VMEM((D_MODEL, 2 * D_EXPERT), BF16), pltpu.VMEM((D_EXPERT, D_MODEL), BF16)],
    )
    return pl.pallas_call(
        _ffn_kernel,
        grid_spec=grid_spec,
        out_shape=jax.ShapeDtypeStruct((R, D_MODEL), F32),
        compiler_params=_params(("arbitrary",)),
    )(tile_expert, n_active, xs, w_e_in, w_e_out)


def _combine_kernel(pos_ref, h_ref, route_ref, g_ref, b_ref, ys_ref, o_ref, buf, sem):
    tm = h_ref.shape[0]
    i = pl.program_id(0)
    cur = i % 2

    def row_copy(step, r, slot):
        p = pos_ref[2 * (step * tm + r) + slot]
        half = step % 2
        return pltpu.make_async_copy(ys_ref.at[pl.ds(p, 1), :], buf.at[half, slot, pl.ds(r, 1), :], sem.at[half])

    def for_rows(fn):
        def body(r, _):
            fn(r, 0)
            fn(r, 1)
            return 0
        lax.fori_loop(0, tm, body, 0, unroll=DMA_UNROLL)

    @pl.when(i == 0)
    def _():
        for_rows(lambda r, slot: row_copy(0, r, slot).start())

    @pl.when(i + 1 < pl.num_programs(0))
    def _():
        for_rows(lambda r, slot: row_copy(i + 1, r, slot).start())

    for_rows(lambda r, slot: row_copy(i, r, slot).wait())
    route = route_ref[...]
    ffn = buf[cur, 0] * route[:, 2:3] + buf[cur, 1] * route[:, 3:4]
    o_ref[...] = _layer_norm(DEEPNORM_ALPHA * h_ref[...] + ffn, g_ref[...], b_ref[...])


def _combine(pos_flat, h, route, g2, b2, ys):
    N = h.shape[0]
    tm = TOK_TILE
    row = lambda i, pos: (i, 0)
    const = lambda i, pos: (0, 0)
    grid_spec = pltpu.PrefetchScalarGridSpec(
        num_scalar_prefetch=1,
        grid=(N // tm,),
        in_specs=[
            pl.BlockSpec((tm, D_MODEL), row), pl.BlockSpec((tm, LANES), row),
            pl.BlockSpec(g2.shape, const), pl.BlockSpec(b2.shape, const),
            pl.BlockSpec(memory_space=pl.ANY),
        ],
        out_specs=pl.BlockSpec((tm, D_MODEL), row),
        scratch_shapes=[pltpu.VMEM((2, 2, tm, D_MODEL), F32), pltpu.SemaphoreType.DMA((2,))],
    )
    return pl.pallas_call(
        _combine_kernel,
        grid_spec=grid_spec,
        out_shape=jax.ShapeDtypeStruct((N, D_MODEL), F32),
        compiler_params=_params(("arbitrary",)),
    )(pos_flat, h, route, g2, b2, ys)


def _rope_tables(T, dim):
    half = dim // 2
    inv = 1.0 / (ROPE_THETA ** (jnp.arange(half, dtype=F32) / half))
    ang = jnp.arange(T).astype(F32)[:, None] * inv[None, :]
    cos = jnp.cos(ang)
    sin = jnp.sin(ang)
    reps = LANES // dim
    cos_t = jnp.tile(jnp.concatenate([cos, cos], axis=1), (1, reps))
    sin_t = jnp.tile(jnp.concatenate([-sin, sin], axis=1), (1, reps))
    return cos_t, sin_t


def _pad_cols(w, width):
    return jnp.pad(w, ((0, 0), (0, width - w.shape[1])))


def _layer(x, w_in, w_gla_gate, b_gla_gate, g_gla_norm, w_out, ln1_g, ln1_b,
           w_gr, b_gr, w_er, b_er, w_e_in, w_e_out, ln2_g, ln2_b):
    B, T, D = x.shape
    N = B * T
    assert D == D_MODEL and T % DSA_TILE == 0 and N % DISPATCH_TILE == 0 and DISPATCH_TILE % RANK_TILE == 0
    x2 = x.reshape(N, D)

    sizes = (512, 512, 512, 256, 32, 8, 256, 256, 512, 512, 16)
    offs = np.concatenate([[0], np.cumsum(sizes)])
    col = lambda k: w_in[:, offs[k]:offs[k + 1]]
    wa = jnp.concatenate([col(0) * (DSA_HEAD_DIM ** -0.5 * LOG2E), col(1), col(2)], axis=1).astype(BF16)
    wi = _pad_cols(jnp.concatenate([col(3), col(4), col(5) * IDX_SCALE], axis=1), 3 * LANES).astype(BF16)
    wb = jnp.concatenate([col(6) * (GLA_DK ** -0.5), col(7), col(8), col(9)], axis=1).astype(BF16)
    wg = _pad_cols(col(10), LANES).astype(BF16)
    wgate = jnp.pad(w_gla_gate, ((0, LANES - GLA_GATE_RANK), (0, 0))).astype(BF16)
    bgate = b_gla_gate.reshape(1, -1)
    cosa, sina = _rope_tables(T, DSA_HEAD_DIM)
    cosi, sini = _rope_tables(T, IDX_DIM)

    q, k, v, iq, ikw, bq, bk, bv, br, la = _in_proj(x2, wa, wi, wb, wg, wgate, bgate, cosa, sina, cosi, sini, T)

    nq = T // DSA_TILE
    iqt = iq.reshape(B, nq, DSA_TILE, IDX_HEADS, IDX_DIM).transpose(0, 1, 4, 3, 2)
    iqt = iqt.reshape(B, nq, IDX_DIM, IDX_HEADS * DSA_TILE)
    ik = ikw[:, :IDX_DIM].astype(BF16).reshape(B, T, IDX_DIM)
    iwt = ikw[:, IDX_DIM:IDX_DIM + IDX_HEADS].reshape(B, T, IDX_HEADS).transpose(0, 2, 1)
    qt = q.reshape(B, T, DSA_WIDTH).transpose(0, 2, 1)
    vt = v.reshape(B, nq, DSA_TILE, DSA_WIDTH).transpose(0, 1, 3, 2)
    ya = _dsa(iqt, ik, iwt, qt, k.reshape(B, T, DSA_WIDTH), vt).transpose(0, 2, 1)

    r3 = lambda a: a.reshape(B, T, a.shape[-1])
    yb = _gla(r3(bq), r3(bk), r3(bv), r3(br), r3(la), g_gla_norm.reshape(1, GLA_DV))

    wr = _pad_cols(jnp.concatenate([w_gr, w_er], axis=1), LANES)
    wr_hi = wr.astype(BF16)
    wr_lo = (wr - wr_hi.astype(F32)).astype(BF16)
    brt = _pad_cols(jnp.concatenate([b_gr, b_er]).reshape(1, -1), LANES)
    h, route = _out_proj(x2, ya.reshape(N, DSA_WIDTH), yb.reshape(N, GLA_WIDTH), w_out.astype(BF16),
                         ln1_g.reshape(1, D), ln1_b.reshape(1, D), wr_hi, wr_lo, brt)

    rank, cnt = _rank(route)
    counts = cnt[0, :N_EXPERTS].astype(jnp.int32)
    padded = ((counts + MOE_TILE - 1) // MOE_TILE) * MOE_TILE
    ends = jnp.cumsum(padded)
    starts = ends - padded
    eid = route[:, 0:2].astype(jnp.int32)
    pos = (starts[eid] + rank[:, 0:2].astype(jnp.int32)).reshape(-1)
    n_rows = 2 * N + N_EXPERTS * MOE_TILE
    n_tiles = n_rows // MOE_TILE
    tile_start = jnp.arange(n_tiles, dtype=jnp.int32) * MOE_TILE
    tile_expert = jnp.minimum(jnp.sum(tile_start[:, None] >= ends[None, :], axis=1), N_EXPERTS - 1).astype(jnp.int32)
    n_active = (ends[-1] // MOE_TILE).astype(jnp.int32).reshape(1)

    xs = _dispatch(pos, h, jnp.zeros((n_rows, D), F32))
    ys = _ffn(tile_expert, n_active, xs, w_e_in, w_e_out)
    out = _combine(pos, h, route, ln2_g.reshape(1, D), ln2_b.reshape(1, D), ys)
    return out.reshape(B, T, D)


def kernel(x, w_in, w_gla_gate, b_gla_gate, g_gla_norm, w_out, ln1_g, ln1_b, w_group_router, b_group_router,
           w_expert_router, b_expert_router, w_expert_in, w_expert_out, ln2_g, ln2_b):
    h = x
    for l in range(w_in.shape[0]):
        h = _layer(h, w_in[l], w_gla_gate[l], b_gla_gate[l], g_gla_norm[l], w_out[l], ln1_g[l], ln1_b[l],
                   w_group_router[l], b_group_router[l], w_expert_router[l], b_expert_router[l],
                   w_expert_in[l], w_expert_out[l], ln2_g[l], ln2_b[l])
    return h
```

```python
import functools

import numpy as np
import jax
import jax.numpy as jnp
from jax import lax
from jax.experimental import pallas as pl
from jax.experimental.pallas import tpu as pltpu

F32 = jnp.float32
BF16 = jnp.bfloat16

D_MODEL = 1024
CHUNK = 64
ROPE_THETA = 10000.0
LN_EPS = 1e-5
DSA_WIDTH = 512
DSA_HEAD_DIM = 64
DSA_HEADS = 8
IDX_HEADS = 8
IDX_DIM = 32
IDX_SCALE = (IDX_HEADS * IDX_DIM) ** -0.5
DSA_TOPK_MAX = 256
GLA_WIDTH = 512
GLA_HEADS = 4
GLA_DV = 128
GLA_DK = 64
GLA_GATE_RANK = 16
GLA_TAU = 16.0
N_GROUPS = 4
EXPERTS_PER_GROUP = 8
N_EXPERTS = 32
D_EXPERT = 512
DEEPNORM_ALPHA = 2.0 ** 0.25

LANES = 128
SUBLANES = 8
TINY = 2.0 ** -126
NEG_BIG = -1e30
VMEM_LIMIT = 56 * 1024 * 1024

TOK_TILE = 256
DSA_TILE = 256
GLA_BLOCK = 256
MOE_TILE = 256
RANK_TILE = 512
DISPATCH_TILE = 512
DMA_UNROLL = 8
LOG2E = 1.4426950408889634
BISECT_ARITH = 40
BISECT_CAP = 80


def _dot(a, b):
    return jnp.dot(a, b, preferred_element_type=F32)


def _dot_nt(a, b):
    return lax.dot_general(a, b, (((1,), (1,)), ((), ())), preferred_element_type=F32)


def _dot_tn(a, b):
    return lax.dot_general(a, b, (((0,), (0,)), ((), ())), preferred_element_type=F32)


def _params(sem):
    return pltpu.CompilerParams(dimension_semantics=sem, vmem_limit_bytes=VMEM_LIMIT)


def _rope_slab(slab, cos, sin, first_half, half):
    swapped = jnp.where(first_half, pltpu.roll(slab, LANES - half, 1), pltpu.roll(slab, half, 1))
    return slab * cos + swapped * sin


def _in_proj_kernel(x_ref, wa_ref, wi_ref, wb_ref, wg_ref, wgate_ref, bgate_ref,
                    cosa_ref, sina_ref, cosi_ref, sini_ref,
                    q_ref, k_ref, v_ref, iq_ref, ikw_ref, bq_ref, bk_ref, bv_ref, br_ref, la_ref):
    tm = x_ref.shape[0]
    xb = x_ref[...].astype(BF16)
    lane = lax.broadcasted_iota(jnp.int32, (tm, LANES), 1)

    a = _dot(xb, wa_ref[...])
    cosa, sina = cosa_ref[...], sina_ref[...]
    first_a = (lane & (DSA_HEAD_DIM - 1)) < DSA_HEAD_DIM // 2
    for c in range(DSA_WIDTH // LANES):
        sl = slice(c * LANES, (c + 1) * LANES)
        q_ref[:, sl] = _rope_slab(a[:, sl], cosa, sina, first_a, DSA_HEAD_DIM // 2).astype(BF16)
        ks = slice(DSA_WIDTH + c * LANES, DSA_WIDTH + (c + 1) * LANES)
        k_ref[:, sl] = _rope_slab(a[:, ks], cosa, sina, first_a, DSA_HEAD_DIM // 2).astype(BF16)
    v_ref[...] = a[:, 2 * DSA_WIDTH:3 * DSA_WIDTH].astype(BF16)

    ii = _dot(xb, wi_ref[...])
    cosi, sini = cosi_ref[...], sini_ref[...]
    first_i = (lane & (IDX_DIM - 1)) < IDX_DIM // 2
    for c in range(2):
        sl = slice(c * LANES, (c + 1) * LANES)
        iq_ref[:, sl] = _rope_slab(ii[:, sl], cosi, sini, first_i, IDX_DIM // 2).astype(BF16)
    last = ii[:, 2 * LANES:3 * LANES]
    ikw_ref[...] = jnp.where(lane < IDX_DIM, _rope_slab(last, cosi, sini, first_i, IDX_DIM // 2), last)

    b = _dot(xb, wb_ref[...])
    bq_ref[...] = b[:, 0:256]
    bk_ref[...] = b[:, 256:512]
    bv_ref[...] = b[:, 512:1024]
    br_ref[...] = b[:, 1024:1536]

    g = _dot(xb, wg_ref[...])
    z = _dot(g.astype(BF16), wgate_ref[...]) + bgate_ref[...]
    log_sig = jnp.minimum(z, 0.0) - jnp.log(1.0 + jnp.exp(-jnp.abs(z)))
    la_ref[...] = log_sig * (1.0 / GLA_TAU)


def _in_proj(x2, wa, wi, wb, wg, wgate, bgate, cosa, sina, cosi, sini, T):
    N = x2.shape[0]
    tm = TOK_TILE
    nt = T // tm
    row = lambda i: (i, 0)
    const = lambda i: (0, 0)
    pos = lambda i: (i % nt, 0)
    outs = [
        (DSA_WIDTH, BF16), (DSA_WIDTH, BF16), (DSA_WIDTH, BF16), (IDX_HEADS * IDX_DIM, BF16), (LANES, F32),
        (256, F32), (256, F32), (512, F32), (512, F32), (256, F32),
    ]
    return pl.pallas_call(
        _in_proj_kernel,
        grid=(N // tm,),
        in_specs=[
            pl.BlockSpec((tm, D_MODEL), row),
            pl.BlockSpec(wa.shape, const), pl.BlockSpec(wi.shape, const), pl.BlockSpec(wb.shape, const),
            pl.BlockSpec(wg.shape, const), pl.BlockSpec(wgate.shape, const), pl.BlockSpec(bgate.shape, const),
            pl.BlockSpec((tm, LANES), pos), pl.BlockSpec((tm, LANES), pos),
            pl.BlockSpec((tm, LANES), pos), pl.BlockSpec((tm, LANES), pos),
        ],
        out_specs=[pl.BlockSpec((tm, w), row) for w, _ in outs],
        out_shape=[jax.ShapeDtypeStruct((N, w), dt) for w, dt in outs],
        compiler_params=_params(("parallel",)),
    )(x2, wa, wi, wb, wg, wgate, bgate, cosa, sina, cosi, sini)


def _dsa_kernel(iqt_ref, ik_ref, iwt_ref, qt_ref, k_ref, vt_ref, o_ref,
                s_scr, qm_scr, st_scr, m_scr, l_scr, acc_scr, sn_scr, *, topk, seq_len):
    tq = DSA_TILE
    tk = DSA_TILE
    grp = tk // SUBLANES
    i = pl.program_id(1)
    kf = jnp.float32(topk)

    iqt = iqt_ref[0, 0]
    iwt = iwt_ref[0]
    krow = lax.broadcasted_iota(jnp.int32, (tk, tq), 0)
    qcol = lax.broadcasted_iota(jnp.int32, (tk, tq), 1)
    qcol8 = lax.broadcasted_iota(jnp.int32, (SUBLANES, tq), 1)

    def tree(x3, op):
        q4 = grp // 4
        return op(jnp.stack([op(x3[a * q4:(a + 1) * q4], axis=0) for a in range(4)]), axis=0)

    def fold(x, op):
        return tree(x.reshape(grp, SUBLANES, tq), op)

    def spread(x):
        return jnp.broadcast_to(x, (SUBLANES, tq))

    def score_tile(j):
        keys = ik_ref[0, pl.ds(pl.multiple_of(j * tk, tk), tk), :]
        rt = _dot(keys, iqt)
        acc = None
        for h in range(IDX_HEADS):
            term = jnp.maximum(rt[:, h * tq:(h + 1) * tq], 0.0) * iwt[h:h + 1, :]
            acc = term if acc is None else acc + term
        return acc

    def score_body(j, carry):
        rmax, rmin = carry
        s = score_tile(j)
        s_scr[j] = s
        return jnp.maximum(rmax, fold(s, jnp.max)), jnp.minimum(rmin, fold(s, jnp.min))

    init = (jnp.full((SUBLANES, tq), -jnp.inf, F32), jnp.full((SUBLANES, tq), jnp.inf, F32))
    rmax, rmin = lax.fori_loop(0, i, score_body, init)
    s = score_tile(i)
    adm = (krow >> 6) <= (qcol >> 6)
    s_scr[i] = jnp.where(adm, s, -jnp.inf)
    rmax = jnp.maximum(rmax, fold(jnp.where(adm, s, -jnp.inf), jnp.max))
    rmin = jnp.minimum(rmin, fold(jnp.where(adm, s, jnp.inf), jnp.min))
    rmax = spread(jnp.max(rmax, axis=0, keepdims=True))
    rmin = spread(jnp.min(rmin, axis=0, keepdims=True))

    def count(preds):
        def body(j, accs):
            t3 = s_scr[j].reshape(grp, SUBLANES, tq)
            return tuple(a + tree(jnp.where(p(t3, j), 1.0, 0.0), jnp.sum) for a, p in zip(accs, preds))
        accs = lax.fori_loop(0, i + 1, body, tuple(jnp.zeros((SUBLANES, tq), F32) for _ in preds))
        return [spread(jnp.sum(a, axis=0, keepdims=True)) for a in accs]

    n_adm = (((qcol8 >> 6) + 1 + i * (tq // CHUNK)) * CHUNK).astype(F32)
    cpos, cnn = count([lambda t, j: t > 0.0, lambda t, j: t >= 0.0])
    search = n_adm > kf
    positive = cpos >= kf
    negative = cnn < kf
    lo = jnp.where(positive, TINY, jnp.where(negative, rmin, 0.0))
    clo = jnp.where(positive, cpos, jnp.where(negative, n_adm, cnn))
    st_scr[0] = jnp.where(search, lo, rmin)
    st_scr[1] = jnp.where(negative, -TINY, rmax + (jnp.abs(rmax) * (2.0 ** -10) + TINY))
    st_scr[2] = jnp.where(search, clo, n_adm)
    zero_thr = jnp.logical_and(jnp.logical_not(positive), jnp.logical_not(negative))
    st_scr[3] = jnp.where(jnp.logical_or(jnp.logical_not(search), jnp.logical_or(zero_thr, clo == kf)), 1.0, 0.0)

    def bis_cond(carry):
        it, pending = carry
        return jnp.logical_and(pending > 0.0, it < BISECT_CAP)

    def bis_body(carry):
        it, _ = carry
        lo, hi, clo, done = st_scr[0], st_scr[1], st_scr[2], st_scr[3]
        a = lax.bitcast_convert_type(jnp.abs(lo), jnp.int32)
        b = lax.bitcast_convert_type(jnp.abs(hi), jnp.int32)
        geo = lax.bitcast_convert_type(a + ((b - a) >> 1), F32)
        geo = jnp.where(hi > 0.0, geo, -geo)
        mid = jnp.where(it < BISECT_ARITH, lo + (hi - lo) * 0.5, geo)
        stuck = jnp.logical_or(mid <= lo, mid >= hi)
        cnt, = count([lambda t, j: t >= mid[None]])
        live = jnp.logical_and(done == 0.0, jnp.logical_not(stuck))
        up = jnp.logical_and(live, cnt >= kf)
        dn = jnp.logical_and(live, cnt < kf)
        lo = jnp.where(up, mid, lo)
        clo = jnp.where(up, cnt, clo)
        hi = jnp.where(dn, mid, hi)
        done = jnp.where(jnp.logical_or(stuck, clo == kf), 1.0, done)
        st_scr[0], st_scr[1], st_scr[2], st_scr[3] = lo, hi, clo, done
        return it + 1, jnp.max(1.0 - done)

    lax.while_loop(bis_cond, bis_body, (jnp.int32(0), jnp.max(1.0 - st_scr[3])))
    thr = st_scr[0]
    clo = st_scr[2]

    @pl.when(jnp.max(clo) > kf)
    def _():
        tie = clo > kf
        cgt, = count([lambda t, j: t > thr[None]])
        need = kf - cgt
        n_it = int(np.ceil(np.log2(seq_len))) + 1
        kidx = (lax.broadcasted_iota(jnp.int32, (grp, SUBLANES, tq), 0) * SUBLANES
                + lax.broadcasted_iota(jnp.int32, (grp, SUBLANES, tq), 1)).astype(F32)

        def jb(_, carry):
            jlo, jhi = carry
            jm = jnp.floor((jlo + jhi) * 0.5)
            f, = count([lambda t, j: jnp.logical_and(t == thr[None], kidx + (j * tk).astype(F32) <= jm[None])])
            ok = f >= need
            return jnp.where(ok, jlo, jm), jnp.where(ok, jm, jhi)

        _, jcut = lax.fori_loop(0, n_it, jb, (jnp.full((SUBLANES, tq), -1.0, F32),
                                              jnp.full((SUBLANES, tq), float(seq_len - 1), F32)))

        def drop(j, _):
            t3 = s_scr[j].reshape(grp, SUBLANES, tq)
            late = kidx + (j * tk).astype(F32) > jcut[None]
            kill = jnp.logical_and(tie[None], jnp.logical_and(t3 == thr[None], late))
            s_scr[j] = jnp.where(kill, -jnp.inf, t3).reshape(tk, tq)
            return 0

        lax.fori_loop(0, i + 1, drop, 0)

    hrow = lax.broadcasted_iota(jnp.int32, (LANES, tq), 0)
    for h in range(DSA_HEADS):
        slab = qt_ref[0, (h // 2) * LANES:(h // 2 + 1) * LANES, :]
        mine = (hrow >= DSA_HEAD_DIM) if (h % 2) else (hrow < DSA_HEAD_DIM)
        qm_scr[h] = jnp.where(mine, slab, jnp.zeros_like(slab))
        m_scr[h] = jnp.full((SUBLANES, tq), NEG_BIG, F32)
        l_scr[h] = jnp.zeros((SUBLANES, tq), F32)
        acc_scr[h] = jnp.zeros((LANES, tq), F32)

    def slab(h):
        return slice((h // 2) * LANES, (h // 2 + 1) * LANES)

    def tile_operands(j):
        t3 = s_scr[j].reshape(grp, SUBLANES, tq)
        bias = jnp.where(t3 >= thr[None], 0.0, NEG_BIG).reshape(tk, tq)
        return k_ref[0, pl.ds(pl.multiple_of(j * tk, tk), tk), :], bias

    def masked_scores(keys, bias, h):
        return _dot(keys[:, slab(h)], qm_scr[h]) + bias

    keys0, bias0 = tile_operands(0)
    for h in range(DSA_HEADS):
        sn_scr[h] = masked_scores(keys0, bias0, h)

    def att_body(j, _):
        keys_n, bias_n = tile_operands(jnp.minimum(j + 1, i))
        for h in range(DSA_HEADS):
            s3 = sn_scr[h].reshape(grp, SUBLANES, tq)
            m_prev = m_scr[h]
            m_new = jnp.maximum(m_prev, spread(jnp.max(tree(s3, jnp.max), axis=0, keepdims=True)))
            p3 = jnp.exp2(s3 - m_new[None])
            corr = jnp.exp2(m_prev - m_new)
            l_scr[h] = corr * l_scr[h] + tree(p3, jnp.sum)
            pv = _dot(vt_ref[0, j, slab(h), :], p3.reshape(tk, tq).astype(BF16))
            acc = acc_scr[h].reshape(LANES // SUBLANES, SUBLANES, tq) * corr[None]
            acc_scr[h] = acc.reshape(LANES, tq) + pv
            m_scr[h] = m_new
            sn_scr[h] = masked_scores(keys_n, bias_n, h)
        return 0

    lax.fori_loop(0, i + 1, att_body, 0)

    def normalised(h):
        inv = 1.0 / spread(jnp.sum(l_scr[h], axis=0, keepdims=True))
        return (acc_scr[h].reshape(LANES // SUBLANES, SUBLANES, tq) * inv[None]).reshape(LANES, tq)

    for sp in range(DSA_HEADS // 2):
        pair = jnp.where(hrow < DSA_HEAD_DIM, normalised(2 * sp), normalised(2 * sp + 1))
        o_ref[0, sp * LANES:(sp + 1) * LANES, :] = pair.astype(o_ref.dtype)


def _dsa(iqt, ik, iwt, qt, k, vt):
    B, T, _ = k.shape
    tq = DSA_TILE
    nq = T // tq
    topk = min(DSA_TOPK_MAX, T // 4)
    kern = functools.partial(_dsa_kernel, topk=topk, seq_len=T)
    return pl.pallas_call(
        kern,
        grid=(B, nq),
        in_specs=[
            pl.BlockSpec((1, 1, IDX_DIM, IDX_HEADS * tq), lambda b, i: (b, i, 0, 0)),
            pl.BlockSpec((1, T, IDX_DIM), lambda b, i: (b, 0, 0)),
            pl.BlockSpec((1, IDX_HEADS, tq), lambda b, i: (b, 0, i)),
            pl.BlockSpec((1, DSA_WIDTH, tq), lambda b, i: (b, 0, i)),
            pl.BlockSpec((1, T, DSA_WIDTH), lambda b, i: (b, 0, 0)),
            pl.BlockSpec((1, nq, DSA_WIDTH, tq), lambda b, i: (b, 0, 0, 0)),
        ],
        out_specs=pl.BlockSpec((1, DSA_WIDTH, tq), lambda b, i: (b, 0, i)),
        out_shape=jax.ShapeDtypeStruct((B, DSA_WIDTH, T), BF16),
        scratch_shapes=[
            pltpu.VMEM((nq, tq, tq), F32),
            pltpu.VMEM((DSA_HEADS, LANES, tq), BF16),
            pltpu.VMEM((4, SUBLANES, tq), F32),
            pltpu.VMEM((DSA_HEADS, SUBLANES, tq), F32),
            pltpu.VMEM((DSA_HEADS, SUBLANES, tq), F32),
            pltpu.VMEM((DSA_HEADS, LANES, tq), F32),
            pltpu.VMEM((DSA_HEADS, tq, tq), F32),
        ],
        compiler_params=_params(("parallel", "arbitrary")),
    )(iqt, ik, iwt, qt, k, vt)


def _gla_kernel(q_ref, k_ref, v_ref, r_ref, la_ref, g_ref, o_ref, s_scr):
    rows = GLA_BLOCK
    nch = rows // CHUNK
    npair = GLA_HEADS // 2

    @pl.when(pl.program_id(1) == 0)
    def _():
        s_scr[...] = jnp.zeros_like(s_scr)

    ri = lax.broadcasted_iota(jnp.int32, (rows, rows), 0)
    ci = lax.broadcasted_iota(jnp.int32, (rows, rows), 1)
    causal = jnp.logical_and((ri >> 6) == (ci >> 6), ci <= ri)
    tri = jnp.where(causal, 1.0, 0.0).astype(BF16)

    la = la_ref[0]
    la_hi = la.astype(BF16)
    rem = la - la_hi.astype(F32)
    la_mid = rem.astype(BF16)
    la_lo = (rem - la_mid.astype(F32)).astype(BF16)
    b = _dot(tri, la_hi) + _dot(tri, la_mid) + _dot(tri, la_lo)
    b_last = jnp.concatenate(
        [jnp.broadcast_to(b[(c + 1) * CHUNK - 1:(c + 1) * CHUNK, :], (CHUNK, b.shape[1])) for c in range(nch)], axis=0)

    q = q_ref[0]
    k = k_ref[0]
    qg = q * jnp.exp(b)
    kg = (k * jnp.exp(-b)).astype(BF16)
    kd = (k * jnp.exp(b_last - b)).astype(BF16)
    decay_rows = jnp.exp(b_last)

    lane = lax.broadcasted_iota(jnp.int32, (rows, LANES), 1)
    eye = lax.broadcasted_iota(jnp.int32, (LANES, LANES), 0) == lax.broadcasted_iota(jnp.int32, (LANES, LANES), 1)
    top_rows = lax.broadcasted_iota(jnp.int32, (LANES, GLA_DV), 0) < GLA_DK

    for pr in range(npair):
        sl = slice(pr * LANES, (pr + 1) * LANES)
        qg_p = qg[:, sl]
        kg_p = kg[:, sl]
        kd_p = kd[:, sl]
        heads = (2 * pr, 2 * pr + 1)
        qg_h = [jnp.where(lane < GLA_DK, qg_p, 0.0).astype(BF16), jnp.where(lane >= GLA_DK, qg_p, 0.0).astype(BF16)]
        v_h = [v_ref[0, :, h * GLA_DV:(h + 1) * GLA_DV].astype(BF16) for h in heads]
        o_h = []
        for t in range(2):
            a = jnp.where(causal, _dot_nt(qg_h[t], kg_p), 0.0)
            o_h.append(_dot(a.astype(BF16), v_h[t]))
        state = s_scr[pr]
        inter = [[], []]
        for c in range(nch):
            rs = slice(c * CHUNK, (c + 1) * CHUNK)
            sb = state.astype(BF16)
            for t in range(2):
                inter[t].append(_dot(qg_h[t][rs], sb))
            kv = jnp.where(top_rows, _dot_tn(kd_p[rs], v_h[0][rs]), _dot_tn(kd_p[rs], v_h[1][rs]))
            drow = jnp.broadcast_to(decay_rows[c * CHUNK:c * CHUNK + 1, sl], (LANES, LANES))
            dcol = jnp.sum(jnp.where(eye, drow, 0.0), axis=1, keepdims=True)
            state = dcol * state + kv
        s_scr[pr] = state
        for t in range(2):
            h = heads[t]
            o = o_h[t] + jnp.concatenate(inter[t], axis=0)
            ms = jnp.mean(o * o, axis=1, keepdims=True)
            on = o * lax.rsqrt(ms + LN_EPS) * g_ref[...]
            r = r_ref[0, :, h * GLA_DV:(h + 1) * GLA_DV]
            o_ref[0, :, h * GLA_DV:(h + 1) * GLA_DV] = (on * (r / (1.0 + jnp.exp(-r)))).astype(o_ref.dtype)


def _gla(bq, bk, bv, br, la, g):
    B, T, _ = bq.shape
    rows = GLA_BLOCK
    blk = lambda w: pl.BlockSpec((1, rows, w), lambda b, i: (b, i, 0))
    return pl.pallas_call(
        _gla_kernel,
        grid=(B, T // rows),
        in_specs=[blk(256), blk(256), blk(512), blk(512), blk(256), pl.BlockSpec((1, GLA_DV), lambda b, i: (0, 0))],
        out_specs=blk(GLA_WIDTH),
        out_shape=jax.ShapeDtypeStruct((B, T, GLA_WIDTH), BF16),
        scratch_shapes=[pltpu.VMEM((GLA_HEADS // 2, 2 * GLA_DK, GLA_DV), F32)],
        compiler_params=_params(("parallel", "arbitrary")),
    )(bq, bk, bv, br, la, g)


def _layer_norm(x, g, b):
    mu = jnp.mean(x, axis=1, keepdims=True)
    xc = x - mu
    var = jnp.mean(xc * xc, axis=1, keepdims=True)
    return xc * lax.rsqrt(var + LN_EPS) * g + b


def _out_proj_kernel(x_ref, ya_ref, yb_ref, wo_ref, g_ref, b_ref, wr_hi_ref, wr_lo_ref, br_ref, h_ref, route_ref):
    tm = x_ref.shape[0]
    mix = _dot(ya_ref[...], wo_ref[0:DSA_WIDTH, :]) + _dot(yb_ref[...], wo_ref[DSA_WIDTH:D_MODEL, :])
    h = _layer_norm(DEEPNORM_ALPHA * x_ref[...] + mix, g_ref[...], b_ref[...])
    h_ref[...] = h

    h_hi = h.astype(BF16)
    h_lo = (h - h_hi.astype(F32)).astype(BF16)
    logits = _dot(h_hi, wr_hi_ref[...]) + _dot(h_lo, wr_hi_ref[...]) + _dot(h_hi, wr_lo_ref[...]) + br_ref[...]

    lane = lax.broadcasted_iota(jnp.int32, (tm, LANES), 1)
    lanef = lane.astype(F32)
    gl = jnp.where(lane < N_GROUPS, logits, -jnp.inf)
    gmax = jnp.max(gl, axis=1, keepdims=True)
    gsel = jnp.min(jnp.where(gl == gmax, lanef, 1e9), axis=1, keepdims=True)
    pg = 1.0 / jnp.sum(jnp.exp(gl - gmax), axis=1, keepdims=True)
    egrp = ((lane - N_GROUPS) >> 3).astype(F32)
    in_grp = jnp.logical_and(jnp.logical_and(lane >= N_GROUPS, lane < N_GROUPS + N_EXPERTS), egrp == gsel)
    el = jnp.where(in_grp, logits, -jnp.inf)
    t1 = jnp.max(el, axis=1, keepdims=True)
    i1 = jnp.min(jnp.where(el == t1, lanef, 1e9), axis=1, keepdims=True)
    el2 = jnp.where(lanef == i1, -jnp.inf, el)
    t2 = jnp.max(el2, axis=1, keepdims=True)
    i2 = jnp.min(jnp.where(el2 == t2, lanef, 1e9), axis=1, keepdims=True)
    e21 = jnp.exp(t2 - t1)
    g1 = pg / (1.0 + e21)
    g2 = pg * e21 / (1.0 + e21)
    route = jnp.where(lane == 0, i1 - N_GROUPS, 0.0)
    route = jnp.where(lane == 1, i2 - N_GROUPS, route)
    route = jnp.where(lane == 2, g1, route)
    route = jnp.where(lane == 3, g2, route)
    route_ref[...] = route


def _out_proj(x2, ya, yb, wo, g1, b1, wr_hi, wr_lo, br):
    N = x2.shape[0]
    tm = TOK_TILE
    row = lambda i: (i, 0)
    const = lambda i: (0, 0)
    return pl.pallas_call(
        _out_proj_kernel,
        grid=(N // tm,),
        in_specs=[
            pl.BlockSpec((tm, D_MODEL), row), pl.BlockSpec((tm, DSA_WIDTH), row), pl.BlockSpec((tm, GLA_WIDTH), row),
            pl.BlockSpec(wo.shape, const), pl.BlockSpec(g1.shape, const), pl.BlockSpec(b1.shape, const),
            pl.BlockSpec(wr_hi.shape, const), pl.BlockSpec(wr_lo.shape, const), pl.BlockSpec(br.shape, const),
        ],
        out_specs=[pl.BlockSpec((tm, D_MODEL), row), pl.BlockSpec((tm, LANES), row)],
        out_shape=[jax.ShapeDtypeStruct((N, D_MODEL), F32), jax.ShapeDtypeStruct((N, LANES), F32)],
        compiler_params=_params(("parallel",)),
    )(x2, ya, yb, wo, g1, b1, wr_hi, wr_lo, br)


def _rank_kernel(route_ref, rank_ref, cnt_ref, carry_scr):
    tm = route_ref.shape[0]

    @pl.when(pl.program_id(0) == 0)
    def _():
        carry_scr[...] = jnp.zeros_like(carry_scr)

    route = route_ref[...]
    lanef = lax.broadcasted_iota(jnp.int32, (tm, LANES), 1).astype(F32)
    e1 = route[:, 0:1]
    e2 = route[:, 1:2]
    hit1 = lanef == e1
    hit2 = lanef == e2
    onehot = jnp.where(jnp.logical_or(hit1, hit2), 1.0, 0.0).astype(BF16)
    ri = lax.broadcasted_iota(jnp.int32, (tm, tm), 0)
    ci = lax.broadcasted_iota(jnp.int32, (tm, tm), 1)
    before = jnp.where(ci < ri, 1.0, 0.0).astype(BF16)
    prefix = _dot(before, onehot) + carry_scr[0:1, :]
    r1 = jnp.sum(jnp.where(hit1, prefix, 0.0), axis=1, keepdims=True)
    r2 = jnp.sum(jnp.where(hit2, prefix, 0.0), axis=1, keepdims=True)
    rank_ref[...] = jnp.where(lanef == 0.0, r1, jnp.where(lanef == 1.0, r2, 0.0))
    total = _dot(jnp.ones((8, tm), BF16), onehot)
    carry_scr[...] = carry_scr[...] + total
    cnt_ref[...] = carry_scr[...]


def _rank(route):
    N = route.shape[0]
    tm = RANK_TILE
    return pl.pallas_call(
        _rank_kernel,
        grid=(N // tm,),
        in_specs=[pl.BlockSpec((tm, LANES), lambda i: (i, 0))],
        out_specs=[pl.BlockSpec((tm, LANES), lambda i: (i, 0)), pl.BlockSpec((8, LANES), lambda i: (0, 0))],
        out_shape=[jax.ShapeDtypeStruct((N, LANES), F32), jax.ShapeDtypeStruct((8, LANES), F32)],
        scratch_shapes=[pltpu.VMEM((8, LANES), F32)],
        compiler_params=_params(("arbitrary",)),
    )(route)


def _dispatch_kernel(pos_ref, h_ref, xs_in_ref, xs_ref, hbuf, load_sem, row_sem):
    del xs_in_ref
    tm = DISPATCH_TILE
    nbuf = hbuf.shape[0]
    i = pl.program_id(0)
    last = pl.num_programs(0) - 1

    def tile_load(step):
        return pltpu.make_async_copy(h_ref.at[pl.ds(step * tm, tm), :], hbuf.at[step % nbuf], load_sem.at[step % nbuf])

    def row_copy(step, r, slot):
        p = pos_ref[2 * (step * tm + r) + slot]
        return pltpu.make_async_copy(hbuf.at[step % nbuf, pl.ds(r, 1), :], xs_ref.at[pl.ds(p, 1), :],
                                     row_sem.at[step % nbuf])

    def for_rows(fn):
        def body(r, _):
            fn(r, 0)
            fn(r, 1)
            return 0
        lax.fori_loop(0, tm, body, 0, unroll=DMA_UNROLL)

    @pl.when(i == 0)
    def _():
        tile_load(0).start()

    @pl.when(i < last)
    def _():
        tile_load(i + 1).start()

    tile_load(i).wait()
    for_rows(lambda r, slot: row_copy(i, r, slot).start())

    @pl.when(i > 0)
    def _():
        for_rows(lambda r, slot: row_copy(i - 1, r, slot).wait())

    @pl.when(i == last)
    def _():
        for_rows(lambda r, slot: row_copy(i, r, slot).wait())


def _dispatch(pos_flat, h, xs_init):
    N = h.shape[0]
    grid_spec = pltpu.PrefetchScalarGridSpec(
        num_scalar_prefetch=1,
        grid=(N // DISPATCH_TILE,),
        in_specs=[pl.BlockSpec(memory_space=pl.ANY), pl.BlockSpec(memory_space=pl.ANY)],
        out_specs=pl.BlockSpec(memory_space=pl.ANY),
        scratch_shapes=[pltpu.VMEM((3, DISPATCH_TILE, D_MODEL), F32),
                        pltpu.SemaphoreType.DMA((3,)), pltpu.SemaphoreType.DMA((3,))],
    )
    return pl.pallas_call(
        _dispatch_kernel,
        grid_spec=grid_spec,
        out_shape=jax.ShapeDtypeStruct(xs_init.shape, xs_init.dtype),
        input_output_aliases={2: 0},
        compiler_params=_params(("arbitrary",)),
    )(pos_flat, h, xs_init)


def _ffn_kernel(te_ref, na_ref, x_ref, wi_ref, wo_ref, y_ref, wi_bf, wo_bf):
    i = pl.program_id(0)
    active = i < na_ref[0]
    fresh = jnp.logical_or(i == 0, te_ref[i] != te_ref[jnp.maximum(i - 1, 0)])

    @pl.when(jnp.logical_and(active, fresh))
    def _():
        wi_bf[...] = wi_ref[0].astype(BF16)
        wo_bf[...] = wo_ref[0].astype(BF16)

    @pl.when(active)
    def _():
        hid = _dot(x_ref[...].astype(BF16), wi_bf[...])
        hg = hid[:, :D_EXPERT]
        hu = hid[:, D_EXPERT:]
        act = (hg / (1.0 + jnp.exp(-hg))) * hu
        y_ref[...] = _dot(act.astype(BF16), wo_bf[...])

    @pl.when(jnp.logical_not(active))
    def _():
        y_ref[...] = jnp.zeros_like(y_ref)


def _ffn(tile_expert, n_active, xs, w_e_in, w_e_out):
    R = xs.shape[0]
    tm = MOE_TILE

    def live(i, te, na):
        return jnp.minimum(i, na[0] - 1)

    grid_spec = pltpu.PrefetchScalarGridSpec(
        num_scalar_prefetch=2,
        grid=(R // tm,),
        in_specs=[
            pl.BlockSpec((tm, D_MODEL), lambda i, te, na: (live(i, te, na), 0)),
            pl.BlockSpec((1, D_MODEL, 2 * D_EXPERT), lambda i, te, na: (te[live(i, te, na)], 0, 0)),
            pl.BlockSpec((1, D_EXPERT, D_MODEL), lambda i, te, na: (te[live(i, te, na)], 0, 0)),
        ],
        out_specs=pl.BlockSpec((tm, D_MODEL), lambda i, te, na: (i, 0)),
        scratch_shapes=[pltpu.VMEM((D_MODEL, 2 * D_EXPERT), BF16), pltpu.VMEM((D_EXPERT, D_MODEL), BF16)],
    )
    return pl.pallas_call(
        _ffn_kernel,
        grid_spec=grid_spec,
        out_shape=jax.ShapeDtypeStruct((R, D_MODEL), F32),
        compiler_params=_params(("arbitrary",)),
    )(tile_expert, n_active, xs, w_e_in, w_e_out)


def _combine_kernel(pos_ref, h_ref, route_ref, g_ref, b_ref, ys_ref, o_ref, buf, sem):
    tm = h_ref.shape[0]
    i = pl.program_id(0)
    cur = i % 2

    def row_copy(step, r, slot):
        p = pos_ref[2 * (step * tm + r) + slot]
        half = step % 2
        return pltpu.make_async_copy(ys_ref.at[pl.ds(p, 1), :], buf.at[half, slot, pl.ds(r, 1), :], sem.at[half])

    def for_rows(fn):
        def body(r, _):
            fn(r, 0)
            fn(r, 1)
            return 0
        lax.fori_loop(0, tm, body, 0, unroll=DMA_UNROLL)

    @pl.when(i == 0)
    def _():
        for_rows(lambda r, slot: row_copy(0, r, slot).start())

    @pl.when(i + 1 < pl.num_programs(0))
    def _():
        for_rows(lambda r, slot: row_copy(i + 1, r, slot).start())

    for_rows(lambda r, slot: row_copy(i, r, slot).wait())
    route = route_ref[...]
    ffn = buf[cur, 0] * route[:, 2:3] + buf[cur, 1] * route[:, 3:4]
    o_ref[...] = _layer_norm(DEEPNORM_ALPHA * h_ref[...] + ffn, g_ref[...], b_ref[...])


def _combine(pos_flat, h, route, g2, b2, ys):
    N = h.shape[0]
    tm = TOK_TILE
    row = lambda i, pos: (i, 0)
    const = lambda i, pos: (0, 0)
    grid_spec = pltpu.PrefetchScalarGridSpec(
        num_scalar_prefetch=1,
        grid=(N // tm,),
        in_specs=[
            pl.BlockSpec((tm, D_MODEL), row), pl.BlockSpec((tm, LANES), row),
            pl.BlockSpec(g2.shape, const), pl.BlockSpec(b2.shape, const),
            pl.BlockSpec(memory_space=pl.ANY),
        ],
        out_specs=pl.BlockSpec((tm, D_MODEL), row),
        scratch_shapes=[pltpu.VMEM((2, 2, tm, D_MODEL), F32), pltpu.SemaphoreType.DMA((2,))],
    )
    return pl.pallas_call(
        _combine_kernel,
        grid_spec=grid_spec,
        out_shape=jax.ShapeDtypeStruct((N, D_MODEL), F32),
        compiler_params=_params(("arbitrary",)),
    )(pos_flat, h, route, g2, b2, ys)


def _rope_tables(T, dim):
    half = dim // 2
    inv = 1.0 / (ROPE_THETA ** (jnp.arange(half, dtype=F32) / half))
    ang = jnp.arange(T).astype(F32)[:, None] * inv[None, :]
    cos = jnp.cos(ang)
    sin = jnp.sin(ang)
    reps = LANES // dim
    cos_t = jnp.tile(jnp.concatenate([cos, cos], axis=1), (1, reps))
    sin_t = jnp.tile(jnp.concatenate([-sin, sin], axis=1), (1, reps))
    return cos_t, sin_t


def _pad_cols(w, width):
    return jnp.pad(w, ((0, 0), (0, width - w.shape[1])))


def _layer(x, w_in, w_gla_gate, b_gla_gate, g_gla_norm, w_out, ln1_g, ln1_b,
           w_gr, b_gr, w_er, b_er, w_e_in, w_e_out, ln2_g, ln2_b):
    B, T, D = x.shape
    N = B * T
    assert D == D_MODEL and T % DSA_TILE == 0 and N % DISPATCH_TILE == 0 and DISPATCH_TILE % RANK_TILE == 0
    x2 = x.reshape(N, D)

    sizes = (512, 512, 512, 256, 32, 8, 256, 256, 512, 512, 16)
    offs = np.concatenate([[0], np.cumsum(sizes)])
    col = lambda k: w_in[:, offs[k]:offs[k + 1]]
    wa = jnp.concatenate([col(0) * (DSA_HEAD_DIM ** -0.5 * LOG2E), col(1), col(2)], axis=1).astype(BF16)
    wi = _pad_cols(jnp.concatenate([col(3), col(4), col(5) * IDX_SCALE], axis=1), 3 * LANES).astype(BF16)
    wb = jnp.concatenate([col(6) * (GLA_DK ** -0.5), col(7), col(8), col(9)], axis=1).astype(BF16)
    wg = _pad_cols(col(10), LANES).astype(BF16)
    wgate = jnp.pad(w_gla_gate, ((0, LANES - GLA_GATE_RANK), (0, 0))).astype(BF16)
    bgate = b_gla_gate.reshape(1, -1)
    cosa, sina = _rope_tables(T, DSA_HEAD_DIM)
    cosi, sini = _rope_tables(T, IDX_DIM)

    q, k, v, iq, ikw, bq, bk, bv, br, la = _in_proj(x2, wa, wi, wb, wg, wgate, bgate, cosa, sina, cosi, sini, T)

    nq = T // DSA_TILE
    iqt = iq.reshape(B, nq, DSA_TILE, IDX_HEADS, IDX_DIM).transpose(0, 1, 4, 3, 2)
    iqt = iqt.reshape(B, nq, IDX_DIM, IDX_HEADS * DSA_TILE)
    ik = ikw[:, :IDX_DIM].astype(BF16).reshape(B, T, IDX_DIM)
    iwt = ikw[:, IDX_DIM:IDX_DIM + IDX_HEADS].reshape(B, T, IDX_HEADS).transpose(0, 2, 1)
    qt = q.reshape(B, T, DSA_WIDTH).transpose(0, 2, 1)
    vt = v.reshape(B, nq, DSA_TILE, DSA_WIDTH).transpose(0, 1, 3, 2)
    ya = _dsa(iqt, ik, iwt, qt, k.reshape(B, T, DSA_WIDTH), vt).transpose(0, 2, 1)

    r3 = lambda a: a.reshape(B, T, a.shape[-1])
    yb = _gla(r3(bq), r3(bk), r3(bv), r3(br), r3(la), g_gla_norm.reshape(1, GLA_DV))

    wr = _pad_cols(jnp.concatenate([w_gr, w_er], axis=1), LANES)
    wr_hi = wr.astype(BF16)
    wr_lo = (wr - wr_hi.astype(F32)).astype(BF16)
    brt = _pad_cols(jnp.concatenate([b_gr, b_er]).reshape(1, -1), LANES)
    h, route = _out_proj(x2, ya.reshape(N, DSA_WIDTH), yb.reshape(N, GLA_WIDTH), w_out.astype(BF16),
                         ln1_g.reshape(1, D), ln1_b.reshape(1, D), wr_hi, wr_lo, brt)

    rank, cnt = _rank(route)
    counts = cnt[0, :N_EXPERTS].astype(jnp.int32)
    padded = ((counts + MOE_TILE - 1) // MOE_TILE) * MOE_TILE
    ends = jnp.cumsum(padded)
    starts = ends - padded
    eid = route[:, 0:2].astype(jnp.int32)
    pos = (starts[eid] + rank[:, 0:2].astype(jnp.int32)).reshape(-1)
    n_rows = 2 * N + N_EXPERTS * MOE_TILE
    n_tiles = n_rows // MOE_TILE
    tile_start = jnp.arange(n_tiles, dtype=jnp.int32) * MOE_TILE
    tile_expert = jnp.minimum(jnp.sum(tile_start[:, None] >= ends[None, :], axis=1), N_EXPERTS - 1).astype(jnp.int32)
    n_active = (ends[-1] // MOE_TILE).astype(jnp.int32).reshape(1)

    xs = _dispatch(pos, h, jnp.zeros((n_rows, D), F32))
    ys = _ffn(tile_expert, n_active, xs, w_e_in, w_e_out)
    out = _combine(pos, h, route, ln2_g.reshape(1, D), ln2_b.reshape(1, D), ys)
    return out.reshape(B, T, D)


def kernel(x, w_in, w_gla_gate, b_gla_gate, g_gla_norm, w_out, ln1_g, ln1_b, w_group_router, b_group_router,
           w_expert_router, b_expert_router, w_expert_in, w_expert_out, ln2_g, ln2_b):
    h = x
    for l in range(w_in.shape[0]):
        h = _layer(h, w_in[l], w_gla_gate[l], b_gla_gate[l], g_gla_norm[l], w_out[l], ln1_g[l], ln1_b[l],
                   w_group_router[l], b_group_router[l], w_expert_router[l], b_expert_router[l],
                   w_expert_in[l], w_expert_out[l], ln2_g[l], ln2_b[l])
    return h
```

```python
import functools

import numpy as np
import jax
import jax.numpy as jnp
from jax import lax
from jax.experimental import pallas as pl
from jax.experimental.pallas import tpu as pltpu

F32 = jnp.float32
BF16 = jnp.bfloat16

D_MODEL = 1024
CHUNK = 64
ROPE_THETA = 10000.0
LN_EPS = 1e-5
DSA_WIDTH = 512
DSA_HEAD_DIM = 64
DSA_HEADS = 8
IDX_HEADS = 8
IDX_DIM = 32
IDX_SCALE = (IDX_HEADS * IDX_DIM) ** -0.5
DSA_TOPK_MAX = 256
GLA_WIDTH = 512
GLA_HEADS = 4
GLA_DV = 128
GLA_DK = 64
GLA_GATE_RANK = 16
GLA_TAU = 16.0
N_GROUPS = 4
EXPERTS_PER_GROUP = 8
N_EXPERTS = 32
D_EXPERT = 512
DEEPNORM_ALPHA = 2.0 ** 0.25

LANES = 128
SUBLANES = 8
TINY = 2.0 ** -126
NEG_BIG = -1e30
VMEM_LIMIT = 56 * 1024 * 1024

TOK_TILE = 256
DSA_TILE = 256
GLA_BLOCK = 256
MOE_TILE = 256
RANK_TILE = 512
DISPATCH_TILE = 512
DMA_UNROLL = 8
LOG2E = 1.4426950408889634
BISECT_ARITH = 40
BISECT_CAP = 80


def _dot(a, b):
    return jnp.dot(a, b, preferred_element_type=F32)


def _dot_nt(a, b):
    return lax.dot_general(a, b, (((1,), (1,)), ((), ())), preferred_element_type=F32)


def _dot_tn(a, b):
    return lax.dot_general(a, b, (((0,), (0,)), ((), ())), preferred_element_type=F32)


def _params(sem):
    return pltpu.CompilerParams(dimension_semantics=sem, vmem_limit_bytes=VMEM_LIMIT)


def _rope_slab(slab, cos, sin, first_half, half):
    swapped = jnp.where(first_half, pltpu.roll(slab, LANES - half, 1), pltpu.roll(slab, half, 1))
    return slab * cos + swapped * sin


def _in_proj_kernel(x_ref, wa_ref, wi_ref, wb_ref, wg_ref, wgate_ref, bgate_ref,
                    cosa_ref, sina_ref, cosi_ref, sini_ref,
                    q_ref, k_ref, v_ref, iq_ref, ikw_ref, bq_ref, bk_ref, bv_ref, br_ref, la_ref):
    tm = x_ref.shape[0]
    xb = x_ref[...].astype(BF16)
    lane = lax.broadcasted_iota(jnp.int32, (tm, LANES), 1)

    a = _dot(xb, wa_ref[...])
    cosa, sina = cosa_ref[...], sina_ref[...]
    first_a = (lane & (DSA_HEAD_DIM - 1)) < DSA_HEAD_DIM // 2
    for c in range(DSA_WIDTH // LANES):
        sl = slice(c * LANES, (c + 1) * LANES)
        q_ref[:, sl] = _rope_slab(a[:, sl], cosa, sina, first_a, DSA_HEAD_DIM // 2).astype(BF16)
        ks = slice(DSA_WIDTH + c * LANES, DSA_WIDTH + (c + 1) * LANES)
        k_ref[:, sl] = _rope_slab(a[:, ks], cosa, sina, first_a, DSA_HEAD_DIM // 2).astype(BF16)
    v_ref[...] = a[:, 2 * DSA_WIDTH:3 * DSA_WIDTH].astype(BF16)

    ii = _dot(xb, wi_ref[...])
    cosi, sini = cosi_ref[...], sini_ref[...]
    first_i = (lane & (IDX_DIM - 1)) < IDX_DIM // 2
    for c in range(2):
        sl = slice(c * LANES, (c + 1) * LANES)
        iq_ref[:, sl] = _rope_slab(ii[:, sl], cosi, sini, first_i, IDX_DIM // 2).astype(BF16)
    last = ii[:, 2 * LANES:3 * LANES]
    ikw_ref[...] = jnp.where(lane < IDX_DIM, _rope_slab(last, cosi, sini, first_i, IDX_DIM // 2), last)

    b = _dot(xb, wb_ref[...])
    bq_ref[...] = b[:, 0:256]
    bk_ref[...] = b[:, 256:512]
    bv_ref[...] = b[:, 512:1024]
    br_ref[...] = b[:, 1024:1536]

    g = _dot(xb, wg_ref[...])
    z = _dot(g.astype(BF16), wgate_ref[...]) + bgate_ref[...]
    log_sig = jnp.minimum(z, 0.0) - jnp.log(1.0 + jnp.exp(-jnp.abs(z)))
    la_ref[...] = log_sig * (1.0 / GLA_TAU)


def _in_proj(x2, wa, wi, wb, wg, wgate, bgate, cosa, sina, cosi, sini, T):
    N = x2.shape[0]
    tm = TOK_TILE
    nt = T // tm
    row = lambda i: (i, 0)
    const = lambda i: (0, 0)
    pos = lambda i: (i % nt, 0)
    outs = [
        (DSA_WIDTH, BF16), (DSA_WIDTH, BF16), (DSA_WIDTH, BF16), (IDX_HEADS * IDX_DIM, BF16), (LANES, F32),
        (256, F32), (256, F32), (512, F32), (512, F32), (256, F32),
    ]
    return pl.pallas_call(
        _in_proj_kernel,
        grid=(N // tm,),
        in_specs=[
            pl.BlockSpec((tm, D_MODEL), row),
            pl.BlockSpec(wa.shape, const), pl.BlockSpec(wi.shape, const), pl.BlockSpec(wb.shape, const),
            pl.BlockSpec(wg.shape, const), pl.BlockSpec(wgate.shape, const), pl.BlockSpec(bgate.shape, const),
            pl.BlockSpec((tm, LANES), pos), pl.BlockSpec((tm, LANES), pos),
            pl.BlockSpec((tm, LANES), pos), pl.BlockSpec((tm, LANES), pos),
        ],
        out_specs=[pl.BlockSpec((tm, w), row) for w, _ in outs],
        out_shape=[jax.ShapeDtypeStruct((N, w), dt) for w, dt in outs],
        compiler_params=_params(("parallel",)),
    )(x2, wa, wi, wb, wg, wgate, bgate, cosa, sina, cosi, sini)


def _dsa_kernel(iqt_ref, ik_ref, iwt_ref, qt_ref, k_ref, vt_ref, o_ref,
                s_scr, qm_scr, st_scr, m_scr, l_scr, acc_scr, sn_scr, *, topk):
    tq = DSA_TILE
    tk = DSA_TILE
    grp = tk // SUBLANES
    i = pl.program_id(1)
    kf = jnp.float32(topk)

    iqt = iqt_ref[0, 0]
    iwt = iwt_ref[0]
    krow = lax.broadcasted_iota(jnp.int32, (tk, tq), 0)
    qcol = lax.broadcasted_iota(jnp.int32, (tk, tq), 1)
    qcol8 = lax.broadcasted_iota(jnp.int32, (SUBLANES, tq), 1)

    def tree(x3, op):
        q4 = grp // 4
        return op(jnp.stack([op(x3[a * q4:(a + 1) * q4], axis=0) for a in range(4)]), axis=0)

    def fold(x, op):
        return tree(x.reshape(grp, SUBLANES, tq), op)

    def spread(x):
        return jnp.broadcast_to(x, (SUBLANES, tq))

    def score_tile(j):
        keys = ik_ref[0, pl.ds(pl.multiple_of(j * tk, tk), tk), :]
        rt = _dot(keys, iqt)
        acc = None
        for h in range(IDX_HEADS):
            term = jnp.maximum(rt[:, h * tq:(h + 1) * tq], 0.0) * iwt[h:h + 1, :]
            acc = term if acc is None else acc + term
        return acc

    def tile_stats(s_hi, s_lo, carry):
        rmax, rmin, cpos, cnn = carry
        return (jnp.maximum(rmax, fold(s_hi, jnp.max)), jnp.minimum(rmin, fold(s_lo, jnp.min)),
                cpos + fold(jnp.where(s_hi > 0.0, 1.0, 0.0), jnp.sum),
                cnn + fold(jnp.where(s_hi >= 0.0, 1.0, 0.0), jnp.sum))

    def score_body(j, carry):
        s = score_tile(j)
        s_scr[j] = s
        return tile_stats(s, s, carry)

    zeros8 = jnp.zeros((SUBLANES, tq), F32)
    init = (jnp.full((SUBLANES, tq), -jnp.inf, F32), jnp.full((SUBLANES, tq), jnp.inf, F32), zeros8, zeros8)
    carry = lax.fori_loop(0, i, score_body, init)
    s = score_tile(i)
    adm = (krow >> 6) <= (qcol >> 6)
    s_adm = jnp.where(adm, s, -jnp.inf)
    s_scr[i] = s_adm
    rmax, rmin, cpos, cnn = tile_stats(s_adm, jnp.where(adm, s, jnp.inf), carry)
    rmax = spread(jnp.max(rmax, axis=0, keepdims=True))
    rmin = spread(jnp.min(rmin, axis=0, keepdims=True))
    cpos = spread(jnp.sum(cpos, axis=0, keepdims=True))
    cnn = spread(jnp.sum(cnn, axis=0, keepdims=True))

    def count(preds):
        def body(j, accs):
            t3 = s_scr[j].reshape(grp, SUBLANES, tq)
            return tuple(a + tree(jnp.where(p(t3, j), 1.0, 0.0), jnp.sum) for a, p in zip(accs, preds))
        accs = lax.fori_loop(0, i + 1, body, tuple(jnp.zeros((SUBLANES, tq), F32) for _ in preds))
        return [spread(jnp.sum(a, axis=0, keepdims=True)) for a in accs]

    n_adm = (((qcol8 >> 6) + 1 + i * (tq // CHUNK)) * CHUNK).astype(F32)
    search = n_adm > kf
    positive = cpos >= kf
    negative = cnn < kf
    lo = jnp.where(positive, TINY, jnp.where(negative, rmin, 0.0))
    clo = jnp.where(positive, cpos, jnp.where(negative, n_adm, cnn))
    st_scr[0] = jnp.where(search, lo, rmin)
    st_scr[1] = jnp.where(negative, -TINY, rmax + (jnp.abs(rmax) * (2.0 ** -10) + TINY))
    st_scr[2] = jnp.where(search, clo, n_adm)
    zero_thr = jnp.logical_and(jnp.logical_not(positive), jnp.logical_not(negative))
    st_scr[3] = jnp.where(jnp.logical_or(jnp.logical_not(search), jnp.logical_or(zero_thr, clo == kf)), 1.0, 0.0)

    def bis_cond(carry):
        it, pending = carry
        return jnp.logical_and(pending > 0.0, it < BISECT_CAP)

    def bis_body(carry):
        it, _ = carry
        lo, hi, clo, done = st_scr[0], st_scr[1], st_scr[2], st_scr[3]
        a = lax.bitcast_convert_type(jnp.abs(lo), jnp.int32)
        b = lax.bitcast_convert_type(jnp.abs(hi), jnp.int32)
        geo = lax.bitcast_convert_type(a + ((b - a) >> 1), F32)
        geo = jnp.where(hi > 0.0, geo, -geo)
        mid = jnp.where(it < BISECT_ARITH, lo + (hi - lo) * 0.5, geo)
        stuck = jnp.logical_or(mid <= lo, mid >= hi)
        cnt, = count([lambda t, j: t >= mid[None]])
        live = jnp.logical_and(done == 0.0, jnp.logical_not(stuck))
        up = jnp.logical_and(live, cnt >= kf)
        dn = jnp.logical_and(live, cnt < kf)
        lo = jnp.where(up, mid, lo)
        clo = jnp.where(up, cnt, clo)
        hi = jnp.where(dn, mid, hi)
        done = jnp.where(jnp.logical_or(stuck, clo == kf), 1.0, done)
        st_scr[0], st_scr[1], st_scr[2], st_scr[3] = lo, hi, clo, done
        return it + 1, jnp.max(1.0 - done)

    lax.while_loop(bis_cond, bis_body, (jnp.int32(0), jnp.max(1.0 - st_scr[3])))
    thr = st_scr[0]
    clo = st_scr[2]

    @pl.when(jnp.max(clo) > kf)
    def _():
        cgt, = count([lambda t, j: t > thr[None]])
        need = kf - cgt
        tied_value = jnp.where(clo > kf, thr, jnp.nan)
        lower = (lax.broadcasted_iota(jnp.int32, (tk, tk), 1)
                 <= lax.broadcasted_iota(jnp.int32, (tk, tk), 0))
        prefix_matrix = jnp.where(lower, 1.0, 0.0).astype(BF16)

        def drop(j, seen):
            t3 = s_scr[j].reshape(grp, SUBLANES, tq)
            tied = t3 == tied_value[None]
            ones = jnp.where(tied, 1.0, 0.0).reshape(tk, tq).astype(BF16)
            prefix = _dot(prefix_matrix, ones)
            rank = prefix.reshape(grp, SUBLANES, tq) + seen[None]
            kill = jnp.logical_and(tied, rank > need[None])
            s_scr[j] = jnp.where(kill, -jnp.inf, t3).reshape(tk, tq)
            return seen + spread(prefix[tk - 1:tk, :])

        lax.fori_loop(0, i + 1, drop, jnp.zeros((SUBLANES, tq), F32))

    hrow = lax.broadcasted_iota(jnp.int32, (LANES, tq), 0)
    for h in range(DSA_HEADS):
        slab = qt_ref[0, (h // 2) * LANES:(h // 2 + 1) * LANES, :]
        mine = (hrow >= DSA_HEAD_DIM) if (h % 2) else (hrow < DSA_HEAD_DIM)
        qm_scr[h] = jnp.where(mine, slab, jnp.zeros_like(slab))
        m_scr[h] = jnp.full((SUBLANES, tq), NEG_BIG, F32)
        l_scr[h] = jnp.zeros((SUBLANES, tq), F32)
        acc_scr[h] = jnp.zeros((LANES, tq), F32)

    def slab(h):
        return slice((h // 2) * LANES, (h // 2 + 1) * LANES)

    def tile_operands(j):
        t3 = s_scr[j].reshape(grp, SUBLANES, tq)
        bias = jnp.where(t3 >= thr[None], 0.0, NEG_BIG).reshape(tk, tq)
        return k_ref[0, pl.ds(pl.multiple_of(j * tk, tk), tk), :], bias

    def masked_scores(keys, bias, h):
        return _dot(keys[:, slab(h)], qm_scr[h]) + bias

    keys0, bias0 = tile_operands(0)
    for h in range(DSA_HEADS):
        sn_scr[h] = masked_scores(keys0, bias0, h)

    def att_body(j, _):
        keys_n, bias_n = tile_operands(jnp.minimum(j + 1, i))
        for h in range(DSA_HEADS):
            s3 = sn_scr[h].reshape(grp, SUBLANES, tq)
            m_prev = m_scr[h]
            m_new = jnp.maximum(m_prev, spread(jnp.max(tree(s3, jnp.max), axis=0, keepdims=True)))
            p3 = jnp.exp2(s3 - m_new[None])
            corr = jnp.exp2(m_prev - m_new)
            l_scr[h] = corr * l_scr[h] + tree(p3, jnp.sum)
            pv = _dot(vt_ref[0, j, slab(h), :], p3.reshape(tk, tq).astype(BF16))
            acc = acc_scr[h].reshape(LANES // SUBLANES, SUBLANES, tq) * corr[None]
            acc_scr[h] = acc.reshape(LANES, tq) + pv
            m_scr[h] = m_new
            sn_scr[h] = masked_scores(keys_n, bias_n, h)
        return 0

    lax.fori_loop(0, i + 1, att_body, 0)

    def normalised(h):
        inv = 1.0 / spread(jnp.sum(l_scr[h], axis=0, keepdims=True))
        return (acc_scr[h].reshape(LANES // SUBLANES, SUBLANES, tq) * inv[None]).reshape(LANES, tq)

    for sp in range(DSA_HEADS // 2):
        pair = jnp.where(hrow < DSA_HEAD_DIM, normalised(2 * sp), normalised(2 * sp + 1))
        o_ref[0, sp * LANES:(sp + 1) * LANES, :] = pair.astype(o_ref.dtype)


def _dsa(iqt, ik, iwt, qt, k, vt):
    B, T, _ = k.shape
    tq = DSA_TILE
    nq = T // tq
    topk = min(DSA_TOPK_MAX, T // 4)
    kern = functools.partial(_dsa_kernel, topk=topk)
    return pl.pallas_call(
        kern,
        grid=(B, nq),
        in_specs=[
            pl.BlockSpec((1, 1, IDX_DIM, IDX_HEADS * tq), lambda b, i: (b, i, 0, 0)),
            pl.BlockSpec((1, T, IDX_DIM), lambda b, i: (b, 0, 0)),
            pl.BlockSpec((1, IDX_HEADS, tq), lambda b, i: (b, 0, i)),
            pl.BlockSpec((1, DSA_WIDTH, tq), lambda b, i: (b, 0, i)),
            pl.BlockSpec((1, T, DSA_WIDTH), lambda b, i: (b, 0, 0)),
            pl.BlockSpec((1, nq, DSA_WIDTH, tq), lambda b, i: (b, 0, 0, 0)),
        ],
        out_specs=pl.BlockSpec((1, DSA_WIDTH, tq), lambda b, i: (b, 0, i)),
        out_shape=jax.ShapeDtypeStruct((B, DSA_WIDTH, T), BF16),
        scratch_shapes=[
            pltpu.VMEM((nq, tq, tq), F32),
            pltpu.VMEM((DSA_HEADS, LANES, tq), BF16),
            pltpu.VMEM((4, SUBLANES, tq), F32),
            pltpu.VMEM((DSA_HEADS, SUBLANES, tq), F32),
            pltpu.VMEM((DSA_HEADS, SUBLANES, tq), F32),
            pltpu.VMEM((DSA_HEADS, LANES, tq), F32),
            pltpu.VMEM((DSA_HEADS, tq, tq), F32),
        ],
        compiler_params=_params(("parallel", "arbitrary")),
    )(iqt, ik, iwt, qt, k, vt)


def _gla_kernel(q_ref, k_ref, v_ref, r_ref, la_ref, g_ref, o_ref, s_scr):
    rows = GLA_BLOCK
    nch = rows // CHUNK
    npair = GLA_HEADS // 2

    @pl.when(pl.program_id(1) == 0)
    def _():
        s_scr[...] = jnp.zeros_like(s_scr)

    ri = lax.broadcasted_iota(jnp.int32, (rows, rows), 0)
    ci = lax.broadcasted_iota(jnp.int32, (rows, rows), 1)
    causal = jnp.logical_and((ri >> 6) == (ci >> 6), ci <= ri)
    tri = jnp.where(causal, 1.0, 0.0).astype(BF16)

    la = la_ref[0]
    la_hi = la.astype(BF16)
    rem = la - la_hi.astype(F32)
    la_mid = rem.astype(BF16)
    la_lo = (rem - la_mid.astype(F32)).astype(BF16)
    b = _dot(tri, la_hi) + _dot(tri, la_mid) + _dot(tri, la_lo)
    b_last = jnp.concatenate(
        [jnp.broadcast_to(b[(c + 1) * CHUNK - 1:(c + 1) * CHUNK, :], (CHUNK, b.shape[1])) for c in range(nch)], axis=0)

    q = q_ref[0]
    k = k_ref[0]
    qg = q * jnp.exp(b)
    kg = (k * jnp.exp(-b)).astype(BF16)
    kd = (k * jnp.exp(b_last - b)).astype(BF16)
    decay_rows = jnp.exp(b_last)

    lane = lax.broadcasted_iota(jnp.int32, (rows, LANES), 1)
    eye = lax.broadcasted_iota(jnp.int32, (LANES, LANES), 0) == lax.broadcasted_iota(jnp.int32, (LANES, LANES), 1)
    top_rows = lax.broadcasted_iota(jnp.int32, (LANES, GLA_DV), 0) < GLA_DK

    for pr in range(npair):
        sl = slice(pr * LANES, (pr + 1) * LANES)
        qg_p = qg[:, sl]
        kg_p = kg[:, sl]
        kd_p = kd[:, sl]
        heads = (2 * pr, 2 * pr + 1)
        qg_h = [jnp.where(lane < GLA_DK, qg_p, 0.0).astype(BF16), jnp.where(lane >= GLA_DK, qg_p, 0.0).astype(BF16)]
        v_h = [v_ref[0, :, h * GLA_DV:(h + 1) * GLA_DV].astype(BF16) for h in heads]
        o_h = []
        for t in range(2):
            a = jnp.where(causal, _dot_nt(qg_h[t], kg_p), 0.0)
            o_h.append(_dot(a.astype(BF16), v_h[t]))
        state = s_scr[pr]
        inter = [[], []]
        for c in range(nch):
            rs = slice(c * CHUNK, (c + 1) * CHUNK)
            sb = state.astype(BF16)
            for t in range(2):
                inter[t].append(_dot(qg_h[t][rs], sb))
            kv = jnp.where(top_rows, _dot_tn(kd_p[rs], v_h[0][rs]), _dot_tn(kd_p[rs], v_h[1][rs]))
            drow = jnp.broadcast_to(decay_rows[c * CHUNK:c * CHUNK + 1, sl], (LANES, LANES))
            dcol = jnp.sum(jnp.where(eye, drow, 0.0), axis=1, keepdims=True)
            state = dcol * state + kv
        s_scr[pr] = state
        for t in range(2):
            h = heads[t]
            o = o_h[t] + jnp.concatenate(inter[t], axis=0)
            ms = jnp.mean(o * o, axis=1, keepdims=True)
            on = o * lax.rsqrt(ms + LN_EPS) * g_ref[...]
            r = r_ref[0, :, h * GLA_DV:(h + 1) * GLA_DV]
            o_ref[0, :, h * GLA_DV:(h + 1) * GLA_DV] = (on * (r / (1.0 + jnp.exp(-r)))).astype(o_ref.dtype)


def _gla(bq, bk, bv, br, la, g):
    B, T, _ = bq.shape
    rows = GLA_BLOCK
    blk = lambda w: pl.BlockSpec((1, rows, w), lambda b, i: (b, i, 0))
    return pl.pallas_call(
        _gla_kernel,
        grid=(B, T // rows),
        in_specs=[blk(256), blk(256), blk(512), blk(512), blk(256), pl.BlockSpec((1, GLA_DV), lambda b, i: (0, 0))],
        out_specs=blk(GLA_WIDTH),
        out_shape=jax.ShapeDtypeStruct((B, T, GLA_WIDTH), BF16),
        scratch_shapes=[pltpu.VMEM((GLA_HEADS // 2, 2 * GLA_DK, GLA_DV), F32)],
        compiler_params=_params(("parallel", "arbitrary")),
    )(bq, bk, bv, br, la, g)


def _layer_norm(x, g, b):
    mu = jnp.mean(x, axis=1, keepdims=True)
    xc = x - mu
    var = jnp.mean(xc * xc, axis=1, keepdims=True)
    return xc * lax.rsqrt(var + LN_EPS) * g + b


def _out_proj_kernel(x_ref, ya_ref, yb_ref, wo_ref, g_ref, b_ref, wr_hi_ref, wr_lo_ref, br_ref, h_ref, route_ref):
    tm = x_ref.shape[0]
    mix = _dot(ya_ref[...], wo_ref[0:DSA_WIDTH, :]) + _dot(yb_ref[...], wo_ref[DSA_WIDTH:D_MODEL, :])
    h = _layer_norm(DEEPNORM_ALPHA * x_ref[...] + mix, g_ref[...], b_ref[...])
    h_ref[...] = h

    h_hi = h.astype(BF16)
    h_lo = (h - h_hi.astype(F32)).astype(BF16)
    logits = _dot(h_hi, wr_hi_ref[...]) + _dot(h_lo, wr_hi_ref[...]) + _dot(h_hi, wr_lo_ref[...]) + br_ref[...]

    lane = lax.broadcasted_iota(jnp.int32, (tm, LANES), 1)
    lanef = lane.astype(F32)
    gl = jnp.where(lane < N_GROUPS, logits, -jnp.inf)
    gmax = jnp.max(gl, axis=1, keepdims=True)
    gsel = jnp.min(jnp.where(gl == gmax, lanef, 1e9), axis=1, keepdims=True)
    pg = 1.0 / jnp.sum(jnp.exp(gl - gmax), axis=1, keepdims=True)
    egrp = ((lane - N_GROUPS) >> 3).astype(F32)
    in_grp = jnp.logical_and(jnp.logical_and(lane >= N_GROUPS, lane < N_GROUPS + N_EXPERTS), egrp == gsel)
    el = jnp.where(in_grp, logits, -jnp.inf)
    t1 = jnp.max(el, axis=1, keepdims=True)
    i1 = jnp.min(jnp.where(el == t1, lanef, 1e9), axis=1, keepdims=True)
    el2 = jnp.where(lanef == i1, -jnp.inf, el)
    t2 = jnp.max(el2, axis=1, keepdims=True)
    i2 = jnp.min(jnp.where(el2 == t2, lanef, 1e9), axis=1, keepdims=True)
    e21 = jnp.exp(t2 - t1)
    g1 = pg / (1.0 + e21)
    g2 = pg * e21 / (1.0 + e21)
    route = jnp.where(lane == 0, i1 - N_GROUPS, 0.0)
    route = jnp.where(lane == 1, i2 - N_GROUPS, route)
    route = jnp.where(lane == 2, g1, route)
    route = jnp.where(lane == 3, g2, route)
    route_ref[...] = route


def _out_proj(x2, ya, yb, wo, g1, b1, wr_hi, wr_lo, br):
    N = x2.shape[0]
    tm = TOK_TILE
    row = lambda i: (i, 0)
    const = lambda i: (0, 0)
    return pl.pallas_call(
        _out_proj_kernel,
        grid=(N // tm,),
        in_specs=[
            pl.BlockSpec((tm, D_MODEL), row), pl.BlockSpec((tm, DSA_WIDTH), row), pl.BlockSpec((tm, GLA_WIDTH), row),
            pl.BlockSpec(wo.shape, const), pl.BlockSpec(g1.shape, const), pl.BlockSpec(b1.shape, const),
            pl.BlockSpec(wr_hi.shape, const), pl.BlockSpec(wr_lo.shape, const), pl.BlockSpec(br.shape, const),
        ],
        out_specs=[pl.BlockSpec((tm, D_MODEL), row), pl.BlockSpec((tm, LANES), row)],
        out_shape=[jax.ShapeDtypeStruct((N, D_MODEL), F32), jax.ShapeDtypeStruct((N, LANES), F32)],
        compiler_params=_params(("parallel",)),
    )(x2, ya, yb, wo, g1, b1, wr_hi, wr_lo, br)


def _rank_kernel(route_ref, rank_ref, cnt_ref, carry_scr):
    tm = route_ref.shape[0]

    @pl.when(pl.program_id(0) == 0)
    def _():
        carry_scr[...] = jnp.zeros_like(carry_scr)

    route = route_ref[...]
    lanef = lax.broadcasted_iota(jnp.int32, (tm, LANES), 1).astype(F32)
    e1 = route[:, 0:1]
    e2 = route[:, 1:2]
    hit1 = lanef == e1
    hit2 = lanef == e2
    onehot = jnp.where(jnp.logical_or(hit1, hit2), 1.0, 0.0).astype(BF16)
    ri = lax.broadcasted_iota(jnp.int32, (tm, tm), 0)
    ci = lax.broadcasted_iota(jnp.int32, (tm, tm), 1)
    before = jnp.where(ci < ri, 1.0, 0.0).astype(BF16)
    prefix = _dot(before, onehot) + carry_scr[0:1, :]
    r1 = jnp.sum(jnp.where(hit1, prefix, 0.0), axis=1, keepdims=True)
    r2 = jnp.sum(jnp.where(hit2, prefix, 0.0), axis=1, keepdims=True)
    rank_ref[...] = jnp.where(lanef == 0.0, r1, jnp.where(lanef == 1.0, r2, 0.0))
    total = _dot(jnp.ones((8, tm), BF16), onehot)
    carry_scr[...] = carry_scr[...] + total
    cnt_ref[...] = carry_scr[...]


def _rank(route):
    N = route.shape[0]
    tm = RANK_TILE
    return pl.pallas_call(
        _rank_kernel,
        grid=(N // tm,),
        in_specs=[pl.BlockSpec((tm, LANES), lambda i: (i, 0))],
        out_specs=[pl.BlockSpec((tm, LANES), lambda i: (i, 0)), pl.BlockSpec((8, LANES), lambda i: (0, 0))],
        out_shape=[jax.ShapeDtypeStruct((N, LANES), F32), jax.ShapeDtypeStruct((8, LANES), F32)],
        scratch_shapes=[pltpu.VMEM((8, LANES), F32)],
        compiler_params=_params(("arbitrary",)),
    )(route)


def _dispatch_kernel(pos_ref, h_ref, xs_in_ref, xs_ref, hbuf, load_sem, row_sem):
    del xs_in_ref
    tm = DISPATCH_TILE
    nbuf = hbuf.shape[0]
    i = pl.program_id(0)
    last = pl.num_programs(0) - 1

    def tile_load(step):
        return pltpu.make_async_copy(h_ref.at[pl.ds(step * tm, tm), :], hbuf.at[step % nbuf], load_sem.at[step % nbuf])

    def row_copy(step, r, slot):
        p = pos_ref[2 * (step * tm + r) + slot]
        return pltpu.make_async_copy(hbuf.at[step % nbuf, pl.ds(r, 1), :], xs_ref.at[pl.ds(p, 1), :],
                                     row_sem.at[step % nbuf])

    def for_rows(fn):
        def body(r, _):
            fn(r, 0)
            fn(r, 1)
            return 0
        lax.fori_loop(0, tm, body, 0, unroll=DMA_UNROLL)

    @pl.when(i == 0)
    def _():
        tile_load(0).start()

    @pl.when(i < last)
    def _():
        tile_load(i + 1).start()

    tile_load(i).wait()
    for_rows(lambda r, slot: row_copy(i, r, slot).start())

    @pl.when(i > 0)
    def _():
        for_rows(lambda r, slot: row_copy(i - 1, r, slot).wait())

    @pl.when(i == last)
    def _():
        for_rows(lambda r, slot: row_copy(i, r, slot).wait())


def _dispatch(pos_flat, h, xs_init):
    N = h.shape[0]
    grid_spec = pltpu.PrefetchScalarGridSpec(
        num_scalar_prefetch=1,
        grid=(N // DISPATCH_TILE,),
        in_specs=[pl.BlockSpec(memory_space=pl.ANY), pl.BlockSpec(memory_space=pl.ANY)],
        out_specs=pl.BlockSpec(memory_space=pl.ANY),
        scratch_shapes=[pltpu.VMEM((3, DISPATCH_TILE, D_MODEL), F32),
                        pltpu.SemaphoreType.DMA((3,)), pltpu.SemaphoreType.DMA((3,))],
    )
    return pl.pallas_call(
        _dispatch_kernel,
        grid_spec=grid_spec,
        out_shape=jax.ShapeDtypeStruct(xs_init.shape, xs_init.dtype),
        input_output_aliases={2: 0},
        compiler_params=_params(("arbitrary",)),
    )(pos_flat, h, xs_init)


def _ffn_kernel(te_ref, na_ref, x_ref, wi_ref, wo_ref, y_ref, wi_bf, wo_bf):
    i = pl.program_id(0)
    active = i < na_ref[0]
    fresh = jnp.logical_or(i == 0, te_ref[i] != te_ref[jnp.maximum(i - 1, 0)])

    @pl.when(jnp.logical_and(active, fresh))
    def _():
        wi_bf[...] = wi_ref[0].astype(BF16)
        wo_bf[...] = wo_ref[0].astype(BF16)

    @pl.when(active)
    def _():
        hid = _dot(x_ref[...].astype(BF16), wi_bf[...])
        hg = hid[:, :D_EXPERT]
        hu = hid[:, D_EXPERT:]
        act = (hg / (1.0 + jnp.exp(-hg))) * hu
        y_ref[...] = _dot(act.astype(BF16), wo_bf[...])

    @pl.when(jnp.logical_not(active))
    def _():
        y_ref[...] = jnp.zeros_like(y_ref)


def _ffn(tile_expert, n_active, xs, w_e_in, w_e_out):
    R = xs.shape[0]
    tm = MOE_TILE

    def live(i, te, na):
        return jnp.minimum(i, na[0] - 1)

    grid_spec = pltpu.PrefetchScalarGridSpec(
        num_scalar_prefetch=2,
        grid=(R // tm,),
        in_specs=[
            pl.BlockSpec((tm, D_MODEL), lambda i, te, na: (live(i, te, na), 0)),
            pl.BlockSpec((1, D_MODEL, 2 * D_EXPERT), lambda i, te, na: (te[live(i, te, na)], 0, 0)),
            pl.BlockSpec((1, D_EXPERT, D_MODEL), lambda i, te, na: (te[live(i, te, na)], 0, 0)),
        ],
        out_specs=pl.BlockSpec((tm, D_MODEL), lambda i, te, na: (i, 0)),
        scratch_shapes=[pltpu.VMEM((D_MODEL, 2 * D_EXPERT), BF16), pltpu.VMEM((D_EXPERT, D_MODEL), BF16)],
    )
    return pl.pallas_call(
        _ffn_kernel,
        grid_spec=grid_spec,
        out_shape=jax.ShapeDtypeStruct((R, D_MODEL), F32),
        compiler_params=_params(("arbitrary",)),
    )(tile_expert, n_active, xs, w_e_in, w_e_out)


def _combine_kernel(pos_ref, h_ref, route_ref, g_ref, b_ref, ys_ref, o_ref, buf, sem):
    tm = h_ref.shape[0]
    i = pl.program_id(0)
    cur = i % 2

    def row_copy(step, r, slot):
        p = pos_ref[2 * (step * tm + r) + slot]
        half = step % 2
        return pltpu.make_async_copy(ys_ref.at[pl.ds(p, 1), :], buf.at[half, slot, pl.ds(r, 1), :], sem.at[half])

    def for_rows(fn):
        def body(r, _):
            fn(r, 0)
            fn(r, 1)
            return 0
        lax.fori_loop(0, tm, body, 0, unroll=DMA_UNROLL)

    @pl.when(i == 0)
    def _():
        for_rows(lambda r, slot: row_copy(0, r, slot).start())

    @pl.when(i + 1 < pl.num_programs(0))
    def _():
        for_rows(lambda r, slot: row_copy(i + 1, r, slot).start())

    for_rows(lambda r, slot: row_copy(i, r, slot).wait())
    route = route_ref[...]
    ffn = buf[cur, 0] * route[:, 2:3] + buf[cur, 1] * route[:, 3:4]
    o_ref[...] = _layer_norm(DEEPNORM_ALPHA * h_ref[...] + ffn, g_ref[...], b_ref[...])


def _combine(pos_flat, h, route, g2, b2, ys):
    N = h.shape[0]
    tm = TOK_TILE
    row = lambda i, pos: (i, 0)
    const = lambda i, pos: (0, 0)
    grid_spec = pltpu.PrefetchScalarGridSpec(
        num_scalar_prefetch=1,
        grid=(N // tm,),
        in_specs=[
            pl.BlockSpec((tm, D_MODEL), row), pl.BlockSpec((tm, LANES), row),
            pl.BlockSpec(g2.shape, const), pl.BlockSpec(b2.shape, const),
            pl.BlockSpec(memory_space=pl.ANY),
        ],
        out_specs=pl.BlockSpec((tm, D_MODEL), row),
        scratch_shapes=[pltpu.VMEM((2, 2, tm, D_MODEL), F32), pltpu.SemaphoreType.DMA((2,))],
    )
    return pl.pallas_call(
        _combine_kernel,
        grid_spec=grid_spec,
        out_shape=jax.ShapeDtypeStruct((N, D_MODEL), F32),
        compiler_params=_params(("arbitrary",)),
    )(pos_flat, h, route, g2, b2, ys)


def _rope_tables(T, dim):
    half = dim // 2
    inv = 1.0 / (ROPE_THETA ** (jnp.arange(half, dtype=F32) / half))
    ang = jnp.arange(T).astype(F32)[:, None] * inv[None, :]
    cos = jnp.cos(ang)
    sin = jnp.sin(ang)
    reps = LANES // dim
    cos_t = jnp.tile(jnp.concatenate([cos, cos], axis=1), (1, reps))
    sin_t = jnp.tile(jnp.concatenate([-sin, sin], axis=1), (1, reps))
    return cos_t, sin_t


def _pad_cols(w, width):
    return jnp.pad(w, ((0, 0), (0, width - w.shape[1])))


def _layer(x, w_in, w_gla_gate, b_gla_gate, g_gla_norm, w_out, ln1_g, ln1_b,
           w_gr, b_gr, w_er, b_er, w_e_in, w_e_out, ln2_g, ln2_b):
    B, T, D = x.shape
    N = B * T
    assert D == D_MODEL and T % DSA_TILE == 0 and N % DISPATCH_TILE == 0 and DISPATCH_TILE % RANK_TILE == 0
    x2 = x.reshape(N, D)

    sizes = (512, 512, 512, 256, 32, 8, 256, 256, 512, 512, 16)
    offs = np.concatenate([[0], np.cumsum(sizes)])
    col = lambda k: w_in[:, offs[k]:offs[k + 1]]
    wa = jnp.concatenate([col(0) * (DSA_HEAD_DIM ** -0.5 * LOG2E), col(1), col(2)], axis=1).astype(BF16)
    wi = _pad_cols(jnp.concatenate([col(3), col(4), col(5) * IDX_SCALE], axis=1), 3 * LANES).astype(BF16)
    wb = jnp.concatenate([col(6) * (GLA_DK ** -0.5), col(7), col(8), col(9)], axis=1).astype(BF16)
    wg = _pad_cols(col(10), LANES).astype(BF16)
    wgate = jnp.pad(w_gla_gate, ((0, LANES - GLA_GATE_RANK), (0, 0))).astype(BF16)
    bgate = b_gla_gate.reshape(1, -1)
    cosa, sina = _rope_tables(T, DSA_HEAD_DIM)
    cosi, sini = _rope_tables(T, IDX_DIM)

    q, k, v, iq, ikw, bq, bk, bv, br, la = _in_proj(x2, wa, wi, wb, wg, wgate, bgate, cosa, sina, cosi, sini, T)

    nq = T // DSA_TILE
    iqt = iq.reshape(B, nq, DSA_TILE, IDX_HEADS, IDX_DIM).transpose(0, 1, 4, 3, 2)
    iqt = iqt.reshape(B, nq, IDX_DIM, IDX_HEADS * DSA_TILE)
    ik = ikw[:, :IDX_DIM].astype(BF16).reshape(B, T, IDX_DIM)
    iwt = ikw[:, IDX_DIM:IDX_DIM + IDX_HEADS].reshape(B, T, IDX_HEADS).transpose(0, 2, 1)
    qt = q.reshape(B, T, DSA_WIDTH).transpose(0, 2, 1)
    vt = v.reshape(B, nq, DSA_TILE, DSA_WIDTH).transpose(0, 1, 3, 2)
    ya = _dsa(iqt, ik, iwt, qt, k.reshape(B, T, DSA_WIDTH), vt).transpose(0, 2, 1)

    r3 = lambda a: a.reshape(B, T, a.shape[-1])
    yb = _gla(r3(bq), r3(bk), r3(bv), r3(br), r3(la), g_gla_norm.reshape(1, GLA_DV))

    wr = _pad_cols(jnp.concatenate([w_gr, w_er], axis=1), LANES)
    wr_hi = wr.astype(BF16)
    wr_lo = (wr - wr_hi.astype(F32)).astype(BF16)
    brt = _pad_cols(jnp.concatenate([b_gr, b_er]).reshape(1, -1), LANES)
    h, route = _out_proj(x2, ya.reshape(N, DSA_WIDTH), yb.reshape(N, GLA_WIDTH), w_out.astype(BF16),
                         ln1_g.reshape(1, D), ln1_b.reshape(1, D), wr_hi, wr_lo, brt)

    rank, cnt = _rank(route)
    counts = cnt[0, :N_EXPERTS].astype(jnp.int32)
    padded = ((counts + MOE_TILE - 1) // MOE_TILE) * MOE_TILE
    ends = jnp.cumsum(padded)
    starts = ends - padded
    eid = route[:, 0:2].astype(jnp.int32)
    pos = (starts[eid] + rank[:, 0:2].astype(jnp.int32)).reshape(-1)
    n_rows = 2 * N + N_EXPERTS * MOE_TILE
    n_tiles = n_rows // MOE_TILE
    tile_start = jnp.arange(n_tiles, dtype=jnp.int32) * MOE_TILE
    tile_expert = jnp.minimum(jnp.sum(tile_start[:, None] >= ends[None, :], axis=1), N_EXPERTS - 1).astype(jnp.int32)
    n_active = (ends[-1] // MOE_TILE).astype(jnp.int32).reshape(1)

    xs = _dispatch(pos, h, jnp.zeros((n_rows, D), F32))
    ys = _ffn(tile_expert, n_active, xs, w_e_in, w_e_out)
    out = _combine(pos, h, route, ln2_g.reshape(1, D), ln2_b.reshape(1, D), ys)
    return out.reshape(B, T, D)


def kernel(x, w_in, w_gla_gate, b_gla_gate, g_gla_norm, w_out, ln1_g, ln1_b, w_group_router, b_group_router,
           w_expert_router, b_expert_router, w_expert_in, w_expert_out, ln2_g, ln2_b):
    h = x
    for l in range(w_in.shape[0]):
        h = _layer(h, w_in[l], w_gla_gate[l], b_gla_gate[l], g_gla_norm[l], w_out[l], ln1_g[l], ln1_b[l],
                   w_group_router[l], b_group_router[l], w_expert_router[l], b_expert_router[l],
                   w_expert_in[l], w_expert_out[l], ln2_g[l], ln2_b[l])
    return h
```

```python
import functools

import numpy as np
import jax
import jax.numpy as jnp
from jax import lax
from jax.experimental import pallas as pl
from jax.experimental.pallas import tpu as pltpu

F32 = jnp.float32
BF16 = jnp.bfloat16

D_MODEL = 1024
CHUNK = 64
ROPE_THETA = 10000.0
LN_EPS = 1e-5
DSA_WIDTH = 512
DSA_HEAD_DIM = 64
DSA_HEADS = 8
IDX_HEADS = 8
IDX_DIM = 32
IDX_SCALE = (IDX_HEADS * IDX_DIM) ** -0.5
DSA_TOPK_MAX = 256
GLA_WIDTH = 512
GLA_HEADS = 4
GLA_DV = 128
GLA_DK = 64
GLA_GATE_RANK = 16
GLA_TAU = 16.0
N_GROUPS = 4
EXPERTS_PER_GROUP = 8
N_EXPERTS = 32
D_EXPERT = 512
DEEPNORM_ALPHA = 2.0 ** 0.25

LANES = 128
SUBLANES = 8
TINY = 2.0 ** -126
NEG_BIG = -1e30
VMEM_LIMIT = 56 * 1024 * 1024

TOK_TILE = 512
DSA_TILE = 256
GLA_BLOCK = 256
MOE_TILE = 512
RANK_TILE = 512
DISPATCH_TILE = 512
DMA_UNROLL = 8
LOG2E = 1.4426950408889634
BISECT_ARITH = 40
BISECT_CAP = 80


def _dot(a, b):
    return jnp.dot(a, b, preferred_element_type=F32)


def _dot_nt(a, b):
    return lax.dot_general(a, b, (((1,), (1,)), ((), ())), preferred_element_type=F32)


def _dot_tn(a, b):
    return lax.dot_general(a, b, (((0,), (0,)), ((), ())), preferred_element_type=F32)


def _params(sem):
    return pltpu.CompilerParams(dimension_semantics=sem, vmem_limit_bytes=VMEM_LIMIT)


def _rope_slab(slab, cos, sin, first_half, half):
    swapped = jnp.where(first_half, pltpu.roll(slab, LANES - half, 1), pltpu.roll(slab, half, 1))
    return slab * cos + swapped * sin


def _in_proj_kernel(x_ref, wa_ref, wi_ref, wb_ref, wg_ref, wgate_ref, bgate_ref,
                    cosa_ref, sina_ref, cosi_ref, sini_ref,
                    q_ref, k_ref, v_ref, iq_ref, ikw_ref, bq_ref, bk_ref, bv_ref, br_ref, la_ref):
    tm = x_ref.shape[0]
    xb = x_ref[...].astype(BF16)
    lane = lax.broadcasted_iota(jnp.int32, (tm, LANES), 1)

    a = _dot(xb, wa_ref[...])
    cosa, sina = cosa_ref[...], sina_ref[...]
    first_a = (lane & (DSA_HEAD_DIM - 1)) < DSA_HEAD_DIM // 2
    for c in range(DSA_WIDTH // LANES):
        sl = slice(c * LANES, (c + 1) * LANES)
        q_ref[:, sl] = _rope_slab(a[:, sl], cosa, sina, first_a, DSA_HEAD_DIM // 2).astype(BF16)
        ks = slice(DSA_WIDTH + c * LANES, DSA_WIDTH + (c + 1) * LANES)
        k_ref[:, sl] = _rope_slab(a[:, ks], cosa, sina, first_a, DSA_HEAD_DIM // 2).astype(BF16)
    v_ref[...] = a[:, 2 * DSA_WIDTH:3 * DSA_WIDTH].astype(BF16)

    ii = _dot(xb, wi_ref[...])
    cosi, sini = cosi_ref[...], sini_ref[...]
    first_i = (lane & (IDX_DIM - 1)) < IDX_DIM // 2
    for c in range(2):
        sl = slice(c * LANES, (c + 1) * LANES)
        iq_ref[:, sl] = _rope_slab(ii[:, sl], cosi, sini, first_i, IDX_DIM // 2).astype(BF16)
    last = ii[:, 2 * LANES:3 * LANES]
    ikw_ref[...] = jnp.where(lane < IDX_DIM, _rope_slab(last, cosi, sini, first_i, IDX_DIM // 2), last)

    b = _dot(xb, wb_ref[...])
    bq_ref[...] = b[:, 0:256]
    bk_ref[...] = b[:, 256:512]
    bv_ref[...] = b[:, 512:1024]
    br_ref[...] = b[:, 1024:1536]

    g = _dot(xb, wg_ref[...])
    z = _dot(g.astype(BF16), wgate_ref[...]) + bgate_ref[...]
    log_sig = jnp.minimum(z, 0.0) - jnp.log(1.0 + jnp.exp(-jnp.abs(z)))
    la_ref[...] = log_sig * (1.0 / GLA_TAU)


def _in_proj(x2, wa, wi, wb, wg, wgate, bgate, cosa, sina, cosi, sini, T):
    N = x2.shape[0]
    tm = TOK_TILE
    nt = T // tm
    row = lambda i: (i, 0)
    const = lambda i: (0, 0)
    pos = lambda i: (i % nt, 0)
    outs = [
        (DSA_WIDTH, BF16), (DSA_WIDTH, BF16), (DSA_WIDTH, BF16), (IDX_HEADS * IDX_DIM, BF16), (LANES, F32),
        (256, F32), (256, F32), (512, F32), (512, F32), (256, F32),
    ]
    return pl.pallas_call(
        _in_proj_kernel,
        grid=(N // tm,),
        in_specs=[
            pl.BlockSpec((tm, D_MODEL), row),
            pl.BlockSpec(wa.shape, const), pl.BlockSpec(wi.shape, const), pl.BlockSpec(wb.shape, const),
            pl.BlockSpec(wg.shape, const), pl.BlockSpec(wgate.shape, const), pl.BlockSpec(bgate.shape, const),
            pl.BlockSpec((tm, LANES), pos), pl.BlockSpec((tm, LANES), pos),
            pl.BlockSpec((tm, LANES), pos), pl.BlockSpec((tm, LANES), pos),
        ],
        out_specs=[pl.BlockSpec((tm, w), row) for w, _ in outs],
        out_shape=[jax.ShapeDtypeStruct((N, w), dt) for w, dt in outs],
        compiler_params=_params(("parallel",)),
    )(x2, wa, wi, wb, wg, wgate, bgate, cosa, sina, cosi, sini)


def _dsa_kernel(iqt_ref, ik_ref, iwt_ref, qt_ref, k_ref, vt_ref, o_ref,
                s_scr, qm_scr, st_scr, m_scr, l_scr, acc_scr, sn_scr, rt_scr, *, topk):
    tq = DSA_TILE
    tk = DSA_TILE
    grp = tk // SUBLANES
    i = pl.program_id(1)
    kf = jnp.float32(topk)

    iqt = iqt_ref[0, 0]
    iwt = iwt_ref[0]
    krow = lax.broadcasted_iota(jnp.int32, (tk, tq), 0)
    qcol = lax.broadcasted_iota(jnp.int32, (tk, tq), 1)
    qcol8 = lax.broadcasted_iota(jnp.int32, (SUBLANES, tq), 1)

    def tree(x3, op):
        q4 = grp // 4
        return op(jnp.stack([op(x3[a * q4:(a + 1) * q4], axis=0) for a in range(4)]), axis=0)

    def fold(x, op):
        return tree(x.reshape(grp, SUBLANES, tq), op)

    def spread(x):
        return jnp.broadcast_to(x, (SUBLANES, tq))

    def head_products(j):
        keys = ik_ref[0, pl.ds(pl.multiple_of(j * tk, tk), tk), :]
        return _dot(keys, iqt)

    def weighted_relu_sum():
        acc = None
        for h in range(IDX_HEADS):
            term = jnp.maximum(rt_scr[:, h * tq:(h + 1) * tq], 0.0) * iwt[h:h + 1, :]
            acc = term if acc is None else acc + term
        return acc

    def tile_stats(s_hi, s_lo, carry):
        rmax, rmin, cpos, cnn = carry
        return (jnp.maximum(rmax, fold(s_hi, jnp.max)), jnp.minimum(rmin, fold(s_lo, jnp.min)),
                cpos + fold(jnp.where(s_hi > 0.0, 1.0, 0.0), jnp.sum),
                cnn + fold(jnp.where(s_hi >= 0.0, 1.0, 0.0), jnp.sum))

    def score_body(j, carry):
        s = weighted_relu_sum()
        s_scr[j] = s
        rt_scr[...] = head_products(j + 1)
        return tile_stats(s, s, carry)

    zeros8 = jnp.zeros((SUBLANES, tq), F32)
    init = (jnp.full((SUBLANES, tq), -jnp.inf, F32), jnp.full((SUBLANES, tq), jnp.inf, F32), zeros8, zeros8)
    rt_scr[...] = head_products(0)
    carry = lax.fori_loop(0, i, score_body, init)
    s = weighted_relu_sum()
    adm = (krow >> 6) <= (qcol >> 6)
    s_adm = jnp.where(adm, s, -jnp.inf)
    s_scr[i] = s_adm
    rmax, rmin, cpos, cnn = tile_stats(s_adm, jnp.where(adm, s, jnp.inf), carry)
    rmax = spread(jnp.max(rmax, axis=0, keepdims=True))
    rmin = spread(jnp.min(rmin, axis=0, keepdims=True))
    cpos = spread(jnp.sum(cpos, axis=0, keepdims=True))
    cnn = spread(jnp.sum(cnn, axis=0, keepdims=True))

    def count(preds):
        def body(j, accs):
            t3 = s_scr[j].reshape(grp, SUBLANES, tq)
            return tuple(a + tree(jnp.where(p(t3, j), 1.0, 0.0), jnp.sum) for a, p in zip(accs, preds))
        accs = lax.fori_loop(0, i + 1, body, tuple(jnp.zeros((SUBLANES, tq), F32) for _ in preds))
        return [spread(jnp.sum(a, axis=0, keepdims=True)) for a in accs]

    n_adm = (((qcol8 >> 6) + 1 + i * (tq // CHUNK)) * CHUNK).astype(F32)
    search = n_adm > kf
    positive = cpos >= kf
    negative = cnn < kf
    lo = jnp.where(positive, TINY, jnp.where(negative, rmin, 0.0))
    clo = jnp.where(positive, cpos, jnp.where(negative, n_adm, cnn))
    st_scr[0] = jnp.where(search, lo, rmin)
    st_scr[1] = jnp.where(negative, -TINY, rmax + (jnp.abs(rmax) * (2.0 ** -10) + TINY))
    st_scr[2] = jnp.where(search, clo, n_adm)
    zero_thr = jnp.logical_and(jnp.logical_not(positive), jnp.logical_not(negative))
    st_scr[3] = jnp.where(jnp.logical_or(jnp.logical_not(search), jnp.logical_or(zero_thr, clo == kf)), 1.0, 0.0)

    def bis_cond(carry):
        it, pending = carry
        return jnp.logical_and(pending > 0.0, it < BISECT_CAP)

    def bis_step(it):
        lo, hi, clo, done = st_scr[0], st_scr[1], st_scr[2], st_scr[3]
        a = lax.bitcast_convert_type(jnp.abs(lo), jnp.int32)
        b = lax.bitcast_convert_type(jnp.abs(hi), jnp.int32)
        geo = lax.bitcast_convert_type(a + ((b - a) >> 1), F32)
        geo = jnp.where(hi > 0.0, geo, -geo)
        mid = jnp.where(it < BISECT_ARITH, lo + (hi - lo) * 0.5, geo)
        stuck = jnp.logical_or(mid <= lo, mid >= hi)
        cnt, = count([lambda t, j: t >= mid[None]])
        live = jnp.logical_and(done == 0.0, jnp.logical_not(stuck))
        up = jnp.logical_and(live, cnt >= kf)
        dn = jnp.logical_and(live, cnt < kf)
        lo = jnp.where(up, mid, lo)
        clo = jnp.where(up, cnt, clo)
        hi = jnp.where(dn, mid, hi)
        done = jnp.where(jnp.logical_or(stuck, clo == kf), 1.0, done)
        st_scr[0], st_scr[1], st_scr[2], st_scr[3] = lo, hi, clo, done
        return done

    def bis_body(carry):
        it, _ = carry
        bis_step(it)
        done = bis_step(it + 1)
        return it + 2, jnp.max(1.0 - done)

    lax.while_loop(bis_cond, bis_body, (jnp.int32(0), jnp.max(1.0 - st_scr[3])))
    thr = st_scr[0]
    clo = st_scr[2]

    @pl.when(jnp.max(clo) > kf)
    def _():
        cgt, = count([lambda t, j: t > thr[None]])
        need = kf - cgt
        tied_value = jnp.where(clo > kf, thr, jnp.nan)
        lower = (lax.broadcasted_iota(jnp.int32, (tk, tk), 1)
                 <= lax.broadcasted_iota(jnp.int32, (tk, tk), 0))
        prefix_matrix = jnp.where(lower, 1.0, 0.0).astype(BF16)

        def drop(j, seen):
            t3 = s_scr[j].reshape(grp, SUBLANES, tq)
            tied = t3 == tied_value[None]
            ones = jnp.where(tied, 1.0, 0.0).reshape(tk, tq).astype(BF16)
            prefix = _dot(prefix_matrix, ones)
            rank = prefix.reshape(grp, SUBLANES, tq) + seen[None]
            kill = jnp.logical_and(tied, rank > need[None])
            s_scr[j] = jnp.where(kill, -jnp.inf, t3).reshape(tk, tq)
            return seen + spread(prefix[tk - 1:tk, :])

        lax.fori_loop(0, i + 1, drop, jnp.zeros((SUBLANES, tq), F32))

    hrow = lax.broadcasted_iota(jnp.int32, (LANES, tq), 0)
    for h in range(DSA_HEADS):
        slab = qt_ref[0, (h // 2) * LANES:(h // 2 + 1) * LANES, :]
        mine = (hrow >= DSA_HEAD_DIM) if (h % 2) else (hrow < DSA_HEAD_DIM)
        qm_scr[h] = jnp.where(mine, slab, jnp.zeros_like(slab))
        m_scr[h] = jnp.full((SUBLANES, tq), NEG_BIG, F32)
        l_scr[h] = jnp.zeros((SUBLANES, tq), F32)
        acc_scr[h] = jnp.zeros((LANES, tq), F32)

    def slab(h):
        return slice((h // 2) * LANES, (h // 2 + 1) * LANES)

    def tile_operands(j):
        t3 = s_scr[j].reshape(grp, SUBLANES, tq)
        bias = jnp.where(t3 >= thr[None], 0.0, NEG_BIG).reshape(tk, tq)
        return k_ref[0, pl.ds(pl.multiple_of(j * tk, tk), tk), :], bias

    def masked_scores(keys, bias, h):
        return _dot(keys[:, slab(h)], qm_scr[h]) + bias

    keys0, bias0 = tile_operands(0)
    for h in range(DSA_HEADS):
        sn_scr[h] = masked_scores(keys0, bias0, h)

    def att_body(j, _):
        keys_n, bias_n = tile_operands(jnp.minimum(j + 1, i))
        for h in range(DSA_HEADS):
            s3 = sn_scr[h].reshape(grp, SUBLANES, tq)
            m_prev = m_scr[h]
            m_new = jnp.maximum(m_prev, spread(jnp.max(tree(s3, jnp.max), axis=0, keepdims=True)))
            p3 = jnp.exp2(s3 - m_new[None])
            corr = jnp.exp2(m_prev - m_new)
            l_scr[h] = corr * l_scr[h] + tree(p3, jnp.sum)
            pv = _dot(vt_ref[0, j, slab(h), :], p3.reshape(tk, tq).astype(BF16))
            acc = acc_scr[h].reshape(LANES // SUBLANES, SUBLANES, tq) * corr[None]
            acc_scr[h] = acc.reshape(LANES, tq) + pv
            m_scr[h] = m_new
            sn_scr[h] = masked_scores(keys_n, bias_n, h)
        return 0

    lax.fori_loop(0, i + 1, att_body, 0)

    def normalised(h):
        inv = 1.0 / spread(jnp.sum(l_scr[h], axis=0, keepdims=True))
        return (acc_scr[h].reshape(LANES // SUBLANES, SUBLANES, tq) * inv[None]).reshape(LANES, tq)

    for sp in range(DSA_HEADS // 2):
        pair = jnp.where(hrow < DSA_HEAD_DIM, normalised(2 * sp), normalised(2 * sp + 1))
        o_ref[0, sp * LANES:(sp + 1) * LANES, :] = pair.astype(o_ref.dtype)


def _dsa(iqt, ik, iwt, qt, k, vt):
    B, T, _ = k.shape
    tq = DSA_TILE
    nq = T // tq
    topk = min(DSA_TOPK_MAX, T // 4)
    kern = functools.partial(_dsa_kernel, topk=topk)
    return pl.pallas_call(
        kern,
        grid=(B, nq),
        in_specs=[
            pl.BlockSpec((1, 1, IDX_DIM, IDX_HEADS * tq), lambda b, i: (b, i, 0, 0)),
            pl.BlockSpec((1, T, IDX_DIM), lambda b, i: (b, 0, 0)),
            pl.BlockSpec((1, IDX_HEADS, tq), lambda b, i: (b, 0, i)),
            pl.BlockSpec((1, DSA_WIDTH, tq), lambda b, i: (b, 0, i)),
            pl.BlockSpec((1, T, DSA_WIDTH), lambda b, i: (b, 0, 0)),
            pl.BlockSpec((1, nq, DSA_WIDTH, tq), lambda b, i: (b, 0, 0, 0)),
        ],
        out_specs=pl.BlockSpec((1, DSA_WIDTH, tq), lambda b, i: (b, 0, i)),
        out_shape=jax.ShapeDtypeStruct((B, DSA_WIDTH, T), BF16),
        scratch_shapes=[
            pltpu.VMEM((nq, tq, tq), F32),
            pltpu.VMEM((DSA_HEADS, LANES, tq), BF16),
            pltpu.VMEM((4, SUBLANES, tq), F32),
            pltpu.VMEM((DSA_HEADS, SUBLANES, tq), F32),
            pltpu.VMEM((DSA_HEADS, SUBLANES, tq), F32),
            pltpu.VMEM((DSA_HEADS, LANES, tq), F32),
            pltpu.VMEM((DSA_HEADS, tq, tq), F32),
            pltpu.VMEM((tq, IDX_HEADS * tq), F32),
        ],
        compiler_params=_params(("parallel", "arbitrary")),
    )(iqt, ik, iwt, qt, k, vt)


def _gla_kernel(q_ref, k_ref, v_ref, r_ref, la_ref, g_ref, o_ref, s_scr):
    rows = GLA_BLOCK
    nch = rows // CHUNK
    npair = GLA_HEADS // 2

    @pl.when(pl.program_id(1) == 0)
    def _():
        s_scr[...] = jnp.zeros_like(s_scr)

    ri = lax.broadcasted_iota(jnp.int32, (rows, rows), 0)
    ci = lax.broadcasted_iota(jnp.int32, (rows, rows), 1)
    causal = jnp.logical_and((ri >> 6) == (ci >> 6), ci <= ri)
    tri = jnp.where(causal, 1.0, 0.0).astype(BF16)

    la = la_ref[0]
    la_hi = la.astype(BF16)
    rem = la - la_hi.astype(F32)
    la_mid = rem.astype(BF16)
    la_lo = (rem - la_mid.astype(F32)).astype(BF16)
    b = _dot(tri, la_hi) + _dot(tri, la_mid) + _dot(tri, la_lo)
    b_last = jnp.concatenate(
        [jnp.broadcast_to(b[(c + 1) * CHUNK - 1:(c + 1) * CHUNK, :], (CHUNK, b.shape[1])) for c in range(nch)], axis=0)

    q = q_ref[0]
    k = k_ref[0]
    qg = q * jnp.exp(b)
    kg = (k * jnp.exp(-b)).astype(BF16)
    kd = (k * jnp.exp(b_last - b)).astype(BF16)
    decay_rows = jnp.exp(b_last)

    lane = lax.broadcasted_iota(jnp.int32, (rows, LANES), 1)
    eye = lax.broadcasted_iota(jnp.int32, (LANES, LANES), 0) == lax.broadcasted_iota(jnp.int32, (LANES, LANES), 1)
    top_rows = lax.broadcasted_iota(jnp.int32, (LANES, GLA_DV), 0) < GLA_DK

    for pr in range(npair):
        sl = slice(pr * LANES, (pr + 1) * LANES)
        qg_p = qg[:, sl]
        kg_p = kg[:, sl]
        kd_p = kd[:, sl]
        heads = (2 * pr, 2 * pr + 1)
        qg_h = [jnp.where(lane < GLA_DK, qg_p, 0.0).astype(BF16), jnp.where(lane >= GLA_DK, qg_p, 0.0).astype(BF16)]
        v_h = [v_ref[0, :, h * GLA_DV:(h + 1) * GLA_DV].astype(BF16) for h in heads]
        o_h = []
        for t in range(2):
            a = jnp.where(causal, _dot_nt(qg_h[t], kg_p), 0.0)
            o_h.append(_dot(a.astype(BF16), v_h[t]))
        state = s_scr[pr]
        inter = [[], []]
        for c in range(nch):
            rs = slice(c * CHUNK, (c + 1) * CHUNK)
            sb = state.astype(BF16)
            for t in range(2):
                inter[t].append(_dot(qg_h[t][rs], sb))
            kv = jnp.where(top_rows, _dot_tn(kd_p[rs], v_h[0][rs]), _dot_tn(kd_p[rs], v_h[1][rs]))
            drow = jnp.broadcast_to(decay_rows[c * CHUNK:c * CHUNK + 1, sl], (LANES, LANES))
            dcol = jnp.sum(jnp.where(eye, drow, 0.0), axis=1, keepdims=True)
            state = dcol * state + kv
        s_scr[pr] = state
        for t in range(2):
            h = heads[t]
            o = o_h[t] + jnp.concatenate(inter[t], axis=0)
            ms = jnp.mean(o * o, axis=1, keepdims=True)
            on = o * lax.rsqrt(ms + LN_EPS) * g_ref[...]
            r = r_ref[0, :, h * GLA_DV:(h + 1) * GLA_DV]
            o_ref[0, :, h * GLA_DV:(h + 1) * GLA_DV] = (on * (r / (1.0 + jnp.exp(-r)))).astype(o_ref.dtype)


def _gla(bq, bk, bv, br, la, g):
    B, T, _ = bq.shape
    rows = GLA_BLOCK
    blk = lambda w: pl.BlockSpec((1, rows, w), lambda b, i: (b, i, 0))
    return pl.pallas_call(
        _gla_kernel,
        grid=(B, T // rows),
        in_specs=[blk(256), blk(256), blk(512), blk(512), blk(256), pl.BlockSpec((1, GLA_DV), lambda b, i: (0, 0))],
        out_specs=blk(GLA_WIDTH),
        out_shape=jax.ShapeDtypeStruct((B, T, GLA_WIDTH), BF16),
        scratch_shapes=[pltpu.VMEM((GLA_HEADS // 2, 2 * GLA_DK, GLA_DV), F32)],
        compiler_params=_params(("parallel", "arbitrary")),
    )(bq, bk, bv, br, la, g)


def _layer_norm(x, g, b):
    mu = jnp.mean(x, axis=1, keepdims=True)
    xc = x - mu
    var = jnp.mean(xc * xc, axis=1, keepdims=True)
    return xc * lax.rsqrt(var + LN_EPS) * g + b


def _out_proj_kernel(x_ref, ya_ref, yb_ref, wo_ref, g_ref, b_ref, wr_hi_ref, wr_lo_ref, br_ref, h_ref, route_ref):
    tm = x_ref.shape[0]
    mix = _dot(ya_ref[...], wo_ref[0:DSA_WIDTH, :]) + _dot(yb_ref[...], wo_ref[DSA_WIDTH:D_MODEL, :])
    h = _layer_norm(DEEPNORM_ALPHA * x_ref[...] + mix, g_ref[...], b_ref[...])
    h_ref[...] = h

    h_hi = h.astype(BF16)
    h_lo = (h - h_hi.astype(F32)).astype(BF16)
    logits = _dot(h_hi, wr_hi_ref[...]) + _dot(h_lo, wr_hi_ref[...]) + _dot(h_hi, wr_lo_ref[...]) + br_ref[...]

    lane = lax.broadcasted_iota(jnp.int32, (tm, LANES), 1)
    lanef = lane.astype(F32)
    gl = jnp.where(lane < N_GROUPS, logits, -jnp.inf)
    gmax = jnp.max(gl, axis=1, keepdims=True)
    gsel = jnp.min(jnp.where(gl == gmax, lanef, 1e9), axis=1, keepdims=True)
    pg = 1.0 / jnp.sum(jnp.exp(gl - gmax), axis=1, keepdims=True)
    egrp = ((lane - N_GROUPS) >> 3).astype(F32)
    in_grp = jnp.logical_and(jnp.logical_and(lane >= N_GROUPS, lane < N_GROUPS + N_EXPERTS), egrp == gsel)
    el = jnp.where(in_grp, logits, -jnp.inf)
    t1 = jnp.max(el, axis=1, keepdims=True)
    i1 = jnp.min(jnp.where(el == t1, lanef, 1e9), axis=1, keepdims=True)
    el2 = jnp.where(lanef == i1, -jnp.inf, el)
    t2 = jnp.max(el2, axis=1, keepdims=True)
    i2 = jnp.min(jnp.where(el2 == t2, lanef, 1e9), axis=1, keepdims=True)
    e21 = jnp.exp(t2 - t1)
    g1 = pg / (1.0 + e21)
    g2 = pg * e21 / (1.0 + e21)
    route = jnp.where(lane == 0, i1 - N_GROUPS, 0.0)
    route = jnp.where(lane == 1, i2 - N_GROUPS, route)
    route = jnp.where(lane == 2, g1, route)
    route = jnp.where(lane == 3, g2, route)
    route_ref[...] = route


def _out_proj(x2, ya, yb, wo, g1, b1, wr_hi, wr_lo, br):
    N = x2.shape[0]
    tm = TOK_TILE
    row = lambda i: (i, 0)
    const = lambda i: (0, 0)
    return pl.pallas_call(
        _out_proj_kernel,
        grid=(N // tm,),
        in_specs=[
            pl.BlockSpec((tm, D_MODEL), row), pl.BlockSpec((tm, DSA_WIDTH), row), pl.BlockSpec((tm, GLA_WIDTH), row),
            pl.BlockSpec(wo.shape, const), pl.BlockSpec(g1.shape, const), pl.BlockSpec(b1.shape, const),
            pl.BlockSpec(wr_hi.shape, const), pl.BlockSpec(wr_lo.shape, const), pl.BlockSpec(br.shape, const),
        ],
        out_specs=[pl.BlockSpec((tm, D_MODEL), row), pl.BlockSpec((tm, LANES), row)],
        out_shape=[jax.ShapeDtypeStruct((N, D_MODEL), F32), jax.ShapeDtypeStruct((N, LANES), F32)],
        compiler_params=_params(("parallel",)),
    )(x2, ya, yb, wo, g1, b1, wr_hi, wr_lo, br)


def _rank_kernel(route_ref, rank_ref, cnt_ref, carry_scr):
    tm = route_ref.shape[0]

    @pl.when(pl.program_id(0) == 0)
    def _():
        carry_scr[...] = jnp.zeros_like(carry_scr)

    route = route_ref[...]
    lanef = lax.broadcasted_iota(jnp.int32, (tm, LANES), 1).astype(F32)
    e1 = route[:, 0:1]
    e2 = route[:, 1:2]
    hit1 = lanef == e1
    hit2 = lanef == e2
    onehot = jnp.where(jnp.logical_or(hit1, hit2), 1.0, 0.0).astype(BF16)
    ri = lax.broadcasted_iota(jnp.int32, (tm, tm), 0)
    ci = lax.broadcasted_iota(jnp.int32, (tm, tm), 1)
    before = jnp.where(ci < ri, 1.0, 0.0).astype(BF16)
    prefix = _dot(before, onehot) + carry_scr[0:1, :]
    r1 = jnp.sum(jnp.where(hit1, prefix, 0.0), axis=1, keepdims=True)
    r2 = jnp.sum(jnp.where(hit2, prefix, 0.0), axis=1, keepdims=True)
    rank_ref[...] = jnp.where(lanef == 0.0, r1, jnp.where(lanef == 1.0, r2, 0.0))
    total = _dot(jnp.ones((8, tm), BF16), onehot)
    carry_scr[...] = carry_scr[...] + total
    cnt_ref[...] = carry_scr[...]


def _rank(route):
    N = route.shape[0]
    tm = RANK_TILE
    return pl.pallas_call(
        _rank_kernel,
        grid=(N // tm,),
        in_specs=[pl.BlockSpec((tm, LANES), lambda i: (i, 0))],
        out_specs=[pl.BlockSpec((tm, LANES), lambda i: (i, 0)), pl.BlockSpec((8, LANES), lambda i: (0, 0))],
        out_shape=[jax.ShapeDtypeStruct((N, LANES), F32), jax.ShapeDtypeStruct((8, LANES), F32)],
        scratch_shapes=[pltpu.VMEM((8, LANES), F32)],
        compiler_params=_params(("arbitrary",)),
    )(route)


def _dispatch_kernel(pos_ref, h_ref, xs_in_ref, xs_ref, hbuf, load_sem, row_sem):
    del xs_in_ref
    tm = DISPATCH_TILE
    nbuf = hbuf.shape[0]
    i = pl.program_id(0)
    last = pl.num_programs(0) - 1

    def tile_load(step):
        return pltpu.make_async_copy(h_ref.at[pl.ds(step * tm, tm), :], hbuf.at[step % nbuf], load_sem.at[step % nbuf])

    def row_copy(step, r, slot):
        p = pos_ref[2 * (step * tm + r) + slot]
        return pltpu.make_async_copy(hbuf.at[step % nbuf, pl.ds(r, 1), :], xs_ref.at[pl.ds(p, 1), :],
                                     row_sem.at[step % nbuf])

    def for_rows(fn):
        def body(r, _):
            fn(r, 0)
            fn(r, 1)
            return 0
        lax.fori_loop(0, tm, body, 0, unroll=DMA_UNROLL)

    @pl.when(i == 0)
    def _():
        tile_load(0).start()

    @pl.when(i < last)
    def _():
        tile_load(i + 1).start()

    tile_load(i).wait()
    for_rows(lambda r, slot: row_copy(i, r, slot).start())

    @pl.when(i > 0)
    def _():
        for_rows(lambda r, slot: row_copy(i - 1, r, slot).wait())

    @pl.when(i == last)
    def _():
        for_rows(lambda r, slot: row_copy(i, r, slot).wait())


def _dispatch(pos_flat, h, xs_init):
    N = h.shape[0]
    grid_spec = pltpu.PrefetchScalarGridSpec(
        num_scalar_prefetch=1,
        grid=(N // DISPATCH_TILE,),
        in_specs=[pl.BlockSpec(memory_space=pl.ANY), pl.BlockSpec(memory_space=pl.ANY)],
        out_specs=pl.BlockSpec(memory_space=pl.ANY),
        scratch_shapes=[pltpu.VMEM((3, DISPATCH_TILE, D_MODEL), F32),
                        pltpu.SemaphoreType.DMA((3,)), pltpu.SemaphoreType.DMA((3,))],
    )
    return pl.pallas_call(
        _dispatch_kernel,
        grid_spec=grid_spec,
        out_shape=jax.ShapeDtypeStruct(xs_init.shape, xs_init.dtype),
        input_output_aliases={2: 0},
        compiler_params=_params(("arbitrary",)),
    )(pos_flat, h, xs_init)


def _ffn_kernel(te_ref, na_ref, x_ref, wi_ref, wo_ref, y_ref, wi_bf, wo_bf):
    i = pl.program_id(0)
    active = i < na_ref[0]
    fresh = jnp.logical_or(i == 0, te_ref[i] != te_ref[jnp.maximum(i - 1, 0)])

    @pl.when(jnp.logical_and(active, fresh))
    def _():
        wi_bf[...] = wi_ref[0].astype(BF16)
        wo_bf[...] = wo_ref[0].astype(BF16)

    @pl.when(active)
    def _():
        hid = _dot(x_ref[...].astype(BF16), wi_bf[...])
        hg = hid[:, :D_EXPERT]
        hu = hid[:, D_EXPERT:]
        act = (hg / (1.0 + jnp.exp(-hg))) * hu
        y_ref[...] = _dot(act.astype(BF16), wo_bf[...])

    @pl.when(jnp.logical_not(active))
    def _():
        y_ref[...] = jnp.zeros_like(y_ref)


def _ffn(tile_expert, n_active, xs, w_e_in, w_e_out):
    R = xs.shape[0]
    tm = MOE_TILE

    def live(i, te, na):
        return jnp.minimum(i, na[0] - 1)

    grid_spec = pltpu.PrefetchScalarGridSpec(
        num_scalar_prefetch=2,
        grid=(R // tm,),
        in_specs=[
            pl.BlockSpec((tm, D_MODEL), lambda i, te, na: (live(i, te, na), 0)),
            pl.BlockSpec((1, D_MODEL, 2 * D_EXPERT), lambda i, te, na: (te[live(i, te, na)], 0, 0)),
            pl.BlockSpec((1, D_EXPERT, D_MODEL), lambda i, te, na: (te[live(i, te, na)], 0, 0)),
        ],
        out_specs=pl.BlockSpec((tm, D_MODEL), lambda i, te, na: (i, 0)),
        scratch_shapes=[pltpu.VMEM((D_MODEL, 2 * D_EXPERT), BF16), pltpu.VMEM((D_EXPERT, D_MODEL), BF16)],
    )
    return pl.pallas_call(
        _ffn_kernel,
        grid_spec=grid_spec,
        out_shape=jax.ShapeDtypeStruct((R, D_MODEL), F32),
        compiler_params=_params(("arbitrary",)),
    )(tile_expert, n_active, xs, w_e_in, w_e_out)


def _combine_kernel(pos_ref, h_ref, route_ref, g_ref, b_ref, ys_ref, o_ref, buf, sem):
    tm = h_ref.shape[0]
    i = pl.program_id(0)
    cur = i % 2

    def row_copy(step, r, slot):
        p = pos_ref[2 * (step * tm + r) + slot]
        half = step % 2
        return pltpu.make_async_copy(ys_ref.at[pl.ds(p, 1), :], buf.at[half, slot, pl.ds(r, 1), :], sem.at[half])

    def for_rows(fn):
        def body(r, _):
            fn(r, 0)
            fn(r, 1)
            return 0
        lax.fori_loop(0, tm, body, 0, unroll=DMA_UNROLL)

    @pl.when(i == 0)
    def _():
        for_rows(lambda r, slot: row_copy(0, r, slot).start())

    @pl.when(i + 1 < pl.num_programs(0))
    def _():
        for_rows(lambda r, slot: row_copy(i + 1, r, slot).start())

    for_rows(lambda r, slot: row_copy(i, r, slot).wait())
    route = route_ref[...]
    ffn = buf[cur, 0] * route[:, 2:3] + buf[cur, 1] * route[:, 3:4]
    o_ref[...] = _layer_norm(DEEPNORM_ALPHA * h_ref[...] + ffn, g_ref[...], b_ref[...])


def _combine(pos_flat, h, route, g2, b2, ys):
    N = h.shape[0]
    tm = TOK_TILE
    row = lambda i, pos: (i, 0)
    const = lambda i, pos: (0, 0)
    grid_spec = pltpu.PrefetchScalarGridSpec(
        num_scalar_prefetch=1,
        grid=(N // tm,),
        in_specs=[
            pl.BlockSpec((tm, D_MODEL), row), pl.BlockSpec((tm, LANES), row),
            pl.BlockSpec(g2.shape, const), pl.BlockSpec(b2.shape, const),
            pl.BlockSpec(memory_space=pl.ANY),
        ],
        out_specs=pl.BlockSpec((tm, D_MODEL), row),
        scratch_shapes=[pltpu.VMEM((2, 2, tm, D_MODEL), F32), pltpu.SemaphoreType.DMA((2,))],
    )
    return pl.pallas_call(
        _combine_kernel,
        grid_spec=grid_spec,
        out_shape=jax.ShapeDtypeStruct((N, D_MODEL), F32),
        compiler_params=_params(("arbitrary",)),
    )(pos_flat, h, route, g2, b2, ys)


def _rope_tables(T, dim):
    half = dim // 2
    inv = 1.0 / (ROPE_THETA ** (jnp.arange(half, dtype=F32) / half))
    ang = jnp.arange(T).astype(F32)[:, None] * inv[None, :]
    cos = jnp.cos(ang)
    sin = jnp.sin(ang)
    reps = LANES // dim
    cos_t = jnp.tile(jnp.concatenate([cos, cos], axis=1), (1, reps))
    sin_t = jnp.tile(jnp.concatenate([-sin, sin], axis=1), (1, reps))
    return cos_t, sin_t


def _pad_cols(w, width):
    return jnp.pad(w, ((0, 0), (0, width - w.shape[1])))


def _layer(x, w_in, w_gla_gate, b_gla_gate, g_gla_norm, w_out, ln1_g, ln1_b,
           w_gr, b_gr, w_er, b_er, w_e_in, w_e_out, ln2_g, ln2_b):
    B, T, D = x.shape
    N = B * T
    assert D == D_MODEL and T % DSA_TILE == 0 and N % DISPATCH_TILE == 0 and DISPATCH_TILE % RANK_TILE == 0
    x2 = x.reshape(N, D)

    sizes = (512, 512, 512, 256, 32, 8, 256, 256, 512, 512, 16)
    offs = np.concatenate([[0], np.cumsum(sizes)])
    col = lambda k: w_in[:, offs[k]:offs[k + 1]]
    wa = jnp.concatenate([col(0) * (DSA_HEAD_DIM ** -0.5 * LOG2E), col(1), col(2)], axis=1).astype(BF16)
    wi = _pad_cols(jnp.concatenate([col(3), col(4), col(5) * IDX_SCALE], axis=1), 3 * LANES).astype(BF16)
    wb = jnp.concatenate([col(6) * (GLA_DK ** -0.5), col(7), col(8), col(9)], axis=1).astype(BF16)
    wg = _pad_cols(col(10), LANES).astype(BF16)
    wgate = jnp.pad(w_gla_gate, ((0, LANES - GLA_GATE_RANK), (0, 0))).astype(BF16)
    bgate = b_gla_gate.reshape(1, -1)
    cosa, sina = _rope_tables(T, DSA_HEAD_DIM)
    cosi, sini = _rope_tables(T, IDX_DIM)

    q, k, v, iq, ikw, bq, bk, bv, br, la = _in_proj(x2, wa, wi, wb, wg, wgate, bgate, cosa, sina, cosi, sini, T)

    nq = T // DSA_TILE
    iqt = iq.reshape(B, nq, DSA_TILE, IDX_HEADS, IDX_DIM).transpose(0, 1, 4, 3, 2)
    iqt = iqt.reshape(B, nq, IDX_DIM, IDX_HEADS * DSA_TILE)
    ik = ikw[:, :IDX_DIM].astype(BF16).reshape(B, T, IDX_DIM)
    iwt = ikw[:, IDX_DIM:IDX_DIM + IDX_HEADS].reshape(B, T, IDX_HEADS).transpose(0, 2, 1)
    qt = q.reshape(B, T, DSA_WIDTH).transpose(0, 2, 1)
    vt = v.reshape(B, nq, DSA_TILE, DSA_WIDTH).transpose(0, 1, 3, 2)
    ya = _dsa(iqt, ik, iwt, qt, k.reshape(B, T, DSA_WIDTH), vt).transpose(0, 2, 1)

    r3 = lambda a: a.reshape(B, T, a.shape[-1])
    yb = _gla(r3(bq), r3(bk), r3(bv), r3(br), r3(la), g_gla_norm.reshape(1, GLA_DV))

    wr = _pad_cols(jnp.concatenate([w_gr, w_er], axis=1), LANES)
    wr_hi = wr.astype(BF16)
    wr_lo = (wr - wr_hi.astype(F32)).astype(BF16)
    brt = _pad_cols(jnp.concatenate([b_gr, b_er]).reshape(1, -1), LANES)
    h, route = _out_proj(x2, ya.reshape(N, DSA_WIDTH), yb.reshape(N, GLA_WIDTH), w_out.astype(BF16),
                         ln1_g.reshape(1, D), ln1_b.reshape(1, D), wr_hi, wr_lo, brt)

    rank, cnt = _rank(route)
    counts = cnt[0, :N_EXPERTS].astype(jnp.int32)
    padded = ((counts + MOE_TILE - 1) // MOE_TILE) * MOE_TILE
    ends = jnp.cumsum(padded)
    starts = ends - padded
    eid = route[:, 0:2].astype(jnp.int32)
    pos = (starts[eid] + rank[:, 0:2].astype(jnp.int32)).reshape(-1)
    n_rows = 2 * N + N_EXPERTS * MOE_TILE
    n_tiles = n_rows // MOE_TILE
    tile_start = jnp.arange(n_tiles, dtype=jnp.int32) * MOE_TILE
    tile_expert = jnp.minimum(jnp.sum(tile_start[:, None] >= ends[None, :], axis=1), N_EXPERTS - 1).astype(jnp.int32)
    n_active = (ends[-1] // MOE_TILE).astype(jnp.int32).reshape(1)

    xs = _dispatch(pos, h, jnp.zeros((n_rows, D), F32))
    ys = _ffn(tile_expert, n_active, xs, w_e_in, w_e_out)
    out = _combine(pos, h, route, ln2_g.reshape(1, D), ln2_b.reshape(1, D), ys)
    return out.reshape(B, T, D)


def kernel(x, w_in, w_gla_gate, b_gla_gate, g_gla_norm, w_out, ln1_g, ln1_b, w_group_router, b_group_router,
           w_expert_router, b_expert_router, w_expert_in, w_expert_out, ln2_g, ln2_b):
    h = x
    for l in range(w_in.shape[0]):
        h = _layer(h, w_in[l], w_gla_gate[l], b_gla_gate[l], g_gla_norm[l], w_out[l], ln1_g[l], ln1_b[l],
                   w_group_router[l], b_group_router[l], w_expert_router[l], b_expert_router[l],
                   w_expert_in[l], w_expert_out[l], ln2_g[l], ln2_b[l])
    return h
```

```python
import functools

import numpy as np
import jax
import jax.numpy as jnp
from jax import lax
from jax.experimental import pallas as pl
from jax.experimental.pallas import tpu as pltpu

F32 = jnp.float32
BF16 = jnp.bfloat16

D_MODEL = 1024
CHUNK = 64
ROPE_THETA = 10000.0
LN_EPS = 1e-5
DSA_WIDTH = 512
DSA_HEAD_DIM = 64
DSA_HEADS = 8
IDX_HEADS = 8
IDX_DIM = 32
IDX_SCALE = (IDX_HEADS * IDX_DIM) ** -0.5
DSA_TOPK_MAX = 256
GLA_WIDTH = 512
GLA_HEADS = 4
GLA_DV = 128
GLA_DK = 64
GLA_GATE_RANK = 16
GLA_TAU = 16.0
N_GROUPS = 4
EXPERTS_PER_GROUP = 8
N_EXPERTS = 32
D_EXPERT = 512
DEEPNORM_ALPHA = 2.0 ** 0.25

LANES = 128
SUBLANES = 8
TINY = 2.0 ** -126
NEG_BIG = -1e30
ACC_ROWS = LANES + 16
VMEM_LIMIT = 56 * 1024 * 1024

TOK_TILE = 512
DSA_TILE = 256
GLA_BLOCK = 256
MOE_TILE = 512
RANK_TILE = 512
DISPATCH_TILE = 512
DMA_UNROLL = 8
LOG2E = 1.4426950408889634
BISECT_ARITH = 40
BISECT_CAP = 80


def _dot(a, b):
    return jnp.dot(a, b, preferred_element_type=F32)


def _dot_nt(a, b):
    return lax.dot_general(a, b, (((1,), (1,)), ((), ())), preferred_element_type=F32)


def _dot_tn(a, b):
    return lax.dot_general(a, b, (((0,), (0,)), ((), ())), preferred_element_type=F32)


def _params(sem):
    return pltpu.CompilerParams(dimension_semantics=sem, vmem_limit_bytes=VMEM_LIMIT)


def _rope_slab(slab, cos, sin, first_half, half):
    swapped = jnp.where(first_half, pltpu.roll(slab, LANES - half, 1), pltpu.roll(slab, half, 1))
    return slab * cos + swapped * sin


def _in_proj_kernel(x_ref, wa_ref, wi_ref, wb_ref, wg_ref, wgate_ref, bgate_ref,
                    cosa_ref, sina_ref, cosi_ref, sini_ref,
                    q_ref, k_ref, v_ref, iq_ref, ikw_ref, bq_ref, bk_ref, bv_ref, br_ref, la_ref):
    tm = x_ref.shape[0]
    xb = x_ref[...].astype(BF16)
    lane = lax.broadcasted_iota(jnp.int32, (tm, LANES), 1)

    a = _dot(xb, wa_ref[...])
    cosa, sina = cosa_ref[...], sina_ref[...]
    first_a = (lane & (DSA_HEAD_DIM - 1)) < DSA_HEAD_DIM // 2
    for c in range(DSA_WIDTH // LANES):
        sl = slice(c * LANES, (c + 1) * LANES)
        q_ref[:, sl] = _rope_slab(a[:, sl], cosa, sina, first_a, DSA_HEAD_DIM // 2).astype(BF16)
        ks = slice(DSA_WIDTH + c * LANES, DSA_WIDTH + (c + 1) * LANES)
        k_ref[:, sl] = _rope_slab(a[:, ks], cosa, sina, first_a, DSA_HEAD_DIM // 2).astype(BF16)
    v_ref[...] = a[:, 2 * DSA_WIDTH:3 * DSA_WIDTH].astype(BF16)

    ii = _dot(xb, wi_ref[...])
    cosi, sini = cosi_ref[...], sini_ref[...]
    first_i = (lane & (IDX_DIM - 1)) < IDX_DIM // 2
    for c in range(2):
        sl = slice(c * LANES, (c + 1) * LANES)
        iq_ref[:, sl] = _rope_slab(ii[:, sl], cosi, sini, first_i, IDX_DIM // 2).astype(BF16)
    last = ii[:, 2 * LANES:3 * LANES]
    ikw_ref[...] = jnp.where(lane < IDX_DIM, _rope_slab(last, cosi, sini, first_i, IDX_DIM // 2), last)

    b = _dot(xb, wb_ref[...])
    bq_ref[...] = b[:, 0:256]
    bk_ref[...] = b[:, 256:512]
    bv_ref[...] = b[:, 512:1024]
    br_ref[...] = b[:, 1024:1536]

    g = _dot(xb, wg_ref[...])
    z = _dot(g.astype(BF16), wgate_ref[...]) + bgate_ref[...]
    log_sig = jnp.minimum(z, 0.0) - jnp.log(1.0 + jnp.exp(-jnp.abs(z)))
    la_ref[...] = log_sig * (1.0 / GLA_TAU)


def _in_proj(x2, wa, wi, wb, wg, wgate, bgate, cosa, sina, cosi, sini, T):
    N = x2.shape[0]
    tm = TOK_TILE
    nt = T // tm
    row = lambda i: (i, 0)
    const = lambda i: (0, 0)
    pos = lambda i: (i % nt, 0)
    outs = [
        (DSA_WIDTH, BF16), (DSA_WIDTH, BF16), (DSA_WIDTH, BF16), (IDX_HEADS * IDX_DIM, BF16), (LANES, F32),
        (256, F32), (256, F32), (512, F32), (512, F32), (256, F32),
    ]
    return pl.pallas_call(
        _in_proj_kernel,
        grid=(N // tm,),
        in_specs=[
            pl.BlockSpec((tm, D_MODEL), row),
            pl.BlockSpec(wa.shape, const), pl.BlockSpec(wi.shape, const), pl.BlockSpec(wb.shape, const),
            pl.BlockSpec(wg.shape, const), pl.BlockSpec(wgate.shape, const), pl.BlockSpec(bgate.shape, const),
            pl.BlockSpec((tm, LANES), pos), pl.BlockSpec((tm, LANES), pos),
            pl.BlockSpec((tm, LANES), pos), pl.BlockSpec((tm, LANES), pos),
        ],
        out_specs=[pl.BlockSpec((tm, w), row) for w, _ in outs],
        out_shape=[jax.ShapeDtypeStruct((N, w), dt) for w, dt in outs],
        compiler_params=_params(("parallel",)),
    )(x2, wa, wi, wb, wg, wgate, bgate, cosa, sina, cosi, sini)


def _dsa_kernel(iqt_ref, ik_ref, iwt_ref, qt_ref, k_ref, vt_ref, o_ref,
                s_scr, qm_scr, st_scr, m_scr, acc_scr, sn_scr, rt_scr, *, topk):
    tq = DSA_TILE
    tk = DSA_TILE
    grp = tk // SUBLANES
    i = pl.program_id(1)
    kf = jnp.float32(topk)

    iqt = iqt_ref[0, 0]
    iwt = iwt_ref[0]
    krow = lax.broadcasted_iota(jnp.int32, (tk, tq), 0)
    qcol = lax.broadcasted_iota(jnp.int32, (tk, tq), 1)
    qcol8 = lax.broadcasted_iota(jnp.int32, (SUBLANES, tq), 1)

    def tree(x3, op):
        q4 = grp // 4
        return op(jnp.stack([op(x3[a * q4:(a + 1) * q4], axis=0) for a in range(4)]), axis=0)

    def fold(x, op):
        return tree(x.reshape(grp, SUBLANES, tq), op)

    def spread(x):
        return jnp.broadcast_to(x, (SUBLANES, tq))

    def head_products(j):
        keys = ik_ref[0, pl.ds(pl.multiple_of(j * tk, tk), tk), :]
        return _dot(keys, iqt)

    def weighted_relu_sum():
        acc = None
        for h in range(IDX_HEADS):
            term = jnp.maximum(rt_scr[:, h * tq:(h + 1) * tq], 0.0) * iwt[h:h + 1, :]
            acc = term if acc is None else acc + term
        return acc

    def tile_stats(s_hi, s_lo, carry):
        rmax, rmin, cpos, cnn = carry
        return (jnp.maximum(rmax, fold(s_hi, jnp.max)), jnp.minimum(rmin, fold(s_lo, jnp.min)),
                cpos + fold(jnp.where(s_hi > 0.0, 1.0, 0.0), jnp.sum),
                cnn + fold(jnp.where(s_hi >= 0.0, 1.0, 0.0), jnp.sum))

    def score_body(j, carry):
        s = weighted_relu_sum()
        s_scr[j] = s
        rt_scr[...] = head_products(j + 1)
        return tile_stats(s, s, carry)

    zeros8 = jnp.zeros((SUBLANES, tq), F32)
    init = (jnp.full((SUBLANES, tq), -jnp.inf, F32), jnp.full((SUBLANES, tq), jnp.inf, F32), zeros8, zeros8)
    rt_scr[...] = head_products(0)
    carry = lax.fori_loop(0, i, score_body, init)
    s = weighted_relu_sum()
    adm = (krow >> 6) <= (qcol >> 6)
    s_adm = jnp.where(adm, s, -jnp.inf)
    s_scr[i] = s_adm
    rmax, rmin, cpos, cnn = tile_stats(s_adm, jnp.where(adm, s, jnp.inf), carry)
    rmax = spread(jnp.max(rmax, axis=0, keepdims=True))
    rmin = spread(jnp.min(rmin, axis=0, keepdims=True))
    cpos = spread(jnp.sum(cpos, axis=0, keepdims=True))
    cnn = spread(jnp.sum(cnn, axis=0, keepdims=True))

    def count(preds):
        def body(j, accs):
            t3 = s_scr[j].reshape(grp, SUBLANES, tq)
            return tuple(a + tree(jnp.where(p(t3, j), 1.0, 0.0), jnp.sum) for a, p in zip(accs, preds))
        accs = lax.fori_loop(0, i + 1, body, tuple(jnp.zeros((SUBLANES, tq), F32) for _ in preds))
        return [spread(jnp.sum(a, axis=0, keepdims=True)) for a in accs]

    n_adm = (((qcol8 >> 6) + 1 + i * (tq // CHUNK)) * CHUNK).astype(F32)
    search = n_adm > kf
    positive = cpos >= kf
    negative = cnn < kf
    lo = jnp.where(positive, TINY, jnp.where(negative, rmin, 0.0))
    clo = jnp.where(positive, cpos, jnp.where(negative, n_adm, cnn))
    st_scr[0] = jnp.where(search, lo, rmin)
    st_scr[1] = jnp.where(negative, -TINY, rmax + (jnp.abs(rmax) * (2.0 ** -10) + TINY))
    st_scr[2] = jnp.where(search, clo, n_adm)
    zero_thr = jnp.logical_and(jnp.logical_not(positive), jnp.logical_not(negative))
    st_scr[3] = jnp.where(jnp.logical_or(jnp.logical_not(search), jnp.logical_or(zero_thr, clo == kf)), 1.0, 0.0)

    def bis_cond(carry):
        it, pending = carry
        return jnp.logical_and(pending > 0.0, it < BISECT_CAP)

    def bis_step(it):
        lo, hi, clo, done = st_scr[0], st_scr[1], st_scr[2], st_scr[3]
        a = lax.bitcast_convert_type(jnp.abs(lo), jnp.int32)
        b = lax.bitcast_convert_type(jnp.abs(hi), jnp.int32)
        geo = lax.bitcast_convert_type(a + ((b - a) >> 1), F32)
        geo = jnp.where(hi > 0.0, geo, -geo)
        mid = jnp.where(it < BISECT_ARITH, lo + (hi - lo) * 0.5, geo)
        stuck = jnp.logical_or(mid <= lo, mid >= hi)
        cnt, = count([lambda t, j: t >= mid[None]])
        live = jnp.logical_and(done == 0.0, jnp.logical_not(stuck))
        up = jnp.logical_and(live, cnt >= kf)
        dn = jnp.logical_and(live, cnt < kf)
        lo = jnp.where(up, mid, lo)
        clo = jnp.where(up, cnt, clo)
        hi = jnp.where(dn, mid, hi)
        done = jnp.where(jnp.logical_or(stuck, clo == kf), 1.0, done)
        st_scr[0], st_scr[1], st_scr[2], st_scr[3] = lo, hi, clo, done
        return done

    def bis_body(carry):
        it, _ = carry
        bis_step(it)
        done = bis_step(it + 1)
        return it + 2, jnp.max(1.0 - done)

    lax.while_loop(bis_cond, bis_body, (jnp.int32(0), jnp.max(1.0 - st_scr[3])))
    thr = st_scr[0]
    clo = st_scr[2]

    @pl.when(jnp.max(clo) > kf)
    def _():
        cgt, = count([lambda t, j: t > thr[None]])
        need = kf - cgt
        tied_value = jnp.where(clo > kf, thr, jnp.nan)
        lower = (lax.broadcasted_iota(jnp.int32, (tk, tk), 1)
                 <= lax.broadcasted_iota(jnp.int32, (tk, tk), 0))
        prefix_matrix = jnp.where(lower, 1.0, 0.0).astype(BF16)

        s_scr[i + 1] = jnp.full((tk, tq), -jnp.inf, F32)

        def drop_pair(pair, seen):
            for u in range(2):
                j = 2 * pair + u
                t3 = s_scr[j].reshape(grp, SUBLANES, tq)
                tied = t3 == tied_value[None]
                ones = jnp.where(tied, 1.0, 0.0).reshape(tk, tq).astype(BF16)
                prefix = _dot(prefix_matrix, ones)
                rank = prefix.reshape(grp, SUBLANES, tq) + seen[None]
                kill = jnp.logical_and(tied, rank > need[None])
                s_scr[j] = jnp.where(kill, -jnp.inf, t3).reshape(tk, tq)
                seen = seen + spread(prefix[tk - 1:tk, :])
            return seen

        lax.fori_loop(0, (i + 2) >> 1, drop_pair, jnp.zeros((SUBLANES, tq), F32))

    hrow = lax.broadcasted_iota(jnp.int32, (LANES, tq), 0)
    for h in range(DSA_HEADS):
        slab = qt_ref[0, (h // 2) * LANES:(h // 2 + 1) * LANES, :]
        mine = (hrow >= DSA_HEAD_DIM) if (h % 2) else (hrow < DSA_HEAD_DIM)
        qm_scr[h] = jnp.where(mine, slab, jnp.zeros_like(slab))
        m_scr[h] = jnp.full((SUBLANES, tq), NEG_BIG, F32)
        acc_scr[h] = jnp.zeros((ACC_ROWS, tq), F32)
    ones_rows = jnp.ones((ACC_ROWS - LANES, tk), BF16)

    def slab(h):
        return slice((h // 2) * LANES, (h // 2 + 1) * LANES)

    def tile_operands(j):
        t3 = s_scr[j].reshape(grp, SUBLANES, tq)
        bias = jnp.where(t3 >= thr[None], 0.0, NEG_BIG).reshape(tk, tq)
        return k_ref[0, pl.ds(pl.multiple_of(j * tk, tk), tk), :], bias

    def masked_scores(keys, bias, h):
        return _dot(keys[:, slab(h)], qm_scr[h]) + bias

    keys0, bias0 = tile_operands(0)
    for h in range(DSA_HEADS):
        sn_scr[h] = masked_scores(keys0, bias0, h)

    def att_body(j, _):
        keys_n, bias_n = tile_operands(jnp.minimum(j + 1, i))
        for h in range(DSA_HEADS):
            s3 = sn_scr[h].reshape(grp, SUBLANES, tq)
            m_prev = m_scr[h]
            m_new = jnp.maximum(m_prev, spread(jnp.max(tree(s3, jnp.max), axis=0, keepdims=True)))
            p = jnp.exp2((s3 - m_new[None]).reshape(tk, tq).astype(BF16))
            corr = jnp.exp2(m_prev - m_new)
            pv = _dot(jnp.concatenate([vt_ref[0, j, slab(h), :], ones_rows], axis=0), p)
            acc = acc_scr[h].reshape(ACC_ROWS // SUBLANES, SUBLANES, tq) * corr[None]
            acc_scr[h] = acc.reshape(ACC_ROWS, tq) + pv
            m_scr[h] = m_new
            sn_scr[h] = masked_scores(keys_n, bias_n, h)
        return 0

    lax.fori_loop(0, i + 1, att_body, 0)

    def normalised(h):
        inv = 1.0 / acc_scr[h, LANES:LANES + SUBLANES, :]
        return (acc_scr[h, 0:LANES, :].reshape(LANES // SUBLANES, SUBLANES, tq) * inv[None]).reshape(LANES, tq)

    for sp in range(DSA_HEADS // 2):
        pair = jnp.where(hrow < DSA_HEAD_DIM, normalised(2 * sp), normalised(2 * sp + 1))
        o_ref[0, sp * LANES:(sp + 1) * LANES, :] = pair.astype(o_ref.dtype)


def _dsa(iqt, ik, iwt, qt, k, vt):
    B, T, _ = k.shape
    tq = DSA_TILE
    nq = T // tq
    topk = min(DSA_TOPK_MAX, T // 4)
    kern = functools.partial(_dsa_kernel, topk=topk)
    return pl.pallas_call(
        kern,
        grid=(B, nq),
        in_specs=[
            pl.BlockSpec((1, 1, IDX_DIM, IDX_HEADS * tq), lambda b, i: (b, i, 0, 0)),
            pl.BlockSpec((1, T, IDX_DIM), lambda b, i: (b, 0, 0)),
            pl.BlockSpec((1, IDX_HEADS, tq), lambda b, i: (b, 0, i)),
            pl.BlockSpec((1, DSA_WIDTH, tq), lambda b, i: (b, 0, i)),
            pl.BlockSpec((1, T, DSA_WIDTH), lambda b, i: (b, 0, 0)),
            pl.BlockSpec((1, nq, DSA_WIDTH, tq), lambda b, i: (b, 0, 0, 0)),
        ],
        out_specs=pl.BlockSpec((1, DSA_WIDTH, tq), lambda b, i: (b, 0, i)),
        out_shape=jax.ShapeDtypeStruct((B, DSA_WIDTH, T), BF16),
        scratch_shapes=[
            pltpu.VMEM((nq + 1, tq, tq), F32),
            pltpu.VMEM((DSA_HEADS, LANES, tq), BF16),
            pltpu.VMEM((4, SUBLANES, tq), F32),
            pltpu.VMEM((DSA_HEADS, SUBLANES, tq), F32),
            pltpu.VMEM((DSA_HEADS, ACC_ROWS, tq), F32),
            pltpu.VMEM((DSA_HEADS, tq, tq), F32),
            pltpu.VMEM((tq, IDX_HEADS * tq), F32),
        ],
        compiler_params=_params(("parallel", "arbitrary")),
    )(iqt, ik, iwt, qt, k, vt)


def _gla_kernel(q_ref, k_ref, v_ref, r_ref, la_ref, g_ref, o_ref, s_scr):
    rows = GLA_BLOCK
    nch = rows // CHUNK
    npair = GLA_HEADS // 2

    @pl.when(pl.program_id(1) == 0)
    def _():
        s_scr[...] = jnp.zeros_like(s_scr)

    ri = lax.broadcasted_iota(jnp.int32, (rows, rows), 0)
    ci = lax.broadcasted_iota(jnp.int32, (rows, rows), 1)
    causal = jnp.logical_and((ri >> 6) == (ci >> 6), ci <= ri)
    tri = jnp.where(causal, 1.0, 0.0).astype(BF16)

    la = la_ref[0]
    la_hi = la.astype(BF16)
    rem = la - la_hi.astype(F32)
    la_mid = rem.astype(BF16)
    la_lo = (rem - la_mid.astype(F32)).astype(BF16)
    b = _dot(tri, la_hi) + _dot(tri, la_mid) + _dot(tri, la_lo)
    b_last = jnp.concatenate(
        [jnp.broadcast_to(b[(c + 1) * CHUNK - 1:(c + 1) * CHUNK, :], (CHUNK, b.shape[1])) for c in range(nch)], axis=0)

    q = q_ref[0]
    k = k_ref[0]
    qg = q * jnp.exp(b)
    kg = (k * jnp.exp(-b)).astype(BF16)
    kd = (k * jnp.exp(b_last - b)).astype(BF16)
    decay_rows = jnp.exp(b_last)

    lane = lax.broadcasted_iota(jnp.int32, (rows, LANES), 1)
    eye = lax.broadcasted_iota(jnp.int32, (LANES, LANES), 0) == lax.broadcasted_iota(jnp.int32, (LANES, LANES), 1)
    top_rows = lax.broadcasted_iota(jnp.int32, (LANES, GLA_DV), 0) < GLA_DK

    for pr in range(npair):
        sl = slice(pr * LANES, (pr + 1) * LANES)
        qg_p = qg[:, sl]
        kg_p = kg[:, sl]
        kd_p = kd[:, sl]
        heads = (2 * pr, 2 * pr + 1)
        qg_h = [jnp.where(lane < GLA_DK, qg_p, 0.0).astype(BF16), jnp.where(lane >= GLA_DK, qg_p, 0.0).astype(BF16)]
        v_h = [v_ref[0, :, h * GLA_DV:(h + 1) * GLA_DV].astype(BF16) for h in heads]
        o_h = []
        for t in range(2):
            a = jnp.where(causal, _dot_nt(qg_h[t], kg_p), 0.0)
            o_h.append(_dot(a.astype(BF16), v_h[t]))
        state = s_scr[pr]
        inter = [[], []]
        for c in range(nch):
            rs = slice(c * CHUNK, (c + 1) * CHUNK)
            sb = state.astype(BF16)
            for t in range(2):
                inter[t].append(_dot(qg_h[t][rs], sb))
            kv = jnp.where(top_rows, _dot_tn(kd_p[rs], v_h[0][rs]), _dot_tn(kd_p[rs], v_h[1][rs]))
            drow = jnp.broadcast_to(decay_rows[c * CHUNK:c * CHUNK + 1, sl], (LANES, LANES))
            dcol = jnp.sum(jnp.where(eye, drow, 0.0), axis=1, keepdims=True)
            state = dcol * state + kv
        s_scr[pr] = state
        for t in range(2):
            h = heads[t]
            o = o_h[t] + jnp.concatenate(inter[t], axis=0)
            ms = jnp.mean(o * o, axis=1, keepdims=True)
            on = o * lax.rsqrt(ms + LN_EPS) * g_ref[...]
            r = r_ref[0, :, h * GLA_DV:(h + 1) * GLA_DV]
            o_ref[0, :, h * GLA_DV:(h + 1) * GLA_DV] = (on * (r / (1.0 + jnp.exp(-r)))).astype(o_ref.dtype)


def _gla(bq, bk, bv, br, la, g):
    B, T, _ = bq.shape
    rows = GLA_BLOCK
    blk = lambda w: pl.BlockSpec((1, rows, w), lambda b, i: (b, i, 0))
    return pl.pallas_call(
        _gla_kernel,
        grid=(B, T // rows),
        in_specs=[blk(256), blk(256), blk(512), blk(512), blk(256), pl.BlockSpec((1, GLA_DV), lambda b, i: (0, 0))],
        out_specs=blk(GLA_WIDTH),
        out_shape=jax.ShapeDtypeStruct((B, T, GLA_WIDTH), BF16),
        scratch_shapes=[pltpu.VMEM((GLA_HEADS // 2, 2 * GLA_DK, GLA_DV), F32)],
        compiler_params=_params(("parallel", "arbitrary")),
    )(bq, bk, bv, br, la, g)


def _layer_norm(x, g, b):
    mu = jnp.mean(x, axis=1, keepdims=True)
    xc = x - mu
    var = jnp.mean(xc * xc, axis=1, keepdims=True)
    return xc * lax.rsqrt(var + LN_EPS) * g + b


def _out_proj_kernel(x_ref, ya_ref, yb_ref, wo_ref, g_ref, b_ref, wr_hi_ref, wr_lo_ref, br_ref, h_ref, route_ref):
    tm = x_ref.shape[0]
    mix = _dot(ya_ref[...], wo_ref[0:DSA_WIDTH, :]) + _dot(yb_ref[...], wo_ref[DSA_WIDTH:D_MODEL, :])
    h = _layer_norm(DEEPNORM_ALPHA * x_ref[...] + mix, g_ref[...], b_ref[...])
    h_ref[...] = h

    h_hi = h.astype(BF16)
    h_lo = (h - h_hi.astype(F32)).astype(BF16)
    logits = _dot(h_hi, wr_hi_ref[...]) + _dot(h_lo, wr_hi_ref[...]) + _dot(h_hi, wr_lo_ref[...]) + br_ref[...]

    lane = lax.broadcasted_iota(jnp.int32, (tm, LANES), 1)
    lanef = lane.astype(F32)
    gl = jnp.where(lane < N_GROUPS, logits, -jnp.inf)
    gmax = jnp.max(gl, axis=1, keepdims=True)
    gsel = jnp.min(jnp.where(gl == gmax, lanef, 1e9), axis=1, keepdims=True)
    pg = 1.0 / jnp.sum(jnp.exp(gl - gmax), axis=1, keepdims=True)
    egrp = ((lane - N_GROUPS) >> 3).astype(F32)
    in_grp = jnp.logical_and(jnp.logical_and(lane >= N_GROUPS, lane < N_GROUPS + N_EXPERTS), egrp == gsel)
    el = jnp.where(in_grp, logits, -jnp.inf)
    t1 = jnp.max(el, axis=1, keepdims=True)
    i1 = jnp.min(jnp.where(el == t1, lanef, 1e9), axis=1, keepdims=True)
    el2 = jnp.where(lanef == i1, -jnp.inf, el)
    t2 = jnp.max(el2, axis=1, keepdims=True)
    i2 = jnp.min(jnp.where(el2 == t2, lanef, 1e9), axis=1, keepdims=True)
    e21 = jnp.exp(t2 - t1)
    g1 = pg / (1.0 + e21)
    g2 = pg * e21 / (1.0 + e21)
    route = jnp.where(lane == 0, i1 - N_GROUPS, 0.0)
    route = jnp.where(lane == 1, i2 - N_GROUPS, route)
    route = jnp.where(lane == 2, g1, route)
    route = jnp.where(lane == 3, g2, route)
    route_ref[...] = route


def _out_proj(x2, ya, yb, wo, g1, b1, wr_hi, wr_lo, br):
    N = x2.shape[0]
    tm = TOK_TILE
    row = lambda i: (i, 0)
    const = lambda i: (0, 0)
    return pl.pallas_call(
        _out_proj_kernel,
        grid=(N // tm,),
        in_specs=[
            pl.BlockSpec((tm, D_MODEL), row), pl.BlockSpec((tm, DSA_WIDTH), row), pl.BlockSpec((tm, GLA_WIDTH), row),
            pl.BlockSpec(wo.shape, const), pl.BlockSpec(g1.shape, const), pl.BlockSpec(b1.shape, const),
            pl.BlockSpec(wr_hi.shape, const), pl.BlockSpec(wr_lo.shape, const), pl.BlockSpec(br.shape, const),
        ],
        out_specs=[pl.BlockSpec((tm, D_MODEL), row), pl.BlockSpec((tm, LANES), row)],
        out_shape=[jax.ShapeDtypeStruct((N, D_MODEL), F32), jax.ShapeDtypeStruct((N, LANES), F32)],
        compiler_params=_params(("parallel",)),
    )(x2, ya, yb, wo, g1, b1, wr_hi, wr_lo, br)


def _rank_kernel(route_ref, rank_ref, cnt_ref, carry_scr):
    tm = route_ref.shape[0]

    @pl.when(pl.program_id(0) == 0)
    def _():
        carry_scr[...] = jnp.zeros_like(carry_scr)

    route = route_ref[...]
    lanef = lax.broadcasted_iota(jnp.int32, (tm, LANES), 1).astype(F32)
    e1 = route[:, 0:1]
    e2 = route[:, 1:2]
    hit1 = lanef == e1
    hit2 = lanef == e2
    onehot = jnp.where(jnp.logical_or(hit1, hit2), 1.0, 0.0).astype(BF16)
    ri = lax.broadcasted_iota(jnp.int32, (tm, tm), 0)
    ci = lax.broadcasted_iota(jnp.int32, (tm, tm), 1)
    before = jnp.where(ci < ri, 1.0, 0.0).astype(BF16)
    prefix = _dot(before, onehot) + carry_scr[0:1, :]
    r1 = jnp.sum(jnp.where(hit1, prefix, 0.0), axis=1, keepdims=True)
    r2 = jnp.sum(jnp.where(hit2, prefix, 0.0), axis=1, keepdims=True)
    rank_ref[...] = jnp.where(lanef == 0.0, r1, jnp.where(lanef == 1.0, r2, 0.0))
    total = _dot(jnp.ones((8, tm), BF16), onehot)
    carry_scr[...] = carry_scr[...] + total
    cnt_ref[...] = carry_scr[...]


def _rank(route):
    N = route.shape[0]
    tm = RANK_TILE
    return pl.pallas_call(
        _rank_kernel,
        grid=(N // tm,),
        in_specs=[pl.BlockSpec((tm, LANES), lambda i: (i, 0))],
        out_specs=[pl.BlockSpec((tm, LANES), lambda i: (i, 0)), pl.BlockSpec((8, LANES), lambda i: (0, 0))],
        out_shape=[jax.ShapeDtypeStruct((N, LANES), F32), jax.ShapeDtypeStruct((8, LANES), F32)],
        scratch_shapes=[pltpu.VMEM((8, LANES), F32)],
        compiler_params=_params(("arbitrary",)),
    )(route)


def _dispatch_kernel(pos_ref, h_ref, xs_in_ref, xs_ref, hbuf, load_sem, row_sem):
    del xs_in_ref
    tm = DISPATCH_TILE
    nbuf = hbuf.shape[0]
    i = pl.program_id(0)
    last = pl.num_programs(0) - 1

    def tile_load(step):
        return pltpu.make_async_copy(h_ref.at[pl.ds(step * tm, tm), :], hbuf.at[step % nbuf], load_sem.at[step % nbuf])

    def row_copy(step, r, slot):
        p = pos_ref[2 * (step * tm + r) + slot]
        return pltpu.make_async_copy(hbuf.at[step % nbuf, pl.ds(r, 1), :], xs_ref.at[pl.ds(p, 1), :],
                                     row_sem.at[step % nbuf])

    def for_rows(fn):
        def body(r, _):
            fn(r, 0)
            fn(r, 1)
            return 0
        lax.fori_loop(0, tm, body, 0, unroll=DMA_UNROLL)

    @pl.when(i == 0)
    def _():
        tile_load(0).start()

    @pl.when(i < last)
    def _():
        tile_load(i + 1).start()

    tile_load(i).wait()
    for_rows(lambda r, slot: row_copy(i, r, slot).start())

    @pl.when(i > 0)
    def _():
        for_rows(lambda r, slot: row_copy(i - 1, r, slot).wait())

    @pl.when(i == last)
    def _():
        for_rows(lambda r, slot: row_copy(i, r, slot).wait())


def _dispatch(pos_flat, h, xs_init):
    N = h.shape[0]
    grid_spec = pltpu.PrefetchScalarGridSpec(
        num_scalar_prefetch=1,
        grid=(N // DISPATCH_TILE,),
        in_specs=[pl.BlockSpec(memory_space=pl.ANY), pl.BlockSpec(memory_space=pl.ANY)],
        out_specs=pl.BlockSpec(memory_space=pl.ANY),
        scratch_shapes=[pltpu.VMEM((3, DISPATCH_TILE, D_MODEL), F32),
                        pltpu.SemaphoreType.DMA((3,)), pltpu.SemaphoreType.DMA((3,))],
    )
    return pl.pallas_call(
        _dispatch_kernel,
        grid_spec=grid_spec,
        out_shape=jax.ShapeDtypeStruct(xs_init.shape, xs_init.dtype),
        input_output_aliases={2: 0},
        compiler_params=_params(("arbitrary",)),
    )(pos_flat, h, xs_init)


def _ffn_kernel(te_ref, na_ref, x_ref, wi_ref, wo_ref, y_ref, wi_bf, wo_bf):
    i = pl.program_id(0)
    active = i < na_ref[0]
    fresh = jnp.logical_or(i == 0, te_ref[i] != te_ref[jnp.maximum(i - 1, 0)])

    @pl.when(jnp.logical_and(active, fresh))
    def _():
        wi_bf[...] = wi_ref[0].astype(BF16)
        wo_bf[...] = wo_ref[0].astype(BF16)

    @pl.when(active)
    def _():
        hid = _dot(x_ref[...].astype(BF16), wi_bf[...])
        hg = hid[:, :D_EXPERT]
        hu = hid[:, D_EXPERT:]
        act = (hg / (1.0 + jnp.exp(-hg))) * hu
        y_ref[...] = _dot(act.astype(BF16), wo_bf[...])

    @pl.when(jnp.logical_not(active))
    def _():
        y_ref[...] = jnp.zeros_like(y_ref)


def _ffn(tile_expert, n_active, xs, w_e_in, w_e_out):
    R = xs.shape[0]
    tm = MOE_TILE

    def live(i, te, na):
        return jnp.minimum(i, na[0] - 1)

    grid_spec = pltpu.PrefetchScalarGridSpec(
        num_scalar_prefetch=2,
        grid=(R // tm,),
        in_specs=[
            pl.BlockSpec((tm, D_MODEL), lambda i, te, na: (live(i, te, na), 0)),
            pl.BlockSpec((1, D_MODEL, 2 * D_EXPERT), lambda i, te, na: (te[live(i, te, na)], 0, 0)),
            pl.BlockSpec((1, D_EXPERT, D_MODEL), lambda i, te, na: (te[live(i, te, na)], 0, 0)),
        ],
        out_specs=pl.BlockSpec((tm, D_MODEL), lambda i, te, na: (i, 0)),
        scratch_shapes=[pltpu.VMEM((D_MODEL, 2 * D_EXPERT), BF16), pltpu.VMEM((D_EXPERT, D_MODEL), BF16)],
    )
    return pl.pallas_call(
        _ffn_kernel,
        grid_spec=grid_spec,
        out_shape=jax.ShapeDtypeStruct((R, D_MODEL), F32),
        compiler_params=_params(("arbitrary",)),
    )(tile_expert, n_active, xs, w_e_in, w_e_out)


def _combine_kernel(pos_ref, h_ref, route_ref, g_ref, b_ref, ys_ref, o_ref, buf, sem):
    tm = h_ref.shape[0]
    i = pl.program_id(0)
    cur = i % 2

    def row_copy(step, r, slot):
        p = pos_ref[2 * (step * tm + r) + slot]
        half = step % 2
        return pltpu.make_async_copy(ys_ref.at[pl.ds(p, 1), :], buf.at[half, slot, pl.ds(r, 1), :], sem.at[half])

    def for_rows(fn):
        def body(r, _):
            fn(r, 0)
            fn(r, 1)
            return 0
        lax.fori_loop(0, tm, body, 0, unroll=DMA_UNROLL)

    @pl.when(i == 0)
    def _():
        for_rows(lambda r, slot: row_copy(0, r, slot).start())

    @pl.when(i + 1 < pl.num_programs(0))
    def _():
        for_rows(lambda r, slot: row_copy(i + 1, r, slot).start())

    for_rows(lambda r, slot: row_copy(i, r, slot).wait())
    route = route_ref[...]
    ffn = buf[cur, 0] * route[:, 2:3] + buf[cur, 1] * route[:, 3:4]
    o_ref[...] = _layer_norm(DEEPNORM_ALPHA * h_ref[...] + ffn, g_ref[...], b_ref[...])


def _combine(pos_flat, h, route, g2, b2, ys):
    N = h.shape[0]
    tm = TOK_TILE
    row = lambda i, pos: (i, 0)
    const = lambda i, pos: (0, 0)
    grid_spec = pltpu.PrefetchScalarGridSpec(
        num_scalar_prefetch=1,
        grid=(N // tm,),
        in_specs=[
            pl.BlockSpec((tm, D_MODEL), row), pl.BlockSpec((tm, LANES), row),
            pl.BlockSpec(g2.shape, const), pl.BlockSpec(b2.shape, const),
            pl.BlockSpec(memory_space=pl.ANY),
        ],
        out_specs=pl.BlockSpec((tm, D_MODEL), row),
        scratch_shapes=[pltpu.VMEM((2, 2, tm, D_MODEL), F32), pltpu.SemaphoreType.DMA((2,))],
    )
    return pl.pallas_call(
        _combine_kernel,
        grid_spec=grid_spec,
        out_shape=jax.ShapeDtypeStruct((N, D_MODEL), F32),
        compiler_params=_params(("arbitrary",)),
    )(pos_flat, h, route, g2, b2, ys)


def _rope_tables(T, dim):
    half = dim // 2
    inv = 1.0 / (ROPE_THETA ** (jnp.arange(half, dtype=F32) / half))
    ang = jnp.arange(T).astype(F32)[:, None] * inv[None, :]
    cos = jnp.cos(ang)
    sin = jnp.sin(ang)
    reps = LANES // dim
    cos_t = jnp.tile(jnp.concatenate([cos, cos], axis=1), (1, reps))
    sin_t = jnp.tile(jnp.concatenate([-sin, sin], axis=1), (1, reps))
    return cos_t, sin_t


def _pad_cols(w, width):
    return jnp.pad(w, ((0, 0), (0, width - w.shape[1])))


def _layer(x, w_in, w_gla_gate, b_gla_gate, g_gla_norm, w_out, ln1_g, ln1_b,
           w_gr, b_gr, w_er, b_er, w_e_in, w_e_out, ln2_g, ln2_b):
    B, T, D = x.shape
    N = B * T
    assert D == D_MODEL and T % DSA_TILE == 0 and N % DISPATCH_TILE == 0 and DISPATCH_TILE % RANK_TILE == 0
    x2 = x.reshape(N, D)

    sizes = (512, 512, 512, 256, 32, 8, 256, 256, 512, 512, 16)
    offs = np.concatenate([[0], np.cumsum(sizes)])
    col = lambda k: w_in[:, offs[k]:offs[k + 1]]
    wa = jnp.concatenate([col(0) * (DSA_HEAD_DIM ** -0.5 * LOG2E), col(1), col(2)], axis=1).astype(BF16)
    wi = _pad_cols(jnp.concatenate([col(3), col(4), col(5) * IDX_SCALE], axis=1), 3 * LANES).astype(BF16)
    wb = jnp.concatenate([col(6) * (GLA_DK ** -0.5), col(7), col(8), col(9)], axis=1).astype(BF16)
    wg = _pad_cols(col(10), LANES).astype(BF16)
    wgate = jnp.pad(w_gla_gate, ((0, LANES - GLA_GATE_RANK), (0, 0))).astype(BF16)
    bgate = b_gla_gate.reshape(1, -1)
    cosa, sina = _rope_tables(T, DSA_HEAD_DIM)
    cosi, sini = _rope_tables(T, IDX_DIM)

    q, k, v, iq, ikw, bq, bk, bv, br, la = _in_proj(x2, wa, wi, wb, wg, wgate, bgate, cosa, sina, cosi, sini, T)

    nq = T // DSA_TILE
    iqt = iq.reshape(B, nq, DSA_TILE, IDX_HEADS, IDX_DIM).transpose(0, 1, 4, 3, 2)
    iqt = iqt.reshape(B, nq, IDX_DIM, IDX_HEADS * DSA_TILE)
    ik = ikw[:, :IDX_DIM].astype(BF16).reshape(B, T, IDX_DIM)
    iwt = ikw[:, IDX_DIM:IDX_DIM + IDX_HEADS].reshape(B, T, IDX_HEADS).transpose(0, 2, 1)
    qt = q.reshape(B, T, DSA_WIDTH).transpose(0, 2, 1)
    vt = v.reshape(B, nq, DSA_TILE, DSA_WIDTH).transpose(0, 1, 3, 2)
    ya = _dsa(iqt, ik, iwt, qt, k.reshape(B, T, DSA_WIDTH), vt).transpose(0, 2, 1)

    r3 = lambda a: a.reshape(B, T, a.shape[-1])
    yb = _gla(r3(bq), r3(bk), r3(bv), r3(br), r3(la), g_gla_norm.reshape(1, GLA_DV))

    wr = _pad_cols(jnp.concatenate([w_gr, w_er], axis=1), LANES)
    wr_hi = wr.astype(BF16)
    wr_lo = (wr - wr_hi.astype(F32)).astype(BF16)
    brt = _pad_cols(jnp.concatenate([b_gr, b_er]).reshape(1, -1), LANES)
    h, route = _out_proj(x2, ya.reshape(N, DSA_WIDTH), yb.reshape(N, GLA_WIDTH), w_out.astype(BF16),
                         ln1_g.reshape(1, D), ln1_b.reshape(1, D), wr_hi, wr_lo, brt)

    rank, cnt = _rank(route)
    counts = cnt[0, :N_EXPERTS].astype(jnp.int32)
    padded = ((counts + MOE_TILE - 1) // MOE_TILE) * MOE_TILE
    ends = jnp.cumsum(padded)
    starts = ends - padded
    eid = route[:, 0:2].astype(jnp.int32)
    pos = (starts[eid] + rank[:, 0:2].astype(jnp.int32)).reshape(-1)
    n_rows = 2 * N + N_EXPERTS * MOE_TILE
    n_tiles = n_rows // MOE_TILE
    tile_start = jnp.arange(n_tiles, dtype=jnp.int32) * MOE_TILE
    tile_expert = jnp.minimum(jnp.sum(tile_start[:, None] >= ends[None, :], axis=1), N_EXPERTS - 1).astype(jnp.int32)
    n_active = (ends[-1] // MOE_TILE).astype(jnp.int32).reshape(1)

    xs = _dispatch(pos, h, jnp.zeros((n_rows, D), F32))
    ys = _ffn(tile_expert, n_active, xs, w_e_in, w_e_out)
    out = _combine(pos, h, route, ln2_g.reshape(1, D), ln2_b.reshape(1, D), ys)
    return out.reshape(B, T, D)


def kernel(x, w_in, w_gla_gate, b_gla_gate, g_gla_norm, w_out, ln1_g, ln1_b, w_group_router, b_group_router,
           w_expert_router, b_expert_router, w_expert_in, w_expert_out, ln2_g, ln2_b):
    h = x
    for l in range(w_in.shape[0]):
        h = _layer(h, w_in[l], w_gla_gate[l], b_gla_gate[l], g_gla_norm[l], w_out[l], ln1_g[l], ln1_b[l],
                   w_group_router[l], b_group_router[l], w_expert_router[l], b_expert_router[l],
                   w_expert_in[l], w_expert_out[l], ln2_g[l], ln2_b[l])
    return h
```

```python
import functools

import numpy as np
import jax
import jax.numpy as jnp
from jax import lax
from jax.experimental import pallas as pl
from jax.experimental.pallas import tpu as pltpu

F32 = jnp.float32
BF16 = jnp.bfloat16

D_MODEL = 1024
CHUNK = 64
ROPE_THETA = 10000.0
LN_EPS = 1e-5
DSA_WIDTH = 512
DSA_HEAD_DIM = 64
DSA_HEADS = 8
IDX_HEADS = 8
IDX_DIM = 32
IDX_SCALE = (IDX_HEADS * IDX_DIM) ** -0.5
DSA_TOPK_MAX = 256
GLA_WIDTH = 512
GLA_HEADS = 4
GLA_DV = 128
GLA_DK = 64
GLA_GATE_RANK = 16
GLA_TAU = 16.0
N_GROUPS = 4
EXPERTS_PER_GROUP = 8
N_EXPERTS = 32
D_EXPERT = 512
DEEPNORM_ALPHA = 2.0 ** 0.25

LANES = 128
SUBLANES = 8
TINY = 2.0 ** -126
NEG_BIG = -1e30
ACC_ROWS = LANES + 16
VMEM_LIMIT = 56 * 1024 * 1024

TOK_TILE = 512
DSA_TILE = 256
GLA_BLOCK = 256
GLA_STEP = 512
MOE_TILE = 512
RANK_TILE = 512
DISPATCH_TILE = 512
DMA_UNROLL = 8
LOG2E = 1.4426950408889634
COUNT_UNROLL = 4
BISECT_ARITH = 40
BISECT_CAP = 80


def _dot(a, b):
    return jnp.dot(a, b, preferred_element_type=F32)


def _dot_nt(a, b):
    return lax.dot_general(a, b, (((1,), (1,)), ((), ())), preferred_element_type=F32)


def _dot_tn(a, b):
    return lax.dot_general(a, b, (((0,), (0,)), ((), ())), preferred_element_type=F32)


def _params(sem):
    return pltpu.CompilerParams(dimension_semantics=sem, vmem_limit_bytes=VMEM_LIMIT)


def _rope_slab(slab, cos, sin, first_half, half):
    swapped = jnp.where(first_half, pltpu.roll(slab, LANES - half, 1), pltpu.roll(slab, half, 1))
    return slab * cos + swapped * sin


def _in_proj_kernel(x_ref, wa_ref, wi_ref, wb_ref, wg_ref, wgate_ref, bgate_ref,
                    cosa_ref, sina_ref, cosi_ref, sini_ref,
                    q_ref, k_ref, v_ref, iq_ref, ikw_ref, bq_ref, bk_ref, bv_ref, br_ref, la_ref):
    tm = x_ref.shape[0]
    xb = x_ref[...].astype(BF16)
    lane = lax.broadcasted_iota(jnp.int32, (tm, LANES), 1)

    a = _dot(xb, wa_ref[...])
    cosa, sina = cosa_ref[...], sina_ref[...]
    first_a = (lane & (DSA_HEAD_DIM - 1)) < DSA_HEAD_DIM // 2
    for c in range(DSA_WIDTH // LANES):
        sl = slice(c * LANES, (c + 1) * LANES)
        q_ref[:, sl] = _rope_slab(a[:, sl], cosa, sina, first_a, DSA_HEAD_DIM // 2).astype(BF16)
        ks = slice(DSA_WIDTH + c * LANES, DSA_WIDTH + (c + 1) * LANES)
        k_ref[:, sl] = _rope_slab(a[:, ks], cosa, sina, first_a, DSA_HEAD_DIM // 2).astype(BF16)
    v_ref[...] = a[:, 2 * DSA_WIDTH:3 * DSA_WIDTH].astype(BF16)

    ii = _dot(xb, wi_ref[...])
    cosi, sini = cosi_ref[...], sini_ref[...]
    first_i = (lane & (IDX_DIM - 1)) < IDX_DIM // 2
    for c in range(2):
        sl = slice(c * LANES, (c + 1) * LANES)
        iq_ref[:, sl] = _rope_slab(ii[:, sl], cosi, sini, first_i, IDX_DIM // 2).astype(BF16)
    last = ii[:, 2 * LANES:3 * LANES]
    ikw_ref[...] = jnp.where(lane < IDX_DIM, _rope_slab(last, cosi, sini, first_i, IDX_DIM // 2), last)

    b = _dot(xb, wb_ref[...])
    bq_ref[...] = b[:, 0:256]
    bk_ref[...] = b[:, 256:512]
    bv_ref[...] = b[:, 512:1024]
    br_ref[...] = b[:, 1024:1536]

    g = _dot(xb, wg_ref[...])
    z = _dot(g.astype(BF16), wgate_ref[...]) + bgate_ref[...]
    log_sig = jnp.minimum(z, 0.0) - jnp.log(1.0 + jnp.exp(-jnp.abs(z)))
    la_ref[...] = log_sig * (1.0 / GLA_TAU)


def _in_proj(x2, wa, wi, wb, wg, wgate, bgate, cosa, sina, cosi, sini, T):
    N = x2.shape[0]
    tm = TOK_TILE
    nt = T // tm
    row = lambda i: (i, 0)
    const = lambda i: (0, 0)
    pos = lambda i: (i % nt, 0)
    outs = [
        (DSA_WIDTH, BF16), (DSA_WIDTH, BF16), (DSA_WIDTH, BF16), (IDX_HEADS * IDX_DIM, BF16), (LANES, F32),
        (256, F32), (256, F32), (512, F32), (512, F32), (256, F32),
    ]
    return pl.pallas_call(
        _in_proj_kernel,
        grid=(N // tm,),
        in_specs=[
            pl.BlockSpec((tm, D_MODEL), row),
            pl.BlockSpec(wa.shape, const), pl.BlockSpec(wi.shape, const), pl.BlockSpec(wb.shape, const),
            pl.BlockSpec(wg.shape, const), pl.BlockSpec(wgate.shape, const), pl.BlockSpec(bgate.shape, const),
            pl.BlockSpec((tm, LANES), pos), pl.BlockSpec((tm, LANES), pos),
            pl.BlockSpec((tm, LANES), pos), pl.BlockSpec((tm, LANES), pos),
        ],
        out_specs=[pl.BlockSpec((tm, w), row) for w, _ in outs],
        out_shape=[jax.ShapeDtypeStruct((N, w), dt) for w, dt in outs],
        compiler_params=_params(("parallel",)),
    )(x2, wa, wi, wb, wg, wgate, bgate, cosa, sina, cosi, sini)


def _dsa_kernel(iqt_ref, ik_ref, iwt_ref, qt_ref, k_ref, vt_ref, o_ref,
                s_scr, qm_scr, st_scr, m_scr, acc_scr, sn_scr, rt_scr, *, topk):
    tq = DSA_TILE
    tk = DSA_TILE
    grp = tk // SUBLANES
    i = pl.program_id(1)
    kf = jnp.float32(topk)

    iqt = iqt_ref[0, 0]
    iwt = iwt_ref[0]
    krow = lax.broadcasted_iota(jnp.int32, (tk, tq), 0)
    qcol = lax.broadcasted_iota(jnp.int32, (tk, tq), 1)
    qcol8 = lax.broadcasted_iota(jnp.int32, (SUBLANES, tq), 1)

    def tree(x3, op):
        q4 = grp // 4
        return op(jnp.stack([op(x3[a * q4:(a + 1) * q4], axis=0) for a in range(4)]), axis=0)

    def fold(x, op):
        return tree(x.reshape(grp, SUBLANES, tq), op)

    def spread(x):
        return jnp.broadcast_to(x, (SUBLANES, tq))

    def head_products(j):
        keys = ik_ref[0, pl.ds(pl.multiple_of(j * tk, tk), tk), :]
        return _dot(keys, iqt)

    def weighted_relu_sum():
        acc = None
        for h in range(IDX_HEADS):
            term = jnp.maximum(rt_scr[:, h * tq:(h + 1) * tq], 0.0) * iwt[h:h + 1, :]
            acc = term if acc is None else acc + term
        return acc

    def tile_stats(s_hi, s_lo, carry):
        rmax, rmin, cpos, cnn = carry
        return (jnp.maximum(rmax, fold(s_hi, jnp.max)), jnp.minimum(rmin, fold(s_lo, jnp.min)),
                cpos + fold(jnp.where(s_hi > 0.0, 1.0, 0.0), jnp.sum),
                cnn + fold(jnp.where(s_hi >= 0.0, 1.0, 0.0), jnp.sum))

    def score_body(j, carry):
        s = weighted_relu_sum()
        s_scr[j] = s
        rt_scr[...] = head_products(j + 1)
        return tile_stats(s, s, carry)

    zeros8 = jnp.zeros((SUBLANES, tq), F32)
    init = (jnp.full((SUBLANES, tq), -jnp.inf, F32), jnp.full((SUBLANES, tq), jnp.inf, F32), zeros8, zeros8)
    rt_scr[...] = head_products(0)
    carry = lax.fori_loop(0, i, score_body, init)
    s = weighted_relu_sum()
    adm = (krow >> 6) <= (qcol >> 6)
    s_adm = jnp.where(adm, s, -jnp.inf)
    s_scr[i] = s_adm
    s_scr[i + 1] = jnp.full((tk, tq), -jnp.inf, F32)
    rmax, rmin, cpos, cnn = tile_stats(s_adm, jnp.where(adm, s, jnp.inf), carry)
    rmax = spread(jnp.max(rmax, axis=0, keepdims=True))
    rmin = spread(jnp.min(rmin, axis=0, keepdims=True))
    cpos = spread(jnp.sum(cpos, axis=0, keepdims=True))
    cnn = spread(jnp.sum(cnn, axis=0, keepdims=True))

    def count(preds):
        def one(j, accs):
            t3 = s_scr[j].reshape(grp, SUBLANES, tq)
            return tuple(a + tree(jnp.where(p(t3), 1.0, 0.0), jnp.sum) for a, p in zip(accs, preds))

        def group(g, accs):
            for u in range(COUNT_UNROLL):
                accs = one(g * COUNT_UNROLL + u, accs)
            return accs

        groups = (i + 1) >> COUNT_UNROLL.bit_length() - 1
        accs = lax.fori_loop(0, groups, group, tuple(jnp.zeros((SUBLANES, tq), F32) for _ in preds))
        accs = lax.fori_loop(groups * COUNT_UNROLL, i + 1, one, accs)
        return [spread(jnp.sum(a, axis=0, keepdims=True)) for a in accs]

    n_adm = (((qcol8 >> 6) + 1 + i * (tq // CHUNK)) * CHUNK).astype(F32)
    search = n_adm > kf
    positive = cpos >= kf
    negative = cnn < kf
    lo = jnp.where(positive, TINY, jnp.where(negative, rmin, 0.0))
    clo = jnp.where(positive, cpos, jnp.where(negative, n_adm, cnn))
    st_scr[0] = jnp.where(search, lo, rmin)
    st_scr[1] = jnp.where(negative, -TINY, rmax + (jnp.abs(rmax) * (2.0 ** -10) + TINY))
    st_scr[2] = jnp.where(search, clo, n_adm)
    zero_thr = jnp.logical_and(jnp.logical_not(positive), jnp.logical_not(negative))
    st_scr[3] = jnp.where(jnp.logical_or(jnp.logical_not(search), jnp.logical_or(zero_thr, clo == kf)), 1.0, 0.0)

    def bis_cond(carry):
        it, pending = carry
        return jnp.logical_and(pending > 0.0, it < BISECT_CAP)

    def bis_step(it):
        lo, hi, clo, done = st_scr[0], st_scr[1], st_scr[2], st_scr[3]
        a = lax.bitcast_convert_type(jnp.abs(lo), jnp.int32)
        b = lax.bitcast_convert_type(jnp.abs(hi), jnp.int32)
        geo = lax.bitcast_convert_type(a + ((b - a) >> 1), F32)
        geo = jnp.where(hi > 0.0, geo, -geo)
        mid = jnp.where(it < BISECT_ARITH, lo + (hi - lo) * 0.5, geo)
        stuck = jnp.logical_or(mid <= lo, mid >= hi)
        cnt, = count([lambda t: t >= mid[None]])
        live = jnp.logical_and(done == 0.0, jnp.logical_not(stuck))
        up = jnp.logical_and(live, cnt >= kf)
        dn = jnp.logical_and(live, cnt < kf)
        lo = jnp.where(up, mid, lo)
        clo = jnp.where(up, cnt, clo)
        hi = jnp.where(dn, mid, hi)
        done = jnp.where(jnp.logical_or(stuck, clo == kf), 1.0, done)
        st_scr[0], st_scr[1], st_scr[2], st_scr[3] = lo, hi, clo, done
        return done

    def bis_body(carry):
        it, _ = carry
        bis_step(it)
        done = bis_step(it + 1)
        return it + 2, jnp.max(1.0 - done)

    lax.while_loop(bis_cond, bis_body, (jnp.int32(0), jnp.max(1.0 - st_scr[3])))
    thr = st_scr[0]
    clo = st_scr[2]

    @pl.when(jnp.max(clo) > kf)
    def _():
        cgt, = count([lambda t: t > thr[None]])
        need = kf - cgt
        tied_value = jnp.where(clo > kf, thr, jnp.nan)
        lower = (lax.broadcasted_iota(jnp.int32, (tk, tk), 1)
                 <= lax.broadcasted_iota(jnp.int32, (tk, tk), 0))
        prefix_matrix = jnp.where(lower, 1.0, 0.0).astype(BF16)

        def drop_pair(pair, seen):
            for u in range(2):
                j = 2 * pair + u
                t3 = s_scr[j].reshape(grp, SUBLANES, tq)
                tied = t3 == tied_value[None]
                ones = jnp.where(tied, 1.0, 0.0).reshape(tk, tq).astype(BF16)
                prefix = _dot(prefix_matrix, ones)
                rank = prefix.reshape(grp, SUBLANES, tq) + seen[None]
                kill = jnp.logical_and(tied, rank > need[None])
                s_scr[j] = jnp.where(kill, -jnp.inf, t3).reshape(tk, tq)
                seen = seen + spread(prefix[tk - 1:tk, :])
            return seen

        lax.fori_loop(0, (i + 2) >> 1, drop_pair, jnp.zeros((SUBLANES, tq), F32))

    hrow = lax.broadcasted_iota(jnp.int32, (LANES, tq), 0)
    for h in range(DSA_HEADS):
        slab = qt_ref[0, (h // 2) * LANES:(h // 2 + 1) * LANES, :]
        mine = (hrow >= DSA_HEAD_DIM) if (h % 2) else (hrow < DSA_HEAD_DIM)
        qm_scr[h] = jnp.where(mine, slab, jnp.zeros_like(slab))
        m_scr[h] = jnp.full((SUBLANES, tq), NEG_BIG, F32)
        acc_scr[h] = jnp.zeros((ACC_ROWS, tq), F32)
    ones_rows = jnp.ones((ACC_ROWS - LANES, tk), BF16)

    def slab(h):
        return slice((h // 2) * LANES, (h // 2 + 1) * LANES)

    def tile_operands(j):
        t3 = s_scr[j].reshape(grp, SUBLANES, tq)
        bias = jnp.where(t3 >= thr[None], 0.0, NEG_BIG).reshape(tk, tq)
        return k_ref[0, pl.ds(pl.multiple_of(j * tk, tk), tk), :], bias

    def masked_scores(keys, bias, h):
        return _dot(keys[:, slab(h)], qm_scr[h]) + bias

    keys0, bias0 = tile_operands(0)
    for h in range(DSA_HEADS):
        sn_scr[h] = masked_scores(keys0, bias0, h)

    def att_body(j, _):
        keys_n, bias_n = tile_operands(jnp.minimum(j + 1, i))
        for h in range(DSA_HEADS):
            s3 = sn_scr[h].reshape(grp, SUBLANES, tq)
            m_prev = m_scr[h]
            m_new = jnp.maximum(m_prev, spread(jnp.max(tree(s3, jnp.max), axis=0, keepdims=True)))
            p = jnp.exp2((s3 - m_new[None]).reshape(tk, tq).astype(BF16))
            corr = jnp.exp2(m_prev - m_new)
            pv = _dot(jnp.concatenate([vt_ref[0, j, slab(h), :], ones_rows], axis=0), p)
            acc = acc_scr[h].reshape(ACC_ROWS // SUBLANES, SUBLANES, tq) * corr[None]
            acc_scr[h] = acc.reshape(ACC_ROWS, tq) + pv
            m_scr[h] = m_new
            sn_scr[h] = masked_scores(keys_n, bias_n, h)
        return 0

    lax.fori_loop(0, i + 1, att_body, 0)

    def normalised(h):
        inv = 1.0 / acc_scr[h, LANES:LANES + SUBLANES, :]
        return (acc_scr[h, 0:LANES, :].reshape(LANES // SUBLANES, SUBLANES, tq) * inv[None]).reshape(LANES, tq)

    for sp in range(DSA_HEADS // 2):
        pair = jnp.where(hrow < DSA_HEAD_DIM, normalised(2 * sp), normalised(2 * sp + 1))
        o_ref[0, sp * LANES:(sp + 1) * LANES, :] = pair.astype(o_ref.dtype)


def _dsa(iqt, ik, iwt, qt, k, vt):
    B, T, _ = k.shape
    tq = DSA_TILE
    nq = T // tq
    topk = min(DSA_TOPK_MAX, T // 4)
    kern = functools.partial(_dsa_kernel, topk=topk)
    return pl.pallas_call(
        kern,
        grid=(B, nq),
        in_specs=[
            pl.BlockSpec((1, 1, IDX_DIM, IDX_HEADS * tq), lambda b, i: (b, i, 0, 0)),
            pl.BlockSpec((1, T, IDX_DIM), lambda b, i: (b, 0, 0)),
            pl.BlockSpec((1, IDX_HEADS, tq), lambda b, i: (b, 0, i)),
            pl.BlockSpec((1, DSA_WIDTH, tq), lambda b, i: (b, 0, i)),
            pl.BlockSpec((1, T, DSA_WIDTH), lambda b, i: (b, 0, 0)),
            pl.BlockSpec((1, nq, DSA_WIDTH, tq), lambda b, i: (b, 0, 0, 0)),
        ],
        out_specs=pl.BlockSpec((1, DSA_WIDTH, tq), lambda b, i: (b, 0, i)),
        out_shape=jax.ShapeDtypeStruct((B, DSA_WIDTH, T), BF16),
        scratch_shapes=[
            pltpu.VMEM((nq + 1, tq, tq), F32),
            pltpu.VMEM((DSA_HEADS, LANES, tq), BF16),
            pltpu.VMEM((4, SUBLANES, tq), F32),
            pltpu.VMEM((DSA_HEADS, SUBLANES, tq), F32),
            pltpu.VMEM((DSA_HEADS, ACC_ROWS, tq), F32),
            pltpu.VMEM((DSA_HEADS, tq, tq), F32),
            pltpu.VMEM((tq, IDX_HEADS * tq), F32),
        ],
        compiler_params=_params(("parallel", "arbitrary")),
    )(iqt, ik, iwt, qt, k, vt)


def _gla_kernel(q_ref, k_ref, v_ref, r_ref, la_ref, g_ref, o_ref, s_scr):
    rows = GLA_BLOCK
    nch = rows // CHUNK
    npair = GLA_HEADS // 2
    nsub = q_ref.shape[1] // rows

    @pl.when(pl.program_id(1) == 0)
    def _():
        s_scr[...] = jnp.zeros_like(s_scr)

    ri = lax.broadcasted_iota(jnp.int32, (rows, rows), 0)
    ci = lax.broadcasted_iota(jnp.int32, (rows, rows), 1)
    causal = jnp.logical_and((ri >> 6) == (ci >> 6), ci <= ri)
    tri = jnp.where(causal, 1.0, 0.0).astype(BF16)
    lane = lax.broadcasted_iota(jnp.int32, (rows, LANES), 1)
    eye = lax.broadcasted_iota(jnp.int32, (LANES, LANES), 0) == lax.broadcasted_iota(jnp.int32, (LANES, LANES), 1)
    top_rows = lax.broadcasted_iota(jnp.int32, (LANES, GLA_DV), 0) < GLA_DK

    def decayed(r0):
        la = la_ref[0, r0:r0 + rows, :]
        la_hi = la.astype(BF16)
        rem = la - la_hi.astype(F32)
        la_mid = rem.astype(BF16)
        la_lo = (rem - la_mid.astype(F32)).astype(BF16)
        b = _dot(tri, la_hi) + _dot(tri, la_mid) + _dot(tri, la_lo)
        b_last = jnp.concatenate(
            [jnp.broadcast_to(b[(c + 1) * CHUNK - 1:(c + 1) * CHUNK, :], (CHUNK, b.shape[1])) for c in range(nch)],
            axis=0)
        q = q_ref[0, r0:r0 + rows, :]
        k = k_ref[0, r0:r0 + rows, :]
        return q * jnp.exp(b), (k * jnp.exp(-b)).astype(BF16), (k * jnp.exp(b_last - b)).astype(BF16), jnp.exp(b_last)

    subs = [decayed(sb * rows) for sb in range(nsub)]

    for pr in range(npair):
        sl = slice(pr * LANES, (pr + 1) * LANES)
        heads = (2 * pr, 2 * pr + 1)
        state = s_scr[pr]
        for sb in range(nsub):
            r0 = sb * rows
            qg, kg, kd, decay_rows = subs[sb]
            qg_p = qg[:, sl]
            kg_p = kg[:, sl]
            kd_p = kd[:, sl]
            qg_h = [jnp.where(lane < GLA_DK, qg_p, 0.0).astype(BF16), jnp.where(lane >= GLA_DK, qg_p, 0.0).astype(BF16)]
            v_h = [v_ref[0, r0:r0 + rows, h * GLA_DV:(h + 1) * GLA_DV].astype(BF16) for h in heads]
            o_h = []
            for t in range(2):
                a = jnp.where(causal, _dot_nt(qg_h[t], kg_p), 0.0)
                o_h.append(_dot(a.astype(BF16), v_h[t]))
            inter = [[], []]
            for c in range(nch):
                rs = slice(c * CHUNK, (c + 1) * CHUNK)
                state_b = state.astype(BF16)
                for t in range(2):
                    inter[t].append(_dot(qg_h[t][rs], state_b))
                kv = jnp.where(top_rows, _dot_tn(kd_p[rs], v_h[0][rs]), _dot_tn(kd_p[rs], v_h[1][rs]))
                drow = jnp.broadcast_to(decay_rows[c * CHUNK:c * CHUNK + 1, sl], (LANES, LANES))
                dcol = jnp.sum(jnp.where(eye, drow, 0.0), axis=1, keepdims=True)
                state = dcol * state + kv
            for t in range(2):
                h = heads[t]
                o = o_h[t] + jnp.concatenate(inter[t], axis=0)
                ms = jnp.mean(o * o, axis=1, keepdims=True)
                on = o * lax.rsqrt(ms + LN_EPS) * g_ref[...]
                r = r_ref[0, r0:r0 + rows, h * GLA_DV:(h + 1) * GLA_DV]
                o_ref[0, r0:r0 + rows, h * GLA_DV:(h + 1) * GLA_DV] = (on * (r / (1.0 + jnp.exp(-r)))).astype(o_ref.dtype)
        s_scr[pr] = state


def _gla(bq, bk, bv, br, la, g):
    B, T, _ = bq.shape
    rows = GLA_STEP
    blk = lambda w: pl.BlockSpec((1, rows, w), lambda b, i: (b, i, 0))
    return pl.pallas_call(
        _gla_kernel,
        grid=(B, T // rows),
        in_specs=[blk(256), blk(256), blk(512), blk(512), blk(256), pl.BlockSpec((1, GLA_DV), lambda b, i: (0, 0))],
        out_specs=blk(GLA_WIDTH),
        out_shape=jax.ShapeDtypeStruct((B, T, GLA_WIDTH), BF16),
        scratch_shapes=[pltpu.VMEM((GLA_HEADS // 2, 2 * GLA_DK, GLA_DV), F32)],
        compiler_params=_params(("parallel", "arbitrary")),
    )(bq, bk, bv, br, la, g)


def _layer_norm(x, g, b):
    mu = jnp.mean(x, axis=1, keepdims=True)
    xc = x - mu
    var = jnp.mean(xc * xc, axis=1, keepdims=True)
    return xc * lax.rsqrt(var + LN_EPS) * g + b


def _out_proj_kernel(x_ref, ya_ref, yb_ref, wo_ref, g_ref, b_ref, wr_hi_ref, wr_lo_ref, br_ref, h_ref, route_ref):
    tm = x_ref.shape[0]
    mix = _dot(ya_ref[...], wo_ref[0:DSA_WIDTH, :]) + _dot(yb_ref[...], wo_ref[DSA_WIDTH:D_MODEL, :])
    h = _layer_norm(DEEPNORM_ALPHA * x_ref[...] + mix, g_ref[...], b_ref[...])
    h_ref[...] = h

    h_hi = h.astype(BF16)
    h_lo = (h - h_hi.astype(F32)).astype(BF16)
    logits = _dot(h_hi, wr_hi_ref[...]) + _dot(h_lo, wr_hi_ref[...]) + _dot(h_hi, wr_lo_ref[...]) + br_ref[...]

    lane = lax.broadcasted_iota(jnp.int32, (tm, LANES), 1)
    lanef = lane.astype(F32)
    gl = jnp.where(lane < N_GROUPS, logits, -jnp.inf)
    gmax = jnp.max(gl, axis=1, keepdims=True)
    gsel = jnp.min(jnp.where(gl == gmax, lanef, 1e9), axis=1, keepdims=True)
    pg = 1.0 / jnp.sum(jnp.exp(gl - gmax), axis=1, keepdims=True)
    egrp = ((lane - N_GROUPS) >> 3).astype(F32)
    in_grp = jnp.logical_and(jnp.logical_and(lane >= N_GROUPS, lane < N_GROUPS + N_EXPERTS), egrp == gsel)
    el = jnp.where(in_grp, logits, -jnp.inf)
    t1 = jnp.max(el, axis=1, keepdims=True)
    i1 = jnp.min(jnp.where(el == t1, lanef, 1e9), axis=1, keepdims=True)
    el2 = jnp.where(lanef == i1, -jnp.inf, el)
    t2 = jnp.max(el2, axis=1, keepdims=True)
    i2 = jnp.min(jnp.where(el2 == t2, lanef, 1e9), axis=1, keepdims=True)
    e21 = jnp.exp(t2 - t1)
    g1 = pg / (1.0 + e21)
    g2 = pg * e21 / (1.0 + e21)
    route = jnp.where(lane == 0, i1 - N_GROUPS, 0.0)
    route = jnp.where(lane == 1, i2 - N_GROUPS, route)
    route = jnp.where(lane == 2, g1, route)
    route = jnp.where(lane == 3, g2, route)
    route_ref[...] = route


def _out_proj(x2, ya, yb, wo, g1, b1, wr_hi, wr_lo, br):
    N = x2.shape[0]
    tm = TOK_TILE
    row = lambda i: (i, 0)
    const = lambda i: (0, 0)
    return pl.pallas_call(
        _out_proj_kernel,
        grid=(N // tm,),
        in_specs=[
            pl.BlockSpec((tm, D_MODEL), row), pl.BlockSpec((tm, DSA_WIDTH), row), pl.BlockSpec((tm, GLA_WIDTH), row),
            pl.BlockSpec(wo.shape, const), pl.BlockSpec(g1.shape, const), pl.BlockSpec(b1.shape, const),
            pl.BlockSpec(wr_hi.shape, const), pl.BlockSpec(wr_lo.shape, const), pl.BlockSpec(br.shape, const),
        ],
        out_specs=[pl.BlockSpec((tm, D_MODEL), row), pl.BlockSpec((tm, LANES), row)],
        out_shape=[jax.ShapeDtypeStruct((N, D_MODEL), F32), jax.ShapeDtypeStruct((N, LANES), F32)],
        compiler_params=_params(("parallel",)),
    )(x2, ya, yb, wo, g1, b1, wr_hi, wr_lo, br)


def _rank_kernel(route_ref, rank_ref, cnt_ref, carry_scr):
    tm = route_ref.shape[0]

    @pl.when(pl.program_id(0) == 0)
    def _():
        carry_scr[...] = jnp.zeros_like(carry_scr)

    route = route_ref[...]
    lanef = lax.broadcasted_iota(jnp.int32, (tm, LANES), 1).astype(F32)
    e1 = route[:, 0:1]
    e2 = route[:, 1:2]
    hit1 = lanef == e1
    hit2 = lanef == e2
    onehot = jnp.where(jnp.logical_or(hit1, hit2), 1.0, 0.0).astype(BF16)
    ri = lax.broadcasted_iota(jnp.int32, (tm, tm), 0)
    ci = lax.broadcasted_iota(jnp.int32, (tm, tm), 1)
    before = jnp.where(ci < ri, 1.0, 0.0).astype(BF16)
    prefix = _dot(before, onehot) + carry_scr[0:1, :]
    r1 = jnp.sum(jnp.where(hit1, prefix, 0.0), axis=1, keepdims=True)
    r2 = jnp.sum(jnp.where(hit2, prefix, 0.0), axis=1, keepdims=True)
    rank_ref[...] = jnp.where(lanef == 0.0, r1, jnp.where(lanef == 1.0, r2, 0.0))
    total = _dot(jnp.ones((8, tm), BF16), onehot)
    carry_scr[...] = carry_scr[...] + total
    cnt_ref[...] = carry_scr[...]


def _rank(route):
    N = route.shape[0]
    tm = RANK_TILE
    return pl.pallas_call(
        _rank_kernel,
        grid=(N // tm,),
        in_specs=[pl.BlockSpec((tm, LANES), lambda i: (i, 0))],
        out_specs=[pl.BlockSpec((tm, LANES), lambda i: (i, 0)), pl.BlockSpec((8, LANES), lambda i: (0, 0))],
        out_shape=[jax.ShapeDtypeStruct((N, LANES), F32), jax.ShapeDtypeStruct((8, LANES), F32)],
        scratch_shapes=[pltpu.VMEM((8, LANES), F32)],
        compiler_params=_params(("arbitrary",)),
    )(route)


def _dispatch_kernel(pos_ref, h_ref, xs_in_ref, xs_ref, hbuf, load_sem, row_sem):
    del xs_in_ref
    tm = DISPATCH_TILE
    nbuf = hbuf.shape[0]
    i = pl.program_id(0)
    last = pl.num_programs(0) - 1

    def tile_load(step):
        return pltpu.make_async_copy(h_ref.at[pl.ds(step * tm, tm), :], hbuf.at[step % nbuf], load_sem.at[step % nbuf])

    def row_copy(step, r, slot):
        p = pos_ref[2 * (step * tm + r) + slot]
        return pltpu.make_async_copy(hbuf.at[step % nbuf, pl.ds(r, 1), :], xs_ref.at[pl.ds(p, 1), :],
                                     row_sem.at[step % nbuf])

    def for_rows(fn):
        def body(r, _):
            fn(r, 0)
            fn(r, 1)
            return 0
        lax.fori_loop(0, tm, body, 0, unroll=DMA_UNROLL)

    @pl.when(i == 0)
    def _():
        tile_load(0).start()

    @pl.when(i < last)
    def _():
        tile_load(i + 1).start()

    tile_load(i).wait()
    for_rows(lambda r, slot: row_copy(i, r, slot).start())

    @pl.when(i > 0)
    def _():
        for_rows(lambda r, slot: row_copy(i - 1, r, slot).wait())

    @pl.when(i == last)
    def _():
        for_rows(lambda r, slot: row_copy(i, r, slot).wait())


def _dispatch(pos_flat, h, xs_init):
    N = h.shape[0]
    grid_spec = pltpu.PrefetchScalarGridSpec(
        num_scalar_prefetch=1,
        grid=(N // DISPATCH_TILE,),
        in_specs=[pl.BlockSpec(memory_space=pl.ANY), pl.BlockSpec(memory_space=pl.ANY)],
        out_specs=pl.BlockSpec(memory_space=pl.ANY),
        scratch_shapes=[pltpu.VMEM((3, DISPATCH_TILE, D_MODEL), F32),
                        pltpu.SemaphoreType.DMA((3,)), pltpu.SemaphoreType.DMA((3,))],
    )
    return pl.pallas_call(
        _dispatch_kernel,
        grid_spec=grid_spec,
        out_shape=jax.ShapeDtypeStruct(xs_init.shape, xs_init.dtype),
        input_output_aliases={2: 0},
        compiler_params=_params(("arbitrary",)),
    )(pos_flat, h, xs_init)


def _ffn_kernel(te_ref, na_ref, x_ref, wi_ref, wo_ref, y_ref, wi_bf, wo_bf):
    i = pl.program_id(0)
    active = i < na_ref[0]
    fresh = jnp.logical_or(i == 0, te_ref[i] != te_ref[jnp.maximum(i - 1, 0)])

    @pl.when(jnp.logical_and(active, fresh))
    def _():
        wi_bf[...] = wi_ref[0].astype(BF16)
        wo_bf[...] = wo_ref[0].astype(BF16)

    @pl.when(active)
    def _():
        hid = _dot(x_ref[...].astype(BF16), wi_bf[...])
        hg = hid[:, :D_EXPERT]
        hu = hid[:, D_EXPERT:]
        act = (hg / (1.0 + jnp.exp(-hg))) * hu
        y_ref[...] = _dot(act.astype(BF16), wo_bf[...])

    @pl.when(jnp.logical_not(active))
    def _():
        y_ref[...] = jnp.zeros_like(y_ref)


def _ffn(tile_expert, n_active, xs, w_e_in, w_e_out):
    R = xs.shape[0]
    tm = MOE_TILE

    def live(i, te, na):
        return jnp.minimum(i, na[0] - 1)

    grid_spec = pltpu.PrefetchScalarGridSpec(
        num_scalar_prefetch=2,
        grid=(R // tm,),
        in_specs=[
            pl.BlockSpec((tm, D_MODEL), lambda i, te, na: (live(i, te, na), 0)),
            pl.BlockSpec((1, D_MODEL, 2 * D_EXPERT), lambda i, te, na: (te[live(i, te, na)], 0, 0)),
            pl.BlockSpec((1, D_EXPERT, D_MODEL), lambda i, te, na: (te[live(i, te, na)], 0, 0)),
        ],
        out_specs=pl.BlockSpec((tm, D_MODEL), lambda i, te, na: (i, 0)),
        scratch_shapes=[pltpu.VMEM((D_MODEL, 2 * D_EXPERT), BF16), pltpu.VMEM((D_EXPERT, D_MODEL), BF16)],
    )
    return pl.pallas_call(
        _ffn_kernel,
        grid_spec=grid_spec,
        out_shape=jax.ShapeDtypeStruct((R, D_MODEL), F32),
        compiler_params=_params(("arbitrary",)),
    )(tile_expert, n_active, xs, w_e_in, w_e_out)


def _combine_kernel(pos_ref, h_ref, route_ref, g_ref, b_ref, ys_ref, o_ref, buf, sem):
    tm = h_ref.shape[0]
    i = pl.program_id(0)
    cur = i % 2

    def row_copy(step, r, slot):
        p = pos_ref[2 * (step * tm + r) + slot]
        half = step % 2
        return pltpu.make_async_copy(ys_ref.at[pl.ds(p, 1), :], buf.at[half, slot, pl.ds(r, 1), :], sem.at[half])

    def for_rows(fn):
        def body(r, _):
            fn(r, 0)
            fn(r, 1)
            return 0
        lax.fori_loop(0, tm, body, 0, unroll=DMA_UNROLL)

    @pl.when(i == 0)
    def _():
        for_rows(lambda r, slot: row_copy(0, r, slot).start())

    @pl.when(i + 1 < pl.num_programs(0))
    def _():
        for_rows(lambda r, slot: row_copy(i + 1, r, slot).start())

    for_rows(lambda r, slot: row_copy(i, r, slot).wait())
    route = route_ref[...]
    ffn = buf[cur, 0] * route[:, 2:3] + buf[cur, 1] * route[:, 3:4]
    o_ref[...] = _layer_norm(DEEPNORM_ALPHA * h_ref[...] + ffn, g_ref[...], b_ref[...])


def _combine(pos_flat, h, route, g2, b2, ys):
    N = h.shape[0]
    tm = TOK_TILE
    row = lambda i, pos: (i, 0)
    const = lambda i, pos: (0, 0)
    grid_spec = pltpu.PrefetchScalarGridSpec(
        num_scalar_prefetch=1,
        grid=(N // tm,),
        in_specs=[
            pl.BlockSpec((tm, D_MODEL), row), pl.BlockSpec((tm, LANES), row),
            pl.BlockSpec(g2.shape, const), pl.BlockSpec(b2.shape, const),
            pl.BlockSpec(memory_space=pl.ANY),
        ],
        out_specs=pl.BlockSpec((tm, D_MODEL), row),
        scratch_shapes=[pltpu.VMEM((2, 2, tm, D_MODEL), F32), pltpu.SemaphoreType.DMA((2,))],
    )
    return pl.pallas_call(
        _combine_kernel,
        grid_spec=grid_spec,
        out_shape=jax.ShapeDtypeStruct((N, D_MODEL), F32),
        compiler_params=_params(("arbitrary",)),
    )(pos_flat, h, route, g2, b2, ys)


def _rope_tables(T, dim):
    half = dim // 2
    inv = 1.0 / (ROPE_THETA ** (jnp.arange(half, dtype=F32) / half))
    ang = jnp.arange(T).astype(F32)[:, None] * inv[None, :]
    cos = jnp.cos(ang)
    sin = jnp.sin(ang)
    reps = LANES // dim
    cos_t = jnp.tile(jnp.concatenate([cos, cos], axis=1), (1, reps))
    sin_t = jnp.tile(jnp.concatenate([-sin, sin], axis=1), (1, reps))
    return cos_t, sin_t


def _pad_cols(w, width):
    return jnp.pad(w, ((0, 0), (0, width - w.shape[1])))


def _layer(x, w_in, w_gla_gate, b_gla_gate, g_gla_norm, w_out, ln1_g, ln1_b,
           w_gr, b_gr, w_er, b_er, w_e_in, w_e_out, ln2_g, ln2_b):
    B, T, D = x.shape
    N = B * T
    assert D == D_MODEL and T % DSA_TILE == 0 and N % DISPATCH_TILE == 0 and DISPATCH_TILE % RANK_TILE == 0
    x2 = x.reshape(N, D)

    sizes = (512, 512, 512, 256, 32, 8, 256, 256, 512, 512, 16)
    offs = np.concatenate([[0], np.cumsum(sizes)])
    col = lambda k: w_in[:, offs[k]:offs[k + 1]]
    wa = jnp.concatenate([col(0) * (DSA_HEAD_DIM ** -0.5 * LOG2E), col(1), col(2)], axis=1).astype(BF16)
    wi = _pad_cols(jnp.concatenate([col(3), col(4), col(5) * IDX_SCALE], axis=1), 3 * LANES).astype(BF16)
    wb = jnp.concatenate([col(6) * (GLA_DK ** -0.5), col(7), col(8), col(9)], axis=1).astype(BF16)
    wg = _pad_cols(col(10), LANES).astype(BF16)
    wgate = jnp.pad(w_gla_gate, ((0, LANES - GLA_GATE_RANK), (0, 0))).astype(BF16)
    bgate = b_gla_gate.reshape(1, -1)
    cosa, sina = _rope_tables(T, DSA_HEAD_DIM)
    cosi, sini = _rope_tables(T, IDX_DIM)

    q, k, v, iq, ikw, bq, bk, bv, br, la = _in_proj(x2, wa, wi, wb, wg, wgate, bgate, cosa, sina, cosi, sini, T)

    nq = T // DSA_TILE
    iqt = iq.reshape(B, nq, DSA_TILE, IDX_HEADS, IDX_DIM).transpose(0, 1, 4, 3, 2)
    iqt = iqt.reshape(B, nq, IDX_DIM, IDX_HEADS * DSA_TILE)
    ik = ikw[:, :IDX_DIM].astype(BF16).reshape(B, T, IDX_DIM)
    iwt = ikw[:, IDX_DIM:IDX_DIM + IDX_HEADS].reshape(B, T, IDX_HEADS).transpose(0, 2, 1)
    qt = q.reshape(B, T, DSA_WIDTH).transpose(0, 2, 1)
    vt = v.reshape(B, nq, DSA_TILE, DSA_WIDTH).transpose(0, 1, 3, 2)
    ya = _dsa(iqt, ik, iwt, qt, k.reshape(B, T, DSA_WIDTH), vt).transpose(0, 2, 1)

    r3 = lambda a: a.reshape(B, T, a.shape[-1])
    yb = _gla(r3(bq), r3(bk), r3(bv), r3(br), r3(la), g_gla_norm.reshape(1, GLA_DV))

    wr = _pad_cols(jnp.concatenate([w_gr, w_er], axis=1), LANES)
    wr_hi = wr.astype(BF16)
    wr_lo = (wr - wr_hi.astype(F32)).astype(BF16)
    brt = _pad_cols(jnp.concatenate([b_gr, b_er]).reshape(1, -1), LANES)
    h, route = _out_proj(x2, ya.reshape(N, DSA_WIDTH), yb.reshape(N, GLA_WIDTH), w_out.astype(BF16),
                         ln1_g.reshape(1, D), ln1_b.reshape(1, D), wr_hi, wr_lo, brt)

    rank, cnt = _rank(route)
    counts = cnt[0, :N_EXPERTS].astype(jnp.int32)
    padded = ((counts + MOE_TILE - 1) // MOE_TILE) * MOE_TILE
    ends = jnp.cumsum(padded)
    starts = ends - padded
    eid = route[:, 0:2].astype(jnp.int32)
    pos = (starts[eid] + rank[:, 0:2].astype(jnp.int32)).reshape(-1)
    n_rows = 2 * N + N_EXPERTS * MOE_TILE
    n_tiles = n_rows // MOE_TILE
    tile_start = jnp.arange(n_tiles, dtype=jnp.int32) * MOE_TILE
    tile_expert = jnp.minimum(jnp.sum(tile_start[:, None] >= ends[None, :], axis=1), N_EXPERTS - 1).astype(jnp.int32)
    n_active = (ends[-1] // MOE_TILE).astype(jnp.int32).reshape(1)

    xs = _dispatch(pos, h, jnp.zeros((n_rows, D), F32))
    ys = _ffn(tile_expert, n_active, xs, w_e_in, w_e_out)
    out = _combine(pos, h, route, ln2_g.reshape(1, D), ln2_b.reshape(1, D), ys)
    return out.reshape(B, T, D)


def kernel(x, w_in, w_gla_gate, b_gla_gate, g_gla_norm, w_out, ln1_g, ln1_b, w_group_router, b_group_router,
           w_expert_router, b_expert_router, w_expert_in, w_expert_out, ln2_g, ln2_b):
    h = x
    for l in range(w_in.shape[0]):
        h = _layer(h, w_in[l], w_gla_gate[l], b_gla_gate[l], g_gla_norm[l], w_out[l], ln1_g[l], ln1_b[l],
                   w_group_router[l], b_group_router[l], w_expert_router[l], b_expert_router[l],
                   w_expert_in[l], w_expert_out[l], ln2_g[l], ln2_b[l])
    return h
```

```python
import functools

import numpy as np
import jax
import jax.numpy as jnp
from jax import lax
from jax.experimental import pallas as pl
from jax.experimental.pallas import tpu as pltpu

F32 = jnp.float32
BF16 = jnp.bfloat16

D_MODEL = 1024
CHUNK = 64
ROPE_THETA = 10000.0
LN_EPS = 1e-5
DSA_WIDTH = 512
DSA_HEAD_DIM = 64
DSA_HEADS = 8
IDX_HEADS = 8
IDX_DIM = 32
IDX_SCALE = (IDX_HEADS * IDX_DIM) ** -0.5
DSA_TOPK_MAX = 256
GLA_WIDTH = 512
GLA_HEADS = 4
GLA_DV = 128
GLA_DK = 64
GLA_GATE_RANK = 16
GLA_TAU = 16.0
N_GROUPS = 4
EXPERTS_PER_GROUP = 8
N_EXPERTS = 32
D_EXPERT = 512
DEEPNORM_ALPHA = 2.0 ** 0.25

LANES = 128
SUBLANES = 8
TINY = 2.0 ** -126
NEG_BIG = -1e30
ACC_ROWS = LANES + 16
VMEM_LIMIT = 56 * 1024 * 1024

TOK_TILE = 512
DSA_TILE = 256
GLA_BLOCK = 256
GLA_STEP = 512
MOE_TILE = 512
RANK_TILE = 512
DISPATCH_TILE = 512
DMA_UNROLL = 8
LOG2E = 1.4426950408889634
COUNT_UNROLL = 4
BISECT_ARITH = 40
BISECT_CAP = 80


def _dot(a, b):
    return jnp.dot(a, b, preferred_element_type=F32)


def _dot_nt(a, b):
    return lax.dot_general(a, b, (((1,), (1,)), ((), ())), preferred_element_type=F32)


def _dot_tn(a, b):
    return lax.dot_general(a, b, (((0,), (0,)), ((), ())), preferred_element_type=F32)


def _pack_halves(x):
    w = x.shape[1] // 2
    hi = lax.bitcast_convert_type(x[:, :w].astype(BF16).astype(F32), jnp.uint32)
    lo = lax.bitcast_convert_type(x[:, w:].astype(BF16).astype(F32), jnp.uint32)
    return (hi & jnp.uint32(0xFFFF0000)) | (lo >> 16)


def _unpack_halves(p):
    left = lax.bitcast_convert_type(p & jnp.uint32(0xFFFF0000), F32)
    right = lax.bitcast_convert_type(p << 16, F32)
    return left, right


def _params(sem):
    return pltpu.CompilerParams(dimension_semantics=sem, vmem_limit_bytes=VMEM_LIMIT)


def _rope_slab(slab, cos, sin, first_half, half):
    swapped = jnp.where(first_half, pltpu.roll(slab, LANES - half, 1), pltpu.roll(slab, half, 1))
    return slab * cos + swapped * sin


def _in_proj_kernel(x_ref, wa_ref, wi_ref, wb_ref, wg_ref, wgate_ref, bgate_ref,
                    cosa_ref, sina_ref, cosi_ref, sini_ref,
                    q_ref, k_ref, v_ref, iq_ref, ikw_ref, bq_ref, bk_ref, bv_ref, br_ref, la_ref):
    tm = x_ref.shape[0]
    xb = x_ref[...].astype(BF16)
    lane = lax.broadcasted_iota(jnp.int32, (tm, LANES), 1)

    a = _dot(xb, wa_ref[...])
    cosa, sina = cosa_ref[...], sina_ref[...]
    first_a = (lane & (DSA_HEAD_DIM - 1)) < DSA_HEAD_DIM // 2
    for c in range(DSA_WIDTH // LANES):
        sl = slice(c * LANES, (c + 1) * LANES)
        q_ref[:, sl] = _rope_slab(a[:, sl], cosa, sina, first_a, DSA_HEAD_DIM // 2).astype(BF16)
        ks = slice(DSA_WIDTH + c * LANES, DSA_WIDTH + (c + 1) * LANES)
        k_ref[:, sl] = _rope_slab(a[:, ks], cosa, sina, first_a, DSA_HEAD_DIM // 2).astype(BF16)
    v_ref[...] = a[:, 2 * DSA_WIDTH:3 * DSA_WIDTH].astype(BF16)

    ii = _dot(xb, wi_ref[...])
    cosi, sini = cosi_ref[...], sini_ref[...]
    first_i = (lane & (IDX_DIM - 1)) < IDX_DIM // 2
    for c in range(2):
        sl = slice(c * LANES, (c + 1) * LANES)
        iq_ref[:, sl] = _rope_slab(ii[:, sl], cosi, sini, first_i, IDX_DIM // 2).astype(BF16)
    last = ii[:, 2 * LANES:3 * LANES]
    ikw_ref[...] = jnp.where(lane < IDX_DIM, _rope_slab(last, cosi, sini, first_i, IDX_DIM // 2), last)

    b = _dot(xb, wb_ref[...])
    bq_ref[...] = b[:, 0:256]
    bk_ref[...] = b[:, 256:512]
    bv_ref[...] = b[:, 512:1024]
    br_ref[...] = b[:, 1024:1536]

    g = _dot(xb, wg_ref[...])
    z = _dot(g.astype(BF16), wgate_ref[...]) + bgate_ref[...]
    log_sig = jnp.minimum(z, 0.0) - jnp.log(1.0 + jnp.exp(-jnp.abs(z)))
    la_ref[...] = log_sig * (1.0 / GLA_TAU)


def _in_proj(x2, wa, wi, wb, wg, wgate, bgate, cosa, sina, cosi, sini, T):
    N = x2.shape[0]
    tm = TOK_TILE
    nt = T // tm
    row = lambda i: (i, 0)
    const = lambda i: (0, 0)
    pos = lambda i: (i % nt, 0)
    outs = [
        (DSA_WIDTH, BF16), (DSA_WIDTH, BF16), (DSA_WIDTH, BF16), (IDX_HEADS * IDX_DIM, BF16), (LANES, F32),
        (256, F32), (256, F32), (512, F32), (512, F32), (256, F32),
    ]
    return pl.pallas_call(
        _in_proj_kernel,
        grid=(N // tm,),
        in_specs=[
            pl.BlockSpec((tm, D_MODEL), row),
            pl.BlockSpec(wa.shape, const), pl.BlockSpec(wi.shape, const), pl.BlockSpec(wb.shape, const),
            pl.BlockSpec(wg.shape, const), pl.BlockSpec(wgate.shape, const), pl.BlockSpec(bgate.shape, const),
            pl.BlockSpec((tm, LANES), pos), pl.BlockSpec((tm, LANES), pos),
            pl.BlockSpec((tm, LANES), pos), pl.BlockSpec((tm, LANES), pos),
        ],
        out_specs=[pl.BlockSpec((tm, w), row) for w, _ in outs],
        out_shape=[jax.ShapeDtypeStruct((N, w), dt) for w, dt in outs],
        compiler_params=_params(("parallel",)),
    )(x2, wa, wi, wb, wg, wgate, bgate, cosa, sina, cosi, sini)


def _dsa_kernel(iqt_ref, ik_ref, iwt_ref, qt_ref, k_ref, vt_ref, o_ref,
                s_scr, qm_scr, st_scr, m_scr, acc_scr, sn_scr, rt_scr, *, topk):
    tq = DSA_TILE
    tk = DSA_TILE
    grp = tk // SUBLANES
    i = pl.program_id(1)
    kf = jnp.float32(topk)

    iqt = iqt_ref[0, 0]
    iwt = iwt_ref[0]
    krow = lax.broadcasted_iota(jnp.int32, (tk, tq), 0)
    qcol = lax.broadcasted_iota(jnp.int32, (tk, tq), 1)
    qcol8 = lax.broadcasted_iota(jnp.int32, (SUBLANES, tq), 1)

    def tree(x3, op):
        q4 = grp // 4
        return op(jnp.stack([op(x3[a * q4:(a + 1) * q4], axis=0) for a in range(4)]), axis=0)

    def fold(x, op):
        return tree(x.reshape(grp, SUBLANES, tq), op)

    def spread(x):
        return jnp.broadcast_to(x, (SUBLANES, tq))

    def head_products(j):
        keys = ik_ref[0, pl.ds(pl.multiple_of(j * tk, tk), tk), :]
        return _dot(keys, iqt)

    def weighted_relu_sum():
        acc = None
        for h in range(IDX_HEADS):
            term = jnp.maximum(rt_scr[:, h * tq:(h + 1) * tq], 0.0) * iwt[h:h + 1, :]
            acc = term if acc is None else acc + term
        return acc

    def tile_stats(s_hi, s_lo, carry):
        rmax, rmin, cpos, cnn = carry
        return (jnp.maximum(rmax, fold(s_hi, jnp.max)), jnp.minimum(rmin, fold(s_lo, jnp.min)),
                cpos + fold(jnp.where(s_hi > 0.0, 1.0, 0.0), jnp.sum),
                cnn + fold(jnp.where(s_hi >= 0.0, 1.0, 0.0), jnp.sum))

    def score_body(j, carry):
        s = weighted_relu_sum()
        s_scr[j] = s
        rt_scr[...] = head_products(j + 1)
        return tile_stats(s, s, carry)

    zeros8 = jnp.zeros((SUBLANES, tq), F32)
    init = (jnp.full((SUBLANES, tq), -jnp.inf, F32), jnp.full((SUBLANES, tq), jnp.inf, F32), zeros8, zeros8)
    rt_scr[...] = head_products(0)
    carry = lax.fori_loop(0, i, score_body, init)
    s = weighted_relu_sum()
    adm = (krow >> 6) <= (qcol >> 6)
    s_adm = jnp.where(adm, s, -jnp.inf)
    s_scr[i] = s_adm
    s_scr[i + 1] = jnp.full((tk, tq), -jnp.inf, F32)
    rmax, rmin, cpos, cnn = tile_stats(s_adm, jnp.where(adm, s, jnp.inf), carry)
    rmax = spread(jnp.max(rmax, axis=0, keepdims=True))
    rmin = spread(jnp.min(rmin, axis=0, keepdims=True))
    cpos = spread(jnp.sum(cpos, axis=0, keepdims=True))
    cnn = spread(jnp.sum(cnn, axis=0, keepdims=True))

    def count(preds):
        def one(j, accs):
            t3 = s_scr[j].reshape(grp, SUBLANES, tq)
            return tuple(a + tree(jnp.where(p(t3), 1.0, 0.0), jnp.sum) for a, p in zip(accs, preds))

        def group(g, accs):
            for u in range(COUNT_UNROLL):
                accs = one(g * COUNT_UNROLL + u, accs)
            return accs

        groups = (i + 1) >> COUNT_UNROLL.bit_length() - 1
        accs = lax.fori_loop(0, groups, group, tuple(jnp.zeros((SUBLANES, tq), F32) for _ in preds))
        accs = lax.fori_loop(groups * COUNT_UNROLL, i + 1, one, accs)
        return [spread(jnp.sum(a, axis=0, keepdims=True)) for a in accs]

    n_adm = (((qcol8 >> 6) + 1 + i * (tq // CHUNK)) * CHUNK).astype(F32)
    search = n_adm > kf
    positive = cpos >= kf
    negative = cnn < kf
    lo = jnp.where(positive, TINY, jnp.where(negative, rmin, 0.0))
    clo = jnp.where(positive, cpos, jnp.where(negative, n_adm, cnn))
    st_scr[0] = jnp.where(search, lo, rmin)
    st_scr[1] = jnp.where(negative, -TINY, rmax + (jnp.abs(rmax) * (2.0 ** -10) + TINY))
    st_scr[2] = jnp.where(search, clo, n_adm)
    zero_thr = jnp.logical_and(jnp.logical_not(positive), jnp.logical_not(negative))
    st_scr[3] = jnp.where(jnp.logical_or(jnp.logical_not(search), jnp.logical_or(zero_thr, clo == kf)), 1.0, 0.0)

    def bis_cond(carry):
        it, pending = carry
        return jnp.logical_and(pending > 0.0, it < BISECT_CAP)

    def bis_step(it):
        lo, hi, clo, done = st_scr[0], st_scr[1], st_scr[2], st_scr[3]
        a = lax.bitcast_convert_type(jnp.abs(lo), jnp.int32)
        b = lax.bitcast_convert_type(jnp.abs(hi), jnp.int32)
        geo = lax.bitcast_convert_type(a + ((b - a) >> 1), F32)
        geo = jnp.where(hi > 0.0, geo, -geo)
        mid = jnp.where(it < BISECT_ARITH, lo + (hi - lo) * 0.5, geo)
        stuck = jnp.logical_or(mid <= lo, mid >= hi)
        cnt, = count([lambda t: t >= mid[None]])
        live = jnp.logical_and(done == 0.0, jnp.logical_not(stuck))
        up = jnp.logical_and(live, cnt >= kf)
        dn = jnp.logical_and(live, cnt < kf)
        lo = jnp.where(up, mid, lo)
        clo = jnp.where(up, cnt, clo)
        hi = jnp.where(dn, mid, hi)
        done = jnp.where(jnp.logical_or(stuck, clo == kf), 1.0, done)
        st_scr[0], st_scr[1], st_scr[2], st_scr[3] = lo, hi, clo, done
        return done

    def bis_body(carry):
        it, _ = carry
        bis_step(it)
        done = bis_step(it + 1)
        return it + 2, jnp.max(1.0 - done)

    lax.while_loop(bis_cond, bis_body, (jnp.int32(0), jnp.max(1.0 - st_scr[3])))
    thr = st_scr[0]
    clo = st_scr[2]

    @pl.when(jnp.max(clo) > kf)
    def _():
        cgt, = count([lambda t: t > thr[None]])
        need = kf - cgt
        tied_value = jnp.where(clo > kf, thr, jnp.nan)
        lower = (lax.broadcasted_iota(jnp.int32, (tk, tk), 1)
                 <= lax.broadcasted_iota(jnp.int32, (tk, tk), 0))
        prefix_matrix = jnp.where(lower, 1.0, 0.0).astype(BF16)

        def drop_pair(pair, seen):
            for u in range(2):
                j = 2 * pair + u
                t3 = s_scr[j].reshape(grp, SUBLANES, tq)
                tied = t3 == tied_value[None]
                ones = jnp.where(tied, 1.0, 0.0).reshape(tk, tq).astype(BF16)
                prefix = _dot(prefix_matrix, ones)
                rank = prefix.reshape(grp, SUBLANES, tq) + seen[None]
                kill = jnp.logical_and(tied, rank > need[None])
                s_scr[j] = jnp.where(kill, -jnp.inf, t3).reshape(tk, tq)
                seen = seen + spread(prefix[tk - 1:tk, :])
            return seen

        lax.fori_loop(0, (i + 2) >> 1, drop_pair, jnp.zeros((SUBLANES, tq), F32))

    hrow = lax.broadcasted_iota(jnp.int32, (LANES, tq), 0)
    for h in range(DSA_HEADS):
        slab = qt_ref[0, (h // 2) * LANES:(h // 2 + 1) * LANES, :]
        mine = (hrow >= DSA_HEAD_DIM) if (h % 2) else (hrow < DSA_HEAD_DIM)
        qm_scr[h] = jnp.where(mine, slab, jnp.zeros_like(slab))
        m_scr[h] = jnp.full((SUBLANES, tq), NEG_BIG, F32)
        acc_scr[h] = jnp.zeros((ACC_ROWS, tq), F32)
    ones_rows = jnp.ones((ACC_ROWS - LANES, tk), BF16)

    def slab(h):
        return slice((h // 2) * LANES, (h // 2 + 1) * LANES)

    def tile_operands(j):
        t3 = s_scr[j].reshape(grp, SUBLANES, tq)
        bias = jnp.where(t3 >= thr[None], 0.0, NEG_BIG).reshape(tk, tq)
        return k_ref[0, pl.ds(pl.multiple_of(j * tk, tk), tk), :], bias

    def masked_scores(keys, bias, h):
        return _dot(keys[:, slab(h)], qm_scr[h]) + bias

    keys0, bias0 = tile_operands(0)
    for h in range(DSA_HEADS):
        sn_scr[h] = masked_scores(keys0, bias0, h)

    def att_body(j, _):
        keys_n, bias_n = tile_operands(jnp.minimum(j + 1, i))
        for h in range(DSA_HEADS):
            s3 = sn_scr[h].reshape(grp, SUBLANES, tq)
            m_prev = m_scr[h]
            m_new = jnp.maximum(m_prev, spread(jnp.max(tree(s3, jnp.max), axis=0, keepdims=True)))
            p = jnp.exp2((s3 - m_new[None]).reshape(tk, tq).astype(BF16))
            corr = jnp.exp2(m_prev - m_new)
            pv = _dot(jnp.concatenate([vt_ref[0, j, slab(h), :], ones_rows], axis=0), p)
            acc = acc_scr[h].reshape(ACC_ROWS // SUBLANES, SUBLANES, tq) * corr[None]
            acc_scr[h] = acc.reshape(ACC_ROWS, tq) + pv
            m_scr[h] = m_new
            sn_scr[h] = masked_scores(keys_n, bias_n, h)
        return 0

    lax.fori_loop(0, i + 1, att_body, 0)

    def normalised(h):
        inv = 1.0 / acc_scr[h, LANES:LANES + SUBLANES, :]
        return (acc_scr[h, 0:LANES, :].reshape(LANES // SUBLANES, SUBLANES, tq) * inv[None]).reshape(LANES, tq)

    for sp in range(DSA_HEADS // 2):
        pair = jnp.where(hrow < DSA_HEAD_DIM, normalised(2 * sp), normalised(2 * sp + 1))
        o_ref[0, sp * LANES:(sp + 1) * LANES, :] = pair.astype(o_ref.dtype)


def _dsa(iqt, ik, iwt, qt, k, vt):
    B, T, _ = k.shape
    tq = DSA_TILE
    nq = T // tq
    topk = min(DSA_TOPK_MAX, T // 4)
    kern = functools.partial(_dsa_kernel, topk=topk)
    return pl.pallas_call(
        kern,
        grid=(B, nq),
        in_specs=[
            pl.BlockSpec((1, 1, IDX_DIM, IDX_HEADS * tq), lambda b, i: (b, i, 0, 0)),
            pl.BlockSpec((1, T, IDX_DIM), lambda b, i: (b, 0, 0)),
            pl.BlockSpec((1, IDX_HEADS, tq), lambda b, i: (b, 0, i)),
            pl.BlockSpec((1, DSA_WIDTH, tq), lambda b, i: (b, 0, i)),
            pl.BlockSpec((1, T, DSA_WIDTH), lambda b, i: (b, 0, 0)),
            pl.BlockSpec((1, nq, DSA_WIDTH, tq), lambda b, i: (b, 0, 0, 0)),
        ],
        out_specs=pl.BlockSpec((1, DSA_WIDTH, tq), lambda b, i: (b, 0, i)),
        out_shape=jax.ShapeDtypeStruct((B, DSA_WIDTH, T), BF16),
        scratch_shapes=[
            pltpu.VMEM((nq + 1, tq, tq), F32),
            pltpu.VMEM((DSA_HEADS, LANES, tq), BF16),
            pltpu.VMEM((4, SUBLANES, tq), F32),
            pltpu.VMEM((DSA_HEADS, SUBLANES, tq), F32),
            pltpu.VMEM((DSA_HEADS, ACC_ROWS, tq), F32),
            pltpu.VMEM((DSA_HEADS, tq, tq), F32),
            pltpu.VMEM((tq, IDX_HEADS * tq), F32),
        ],
        compiler_params=_params(("parallel", "arbitrary")),
    )(iqt, ik, iwt, qt, k, vt)


def _gla_kernel(q_ref, k_ref, v_ref, r_ref, la_ref, g_ref, o_ref, s_scr):
    rows = GLA_BLOCK
    nch = rows // CHUNK
    npair = GLA_HEADS // 2
    nsub = q_ref.shape[1] // rows

    @pl.when(pl.program_id(1) == 0)
    def _():
        s_scr[...] = jnp.zeros_like(s_scr)

    ri = lax.broadcasted_iota(jnp.int32, (rows, rows), 0)
    ci = lax.broadcasted_iota(jnp.int32, (rows, rows), 1)
    causal = jnp.logical_and((ri >> 6) == (ci >> 6), ci <= ri)
    tri = jnp.where(causal, 1.0, 0.0).astype(BF16)
    lane = lax.broadcasted_iota(jnp.int32, (rows, LANES), 1)
    eye = lax.broadcasted_iota(jnp.int32, (LANES, LANES), 0) == lax.broadcasted_iota(jnp.int32, (LANES, LANES), 1)
    top_rows = lax.broadcasted_iota(jnp.int32, (LANES, GLA_DV), 0) < GLA_DK

    def decayed(r0):
        la = la_ref[0, r0:r0 + rows, :]
        la_hi = la.astype(BF16)
        rem = la - la_hi.astype(F32)
        la_mid = rem.astype(BF16)
        la_lo = (rem - la_mid.astype(F32)).astype(BF16)
        b = _dot(tri, la_hi) + _dot(tri, la_mid) + _dot(tri, la_lo)
        b_last = jnp.concatenate(
            [jnp.broadcast_to(b[(c + 1) * CHUNK - 1:(c + 1) * CHUNK, :], (CHUNK, b.shape[1])) for c in range(nch)],
            axis=0)
        q = q_ref[0, r0:r0 + rows, :]
        k = k_ref[0, r0:r0 + rows, :]
        return q * jnp.exp(b), (k * jnp.exp(-b)).astype(BF16), (k * jnp.exp(b_last - b)).astype(BF16), jnp.exp(b_last)

    subs = [decayed(sb * rows) for sb in range(nsub)]

    for pr in range(npair):
        sl = slice(pr * LANES, (pr + 1) * LANES)
        heads = (2 * pr, 2 * pr + 1)
        state = s_scr[pr]
        for sb in range(nsub):
            r0 = sb * rows
            qg, kg, kd, decay_rows = subs[sb]
            qg_p = qg[:, sl]
            kg_p = kg[:, sl]
            kd_p = kd[:, sl]
            qg_h = [jnp.where(lane < GLA_DK, qg_p, 0.0).astype(BF16), jnp.where(lane >= GLA_DK, qg_p, 0.0).astype(BF16)]
            v_h = [v_ref[0, r0:r0 + rows, h * GLA_DV:(h + 1) * GLA_DV].astype(BF16) for h in heads]
            o_h = []
            for t in range(2):
                a = jnp.where(causal, _dot_nt(qg_h[t], kg_p), 0.0)
                o_h.append(_dot(a.astype(BF16), v_h[t]))
            inter = [[], []]
            for c in range(nch):
                rs = slice(c * CHUNK, (c + 1) * CHUNK)
                state_b = state.astype(BF16)
                for t in range(2):
                    inter[t].append(_dot(qg_h[t][rs], state_b))
                kv = jnp.where(top_rows, _dot_tn(kd_p[rs], v_h[0][rs]), _dot_tn(kd_p[rs], v_h[1][rs]))
                drow = jnp.broadcast_to(decay_rows[c * CHUNK:c * CHUNK + 1, sl], (LANES, LANES))
                dcol = jnp.sum(jnp.where(eye, drow, 0.0), axis=1, keepdims=True)
                state = dcol * state + kv
            for t in range(2):
                h = heads[t]
                o = o_h[t] + jnp.concatenate(inter[t], axis=0)
                ms = jnp.mean(o * o, axis=1, keepdims=True)
                on = o * lax.rsqrt(ms + LN_EPS) * g_ref[...]
                r = r_ref[0, r0:r0 + rows, h * GLA_DV:(h + 1) * GLA_DV]
                o_ref[0, r0:r0 + rows, h * GLA_DV:(h + 1) * GLA_DV] = (on * (r / (1.0 + jnp.exp(-r)))).astype(o_ref.dtype)
        s_scr[pr] = state


def _gla(bq, bk, bv, br, la, g):
    B, T, _ = bq.shape
    rows = GLA_STEP
    blk = lambda w: pl.BlockSpec((1, rows, w), lambda b, i: (b, i, 0))
    return pl.pallas_call(
        _gla_kernel,
        grid=(B, T // rows),
        in_specs=[blk(256), blk(256), blk(512), blk(512), blk(256), pl.BlockSpec((1, GLA_DV), lambda b, i: (0, 0))],
        out_specs=blk(GLA_WIDTH),
        out_shape=jax.ShapeDtypeStruct((B, T, GLA_WIDTH), BF16),
        scratch_shapes=[pltpu.VMEM((GLA_HEADS // 2, 2 * GLA_DK, GLA_DV), F32)],
        compiler_params=_params(("parallel", "arbitrary")),
    )(bq, bk, bv, br, la, g)


def _layer_norm(x, g, b):
    mu = jnp.mean(x, axis=1, keepdims=True)
    xc = x - mu
    var = jnp.mean(xc * xc, axis=1, keepdims=True)
    return xc * lax.rsqrt(var + LN_EPS) * g + b


def _out_proj_kernel(x_ref, ya_ref, yb_ref, wo_ref, g_ref, b_ref, wr_hi_ref, wr_lo_ref, br_ref, h_ref, route_ref):
    tm = x_ref.shape[0]
    mix = _dot(ya_ref[...], wo_ref[0:DSA_WIDTH, :]) + _dot(yb_ref[...], wo_ref[DSA_WIDTH:D_MODEL, :])
    h = _layer_norm(DEEPNORM_ALPHA * x_ref[...] + mix, g_ref[...], b_ref[...])
    h_ref[...] = h

    h_hi = h.astype(BF16)
    h_lo = (h - h_hi.astype(F32)).astype(BF16)
    logits = _dot(h_hi, wr_hi_ref[...]) + _dot(h_lo, wr_hi_ref[...]) + _dot(h_hi, wr_lo_ref[...]) + br_ref[...]

    lane = lax.broadcasted_iota(jnp.int32, (tm, LANES), 1)
    lanef = lane.astype(F32)
    gl = jnp.where(lane < N_GROUPS, logits, -jnp.inf)
    gmax = jnp.max(gl, axis=1, keepdims=True)
    gsel = jnp.min(jnp.where(gl == gmax, lanef, 1e9), axis=1, keepdims=True)
    pg = 1.0 / jnp.sum(jnp.exp(gl - gmax), axis=1, keepdims=True)
    egrp = ((lane - N_GROUPS) >> 3).astype(F32)
    in_grp = jnp.logical_and(jnp.logical_and(lane >= N_GROUPS, lane < N_GROUPS + N_EXPERTS), egrp == gsel)
    el = jnp.where(in_grp, logits, -jnp.inf)
    t1 = jnp.max(el, axis=1, keepdims=True)
    i1 = jnp.min(jnp.where(el == t1, lanef, 1e9), axis=1, keepdims=True)
    el2 = jnp.where(lanef == i1, -jnp.inf, el)
    t2 = jnp.max(el2, axis=1, keepdims=True)
    i2 = jnp.min(jnp.where(el2 == t2, lanef, 1e9), axis=1, keepdims=True)
    e21 = jnp.exp(t2 - t1)
    g1 = pg / (1.0 + e21)
    g2 = pg * e21 / (1.0 + e21)
    route = jnp.where(lane == 0, i1 - N_GROUPS, 0.0)
    route = jnp.where(lane == 1, i2 - N_GROUPS, route)
    route = jnp.where(lane == 2, g1, route)
    route = jnp.where(lane == 3, g2, route)
    route_ref[...] = route


def _out_proj(x2, ya, yb, wo, g1, b1, wr_hi, wr_lo, br):
    N = x2.shape[0]
    tm = TOK_TILE
    row = lambda i: (i, 0)
    const = lambda i: (0, 0)
    return pl.pallas_call(
        _out_proj_kernel,
        grid=(N // tm,),
        in_specs=[
            pl.BlockSpec((tm, D_MODEL), row), pl.BlockSpec((tm, DSA_WIDTH), row), pl.BlockSpec((tm, GLA_WIDTH), row),
            pl.BlockSpec(wo.shape, const), pl.BlockSpec(g1.shape, const), pl.BlockSpec(b1.shape, const),
            pl.BlockSpec(wr_hi.shape, const), pl.BlockSpec(wr_lo.shape, const), pl.BlockSpec(br.shape, const),
        ],
        out_specs=[pl.BlockSpec((tm, D_MODEL), row), pl.BlockSpec((tm, LANES), row)],
        out_shape=[jax.ShapeDtypeStruct((N, D_MODEL), F32), jax.ShapeDtypeStruct((N, LANES), F32)],
        compiler_params=_params(("parallel",)),
    )(x2, ya, yb, wo, g1, b1, wr_hi, wr_lo, br)


def _rank_kernel(route_ref, rank_ref, cnt_ref, carry_scr):
    tm = route_ref.shape[0]

    @pl.when(pl.program_id(0) == 0)
    def _():
        carry_scr[...] = jnp.zeros_like(carry_scr)

    route = route_ref[...]
    lanef = lax.broadcasted_iota(jnp.int32, (tm, LANES), 1).astype(F32)
    e1 = route[:, 0:1]
    e2 = route[:, 1:2]
    hit1 = lanef == e1
    hit2 = lanef == e2
    onehot = jnp.where(jnp.logical_or(hit1, hit2), 1.0, 0.0).astype(BF16)
    ri = lax.broadcasted_iota(jnp.int32, (tm, tm), 0)
    ci = lax.broadcasted_iota(jnp.int32, (tm, tm), 1)
    before = jnp.where(ci < ri, 1.0, 0.0).astype(BF16)
    prefix = _dot(before, onehot) + carry_scr[0:1, :]
    r1 = jnp.sum(jnp.where(hit1, prefix, 0.0), axis=1, keepdims=True)
    r2 = jnp.sum(jnp.where(hit2, prefix, 0.0), axis=1, keepdims=True)
    rank_ref[...] = jnp.where(lanef == 0.0, r1, jnp.where(lanef == 1.0, r2, 0.0))
    total = _dot(jnp.ones((8, tm), BF16), onehot)
    carry_scr[...] = carry_scr[...] + total
    cnt_ref[...] = carry_scr[...]


def _rank(route):
    N = route.shape[0]
    tm = RANK_TILE
    return pl.pallas_call(
        _rank_kernel,
        grid=(N // tm,),
        in_specs=[pl.BlockSpec((tm, LANES), lambda i: (i, 0))],
        out_specs=[pl.BlockSpec((tm, LANES), lambda i: (i, 0)), pl.BlockSpec((8, LANES), lambda i: (0, 0))],
        out_shape=[jax.ShapeDtypeStruct((N, LANES), F32), jax.ShapeDtypeStruct((8, LANES), F32)],
        scratch_shapes=[pltpu.VMEM((8, LANES), F32)],
        compiler_params=_params(("arbitrary",)),
    )(route)


def _dispatch_kernel(pos_ref, h_ref, xs_in_ref, xs_ref, hbuf, pbuf, load_sem, row_sem):
    del xs_in_ref
    tm = DISPATCH_TILE
    nbuf = hbuf.shape[0]
    i = pl.program_id(0)
    last = pl.num_programs(0) - 1

    def tile_load(step):
        return pltpu.make_async_copy(h_ref.at[pl.ds(step * tm, tm), :], hbuf.at[step % nbuf], load_sem.at[step % nbuf])

    def row_copy(step, r, slot):
        p = pos_ref[2 * (step * tm + r) + slot]
        return pltpu.make_async_copy(pbuf.at[step % nbuf, pl.ds(r, 1), :], xs_ref.at[pl.ds(p, 1), :],
                                     row_sem.at[step % nbuf])

    def for_rows(fn):
        def body(r, _):
            fn(r, 0)
            fn(r, 1)
            return 0
        lax.fori_loop(0, tm, body, 0, unroll=DMA_UNROLL)

    @pl.when(i == 0)
    def _():
        tile_load(0).start()

    @pl.when(i < last)
    def _():
        tile_load(i + 1).start()

    tile_load(i).wait()
    pbuf[i % nbuf] = _pack_halves(hbuf[i % nbuf])
    for_rows(lambda r, slot: row_copy(i, r, slot).start())

    @pl.when(i > 0)
    def _():
        for_rows(lambda r, slot: row_copy(i - 1, r, slot).wait())

    @pl.when(i == last)
    def _():
        for_rows(lambda r, slot: row_copy(i, r, slot).wait())


def _dispatch(pos_flat, h, xs_init):
    N = h.shape[0]
    grid_spec = pltpu.PrefetchScalarGridSpec(
        num_scalar_prefetch=1,
        grid=(N // DISPATCH_TILE,),
        in_specs=[pl.BlockSpec(memory_space=pl.ANY), pl.BlockSpec(memory_space=pl.ANY)],
        out_specs=pl.BlockSpec(memory_space=pl.ANY),
        scratch_shapes=[pltpu.VMEM((3, DISPATCH_TILE, D_MODEL), F32),
                        pltpu.VMEM((3, DISPATCH_TILE, D_MODEL // 2), jnp.uint32),
                        pltpu.SemaphoreType.DMA((3,)), pltpu.SemaphoreType.DMA((3,))],
    )
    return pl.pallas_call(
        _dispatch_kernel,
        grid_spec=grid_spec,
        out_shape=jax.ShapeDtypeStruct(xs_init.shape, xs_init.dtype),
        input_output_aliases={2: 0},
        compiler_params=_params(("arbitrary",)),
    )(pos_flat, h, xs_init)


def _ffn_kernel(te_ref, na_ref, x_ref, wi_ref, wo_ref, y_ref, wi_bf, wo_bf):
    i = pl.program_id(0)
    active = i < na_ref[0]
    fresh = jnp.logical_or(i == 0, te_ref[i] != te_ref[jnp.maximum(i - 1, 0)])

    @pl.when(jnp.logical_and(active, fresh))
    def _():
        wi_bf[...] = wi_ref[0].astype(BF16)
        wo_bf[...] = wo_ref[0].astype(BF16)

    @pl.when(active)
    def _():
        x_left, x_right = _unpack_halves(x_ref[...])
        half = D_MODEL // 2
        hid = (_dot(x_left.astype(BF16), wi_bf[0:half, :])
               + _dot(x_right.astype(BF16), wi_bf[half:D_MODEL, :]))
        hg = hid[:, :D_EXPERT]
        hu = hid[:, D_EXPERT:]
        act = (hg / (1.0 + jnp.exp(-hg))) * hu
        y_ref[...] = _pack_halves(_dot(act.astype(BF16), wo_bf[...]))

    @pl.when(jnp.logical_not(active))
    def _():
        y_ref[...] = jnp.zeros_like(y_ref)


def _ffn(tile_expert, n_active, xs, w_e_in, w_e_out):
    R = xs.shape[0]
    tm = MOE_TILE

    def live(i, te, na):
        return jnp.minimum(i, na[0] - 1)

    grid_spec = pltpu.PrefetchScalarGridSpec(
        num_scalar_prefetch=2,
        grid=(R // tm,),
        in_specs=[
            pl.BlockSpec((tm, D_MODEL // 2), lambda i, te, na: (live(i, te, na), 0)),
            pl.BlockSpec((1, D_MODEL, 2 * D_EXPERT), lambda i, te, na: (te[live(i, te, na)], 0, 0)),
            pl.BlockSpec((1, D_EXPERT, D_MODEL), lambda i, te, na: (te[live(i, te, na)], 0, 0)),
        ],
        out_specs=pl.BlockSpec((tm, D_MODEL // 2), lambda i, te, na: (i, 0)),
        scratch_shapes=[pltpu.VMEM((D_MODEL, 2 * D_EXPERT), BF16), pltpu.VMEM((D_EXPERT, D_MODEL), BF16)],
    )
    return pl.pallas_call(
        _ffn_kernel,
        grid_spec=grid_spec,
        out_shape=jax.ShapeDtypeStruct((R, D_MODEL // 2), jnp.uint32),
        compiler_params=_params(("arbitrary",)),
    )(tile_expert, n_active, xs, w_e_in, w_e_out)


def _combine_kernel(pos_ref, h_ref, route_ref, g_ref, b_ref, ys_ref, o_ref, buf, sem):
    tm = h_ref.shape[0]
    i = pl.program_id(0)
    cur = i % 2

    def row_copy(step, r, slot):
        p = pos_ref[2 * (step * tm + r) + slot]
        half = step % 2
        return pltpu.make_async_copy(ys_ref.at[pl.ds(p, 1), :], buf.at[half, slot, pl.ds(r, 1), :], sem.at[half])

    def for_rows(fn):
        def body(r, _):
            fn(r, 0)
            fn(r, 1)
            return 0
        lax.fori_loop(0, tm, body, 0, unroll=DMA_UNROLL)

    @pl.when(i == 0)
    def _():
        for_rows(lambda r, slot: row_copy(0, r, slot).start())

    @pl.when(i + 1 < pl.num_programs(0))
    def _():
        for_rows(lambda r, slot: row_copy(i + 1, r, slot).start())

    for_rows(lambda r, slot: row_copy(i, r, slot).wait())
    route = route_ref[...]
    l0, r0 = _unpack_halves(buf[cur, 0])
    l1, r1 = _unpack_halves(buf[cur, 1])
    g0, g1 = route[:, 2:3], route[:, 3:4]
    ffn = jnp.concatenate([l0 * g0 + l1 * g1, r0 * g0 + r1 * g1], axis=1)
    o_ref[...] = _layer_norm(DEEPNORM_ALPHA * h_ref[...] + ffn, g_ref[...], b_ref[...])


def _combine(pos_flat, h, route, g2, b2, ys):
    N = h.shape[0]
    tm = TOK_TILE
    row = lambda i, pos: (i, 0)
    const = lambda i, pos: (0, 0)
    grid_spec = pltpu.PrefetchScalarGridSpec(
        num_scalar_prefetch=1,
        grid=(N // tm,),
        in_specs=[
            pl.BlockSpec((tm, D_MODEL), row), pl.BlockSpec((tm, LANES), row),
            pl.BlockSpec(g2.shape, const), pl.BlockSpec(b2.shape, const),
            pl.BlockSpec(memory_space=pl.ANY),
        ],
        out_specs=pl.BlockSpec((tm, D_MODEL), row),
        scratch_shapes=[pltpu.VMEM((2, 2, tm, D_MODEL // 2), jnp.uint32), pltpu.SemaphoreType.DMA((2,))],
    )
    return pl.pallas_call(
        _combine_kernel,
        grid_spec=grid_spec,
        out_shape=jax.ShapeDtypeStruct((N, D_MODEL), F32),
        compiler_params=_params(("arbitrary",)),
    )(pos_flat, h, route, g2, b2, ys)


def _rope_tables(T, dim):
    half = dim // 2
    inv = 1.0 / (ROPE_THETA ** (jnp.arange(half, dtype=F32) / half))
    ang = jnp.arange(T).astype(F32)[:, None] * inv[None, :]
    cos = jnp.cos(ang)
    sin = jnp.sin(ang)
    reps = LANES // dim
    cos_t = jnp.tile(jnp.concatenate([cos, cos], axis=1), (1, reps))
    sin_t = jnp.tile(jnp.concatenate([-sin, sin], axis=1), (1, reps))
    return cos_t, sin_t


def _pad_cols(w, width):
    return jnp.pad(w, ((0, 0), (0, width - w.shape[1])))


def _layer(x, w_in, w_gla_gate, b_gla_gate, g_gla_norm, w_out, ln1_g, ln1_b,
           w_gr, b_gr, w_er, b_er, w_e_in, w_e_out, ln2_g, ln2_b):
    B, T, D = x.shape
    N = B * T
    assert D == D_MODEL and T % DSA_TILE == 0 and N % DISPATCH_TILE == 0 and DISPATCH_TILE % RANK_TILE == 0
    x2 = x.reshape(N, D)

    sizes = (512, 512, 512, 256, 32, 8, 256, 256, 512, 512, 16)
    offs = np.concatenate([[0], np.cumsum(sizes)])
    col = lambda k: w_in[:, offs[k]:offs[k + 1]]
    wa = jnp.concatenate([col(0) * (DSA_HEAD_DIM ** -0.5 * LOG2E), col(1), col(2)], axis=1).astype(BF16)
    wi = _pad_cols(jnp.concatenate([col(3), col(4), col(5) * IDX_SCALE], axis=1), 3 * LANES).astype(BF16)
    wb = jnp.concatenate([col(6) * (GLA_DK ** -0.5), col(7), col(8), col(9)], axis=1).astype(BF16)
    wg = _pad_cols(col(10), LANES).astype(BF16)
    wgate = jnp.pad(w_gla_gate, ((0, LANES - GLA_GATE_RANK), (0, 0))).astype(BF16)
    bgate = b_gla_gate.reshape(1, -1)
    cosa, sina = _rope_tables(T, DSA_HEAD_DIM)
    cosi, sini = _rope_tables(T, IDX_DIM)

    q, k, v, iq, ikw, bq, bk, bv, br, la = _in_proj(x2, wa, wi, wb, wg, wgate, bgate, cosa, sina, cosi, sini, T)

    nq = T // DSA_TILE
    iqt = iq.reshape(B, nq, DSA_TILE, IDX_HEADS, IDX_DIM).transpose(0, 1, 4, 3, 2)
    iqt = iqt.reshape(B, nq, IDX_DIM, IDX_HEADS * DSA_TILE)
    ik = ikw[:, :IDX_DIM].astype(BF16).reshape(B, T, IDX_DIM)
    iwt = ikw[:, IDX_DIM:IDX_DIM + IDX_HEADS].reshape(B, T, IDX_HEADS).transpose(0, 2, 1)
    qt = q.reshape(B, T, DSA_WIDTH).transpose(0, 2, 1)
    vt = v.reshape(B, nq, DSA_TILE, DSA_WIDTH).transpose(0, 1, 3, 2)
    ya = _dsa(iqt, ik, iwt, qt, k.reshape(B, T, DSA_WIDTH), vt).transpose(0, 2, 1)

    r3 = lambda a: a.reshape(B, T, a.shape[-1])
    yb = _gla(r3(bq), r3(bk), r3(bv), r3(br), r3(la), g_gla_norm.reshape(1, GLA_DV))

    wr = _pad_cols(jnp.concatenate([w_gr, w_er], axis=1), LANES)
    wr_hi = wr.astype(BF16)
    wr_lo = (wr - wr_hi.astype(F32)).astype(BF16)
    brt = _pad_cols(jnp.concatenate([b_gr, b_er]).reshape(1, -1), LANES)
    h, route = _out_proj(x2, ya.reshape(N, DSA_WIDTH), yb.reshape(N, GLA_WIDTH), w_out.astype(BF16),
                         ln1_g.reshape(1, D), ln1_b.reshape(1, D), wr_hi, wr_lo, brt)

    rank, cnt = _rank(route)
    counts = cnt[0, :N_EXPERTS].astype(jnp.int32)
    padded = ((counts + MOE_TILE - 1) // MOE_TILE) * MOE_TILE
    ends = jnp.cumsum(padded)
    starts = ends - padded
    eid = route[:, 0:2].astype(jnp.int32)
    pos = (starts[eid] + rank[:, 0:2].astype(jnp.int32)).reshape(-1)
    n_rows = 2 * N + N_EXPERTS * MOE_TILE
    n_tiles = n_rows // MOE_TILE
    tile_start = jnp.arange(n_tiles, dtype=jnp.int32) * MOE_TILE
    tile_expert = jnp.minimum(jnp.sum(tile_start[:, None] >= ends[None, :], axis=1), N_EXPERTS - 1).astype(jnp.int32)
    n_active = (ends[-1] // MOE_TILE).astype(jnp.int32).reshape(1)

    xs = _dispatch(pos, h, jnp.zeros((n_rows, D // 2), jnp.uint32))
    ys = _ffn(tile_expert, n_active, xs, w_e_in, w_e_out)
    out = _combine(pos, h, route, ln2_g.reshape(1, D), ln2_b.reshape(1, D), ys)
    return out.reshape(B, T, D)


def kernel(x, w_in, w_gla_gate, b_gla_gate, g_gla_norm, w_out, ln1_g, ln1_b, w_group_router, b_group_router,
           w_expert_router, b_expert_router, w_expert_in, w_expert_out, ln2_g, ln2_b):
    h = x
    for l in range(w_in.shape[0]):
        h = _layer(h, w_in[l], w_gla_gate[l], b_gla_gate[l], g_gla_norm[l], w_out[l], ln1_g[l], ln1_b[l],
                   w_group_router[l], b_group_router[l], w_expert_router[l], b_expert_router[l],
                   w_expert_in[l], w_expert_out[l], ln2_g[l], ln2_b[l])
    return h
```

```python
import functools

import numpy as np
import jax
import jax.numpy as jnp
from jax import lax
from jax.experimental import pallas as pl
from jax.experimental.pallas import tpu as pltpu

F32 = jnp.float32
BF16 = jnp.bfloat16

D_MODEL = 1024
CHUNK = 64
ROPE_THETA = 10000.0
LN_EPS = 1e-5
DSA_WIDTH = 512
DSA_HEAD_DIM = 64
DSA_HEADS = 8
IDX_HEADS = 8
IDX_DIM = 32
IDX_SCALE = (IDX_HEADS * IDX_DIM) ** -0.5
DSA_TOPK_MAX = 256
GLA_WIDTH = 512
GLA_HEADS = 4
GLA_DV = 128
GLA_DK = 64
GLA_GATE_RANK = 16
GLA_TAU = 16.0
N_GROUPS = 4
EXPERTS_PER_GROUP = 8
N_EXPERTS = 32
D_EXPERT = 512
DEEPNORM_ALPHA = 2.0 ** 0.25

LANES = 128
SUBLANES = 8
TINY = 2.0 ** -126
NEG_BIG = -1e30
ACC_ROWS = LANES + 16
VMEM_LIMIT = 56 * 1024 * 1024

TOK_TILE = 512
DSA_TILE = 256
GLA_BLOCK = 256
GLA_STEP = 1024
MOE_TILE = 512
RANK_TILE = 512
DISPATCH_TILE = 512
DMA_UNROLL = 8
LOG2E = 1.4426950408889634
COUNT_UNROLL = 4
BISECT_ARITH = 40
BISECT_CAP = 80


def _dot(a, b):
    return jnp.dot(a, b, preferred_element_type=F32)


def _dot_nt(a, b):
    return lax.dot_general(a, b, (((1,), (1,)), ((), ())), preferred_element_type=F32)


def _dot_tn(a, b):
    return lax.dot_general(a, b, (((0,), (0,)), ((), ())), preferred_element_type=F32)


def _pack_halves(x):
    w = x.shape[1] // 2
    hi = lax.bitcast_convert_type(x[:, :w].astype(BF16).astype(F32), jnp.uint32)
    lo = lax.bitcast_convert_type(x[:, w:].astype(BF16).astype(F32), jnp.uint32)
    return (hi & jnp.uint32(0xFFFF0000)) | (lo >> 16)


def _unpack_halves(p):
    left = lax.bitcast_convert_type(p & jnp.uint32(0xFFFF0000), F32)
    right = lax.bitcast_convert_type(p << 16, F32)
    return left, right


def _params(sem):
    return pltpu.CompilerParams(dimension_semantics=sem, vmem_limit_bytes=VMEM_LIMIT)


def _rope_slab(slab, cos, sin, first_half, half):
    swapped = jnp.where(first_half, pltpu.roll(slab, LANES - half, 1), pltpu.roll(slab, half, 1))
    return slab * cos + swapped * sin


def _in_proj_kernel(x_ref, wa_ref, wi_ref, wb_ref, wg_ref, wgate_ref, bgate_ref,
                    cosa_ref, sina_ref, cosi_ref, sini_ref,
                    q_ref, k_ref, v_ref, iq_ref, ikw_ref, bq_ref, bk_ref, bv_ref, br_ref, la_ref):
    tm = x_ref.shape[0]
    xb = x_ref[...].astype(BF16)
    lane = lax.broadcasted_iota(jnp.int32, (tm, LANES), 1)

    a = _dot(xb, wa_ref[...])
    cosa, sina = cosa_ref[...], sina_ref[...]
    first_a = (lane & (DSA_HEAD_DIM - 1)) < DSA_HEAD_DIM // 2
    for c in range(DSA_WIDTH // LANES):
        sl = slice(c * LANES, (c + 1) * LANES)
        q_ref[:, sl] = _rope_slab(a[:, sl], cosa, sina, first_a, DSA_HEAD_DIM // 2).astype(BF16)
        ks = slice(DSA_WIDTH + c * LANES, DSA_WIDTH + (c + 1) * LANES)
        k_ref[:, sl] = _rope_slab(a[:, ks], cosa, sina, first_a, DSA_HEAD_DIM // 2).astype(BF16)
    v_ref[...] = a[:, 2 * DSA_WIDTH:3 * DSA_WIDTH].astype(BF16)

    ii = _dot(xb, wi_ref[...])
    cosi, sini = cosi_ref[...], sini_ref[...]
    first_i = (lane & (IDX_DIM - 1)) < IDX_DIM // 2
    for c in range(2):
        sl = slice(c * LANES, (c + 1) * LANES)
        iq_ref[:, sl] = _rope_slab(ii[:, sl], cosi, sini, first_i, IDX_DIM // 2).astype(BF16)
    last = ii[:, 2 * LANES:3 * LANES]
    ikw_ref[...] = jnp.where(lane < IDX_DIM, _rope_slab(last, cosi, sini, first_i, IDX_DIM // 2), last)

    b = _dot(xb, wb_ref[...])
    bq_ref[...] = b[:, 0:256]
    bk_ref[...] = b[:, 256:512]
    bv_ref[...] = b[:, 512:1024]
    br_ref[...] = b[:, 1024:1536]

    g = _dot(xb, wg_ref[...])
    z = _dot(g.astype(BF16), wgate_ref[...]) + bgate_ref[...]
    log_sig = jnp.minimum(z, 0.0) - jnp.log(1.0 + jnp.exp(-jnp.abs(z)))
    la_ref[...] = log_sig * (1.0 / GLA_TAU)


def _in_proj(x2, wa, wi, wb, wg, wgate, bgate, cosa, sina, cosi, sini, T):
    N = x2.shape[0]
    tm = TOK_TILE
    nt = T // tm
    row = lambda i: (i, 0)
    const = lambda i: (0, 0)
    pos = lambda i: (i % nt, 0)
    outs = [
        (DSA_WIDTH, BF16), (DSA_WIDTH, BF16), (DSA_WIDTH, BF16), (IDX_HEADS * IDX_DIM, BF16), (LANES, F32),
        (256, F32), (256, F32), (512, F32), (512, F32), (256, F32),
    ]
    return pl.pallas_call(
        _in_proj_kernel,
        grid=(N // tm,),
        in_specs=[
            pl.BlockSpec((tm, D_MODEL), row),
            pl.BlockSpec(wa.shape, const), pl.BlockSpec(wi.shape, const), pl.BlockSpec(wb.shape, const),
            pl.BlockSpec(wg.shape, const), pl.BlockSpec(wgate.shape, const), pl.BlockSpec(bgate.shape, const),
            pl.BlockSpec((tm, LANES), pos), pl.BlockSpec((tm, LANES), pos),
            pl.BlockSpec((tm, LANES), pos), pl.BlockSpec((tm, LANES), pos),
        ],
        out_specs=[pl.BlockSpec((tm, w), row) for w, _ in outs],
        out_shape=[jax.ShapeDtypeStruct((N, w), dt) for w, dt in outs],
        compiler_params=_params(("parallel",)),
    )(x2, wa, wi, wb, wg, wgate, bgate, cosa, sina, cosi, sini)


def _dsa_kernel(iqt_ref, ik_ref, iwt_ref, qt_ref, k_ref, vt_ref, o_ref,
                s_scr, qm_scr, st_scr, m_scr, acc_scr, sn_scr, rt_scr, *, topk):
    tq = DSA_TILE
    tk = DSA_TILE
    grp = tk // SUBLANES
    i = pl.program_id(1)
    kf = jnp.float32(topk)

    iqt = iqt_ref[0, 0]
    iwt = iwt_ref[0]
    krow = lax.broadcasted_iota(jnp.int32, (tk, tq), 0)
    qcol = lax.broadcasted_iota(jnp.int32, (tk, tq), 1)
    qcol8 = lax.broadcasted_iota(jnp.int32, (SUBLANES, tq), 1)

    def tree(x3, op):
        q4 = grp // 4
        return op(jnp.stack([op(x3[a * q4:(a + 1) * q4], axis=0) for a in range(4)]), axis=0)

    def fold(x, op):
        return tree(x.reshape(grp, SUBLANES, tq), op)

    def spread(x):
        return jnp.broadcast_to(x, (SUBLANES, tq))

    def head_products(j):
        keys = ik_ref[0, pl.ds(pl.multiple_of(j * tk, tk), tk), :]
        return _dot(keys, iqt)

    def weighted_relu_sum():
        acc = None
        for h in range(IDX_HEADS):
            term = jnp.maximum(rt_scr[:, h * tq:(h + 1) * tq], 0.0) * iwt[h:h + 1, :]
            acc = term if acc is None else acc + term
        return acc

    def tile_stats(s_hi, s_lo, carry):
        rmax, rmin, cpos, cnn = carry
        return (jnp.maximum(rmax, fold(s_hi, jnp.max)), jnp.minimum(rmin, fold(s_lo, jnp.min)),
                cpos + fold(jnp.where(s_hi > 0.0, 1.0, 0.0), jnp.sum),
                cnn + fold(jnp.where(s_hi >= 0.0, 1.0, 0.0), jnp.sum))

    def score_body(j, carry):
        s = weighted_relu_sum()
        s_scr[j] = s
        rt_scr[...] = head_products(j + 1)
        return tile_stats(s, s, carry)

    zeros8 = jnp.zeros((SUBLANES, tq), F32)
    init = (jnp.full((SUBLANES, tq), -jnp.inf, F32), jnp.full((SUBLANES, tq), jnp.inf, F32), zeros8, zeros8)
    rt_scr[...] = head_products(0)
    carry = lax.fori_loop(0, i, score_body, init)
    s = weighted_relu_sum()
    adm = (krow >> 6) <= (qcol >> 6)
    s_adm = jnp.where(adm, s, -jnp.inf)
    s_scr[i] = s_adm
    s_scr[i + 1] = jnp.full((tk, tq), -jnp.inf, F32)
    rmax, rmin, cpos, cnn = tile_stats(s_adm, jnp.where(adm, s, jnp.inf), carry)
    rmax = spread(jnp.max(rmax, axis=0, keepdims=True))
    rmin = spread(jnp.min(rmin, axis=0, keepdims=True))
    cpos = spread(jnp.sum(cpos, axis=0, keepdims=True))
    cnn = spread(jnp.sum(cnn, axis=0, keepdims=True))

    def count(preds):
        def one(j, accs):
            t3 = s_scr[j].reshape(grp, SUBLANES, tq)
            return tuple(a + tree(jnp.where(p(t3), 1.0, 0.0), jnp.sum) for a, p in zip(accs, preds))

        def group(g, accs):
            for u in range(COUNT_UNROLL):
                accs = one(g * COUNT_UNROLL + u, accs)
            return accs

        groups = (i + 1) >> COUNT_UNROLL.bit_length() - 1
        accs = lax.fori_loop(0, groups, group, tuple(jnp.zeros((SUBLANES, tq), F32) for _ in preds))
        accs = lax.fori_loop(groups * COUNT_UNROLL, i + 1, one, accs)
        return [spread(jnp.sum(a, axis=0, keepdims=True)) for a in accs]

    n_adm = (((qcol8 >> 6) + 1 + i * (tq // CHUNK)) * CHUNK).astype(F32)
    search = n_adm > kf
    positive = cpos >= kf
    negative = cnn < kf
    lo = jnp.where(positive, TINY, jnp.where(negative, rmin, 0.0))
    clo = jnp.where(positive, cpos, jnp.where(negative, n_adm, cnn))
    st_scr[0] = jnp.where(search, lo, rmin)
    st_scr[1] = jnp.where(negative, -TINY, rmax + (jnp.abs(rmax) * (2.0 ** -10) + TINY))
    st_scr[2] = jnp.where(search, clo, n_adm)
    zero_thr = jnp.logical_and(jnp.logical_not(positive), jnp.logical_not(negative))
    st_scr[3] = jnp.where(jnp.logical_or(jnp.logical_not(search), jnp.logical_or(zero_thr, clo == kf)), 1.0, 0.0)

    def bis_cond(carry):
        it, pending = carry
        return jnp.logical_and(pending > 0.0, it < BISECT_CAP)

    def bis_step(it):
        lo, hi, clo, done = st_scr[0], st_scr[1], st_scr[2], st_scr[3]
        a = lax.bitcast_convert_type(jnp.abs(lo), jnp.int32)
        b = lax.bitcast_convert_type(jnp.abs(hi), jnp.int32)
        geo = lax.bitcast_convert_type(a + ((b - a) >> 1), F32)
        geo = jnp.where(hi > 0.0, geo, -geo)
        mid = jnp.where(it < BISECT_ARITH, lo + (hi - lo) * 0.5, geo)
        stuck = jnp.logical_or(mid <= lo, mid >= hi)
        cnt, = count([lambda t: t >= mid[None]])
        live = jnp.logical_and(done == 0.0, jnp.logical_not(stuck))
        up = jnp.logical_and(live, cnt >= kf)
        dn = jnp.logical_and(live, cnt < kf)
        lo = jnp.where(up, mid, lo)
        clo = jnp.where(up, cnt, clo)
        hi = jnp.where(dn, mid, hi)
        done = jnp.where(jnp.logical_or(stuck, clo == kf), 1.0, done)
        st_scr[0], st_scr[1], st_scr[2], st_scr[3] = lo, hi, clo, done
        return done

    def bis_body(carry):
        it, _ = carry
        bis_step(it)
        done = bis_step(it + 1)
        return it + 2, jnp.max(1.0 - done)

    lax.while_loop(bis_cond, bis_body, (jnp.int32(0), jnp.max(1.0 - st_scr[3])))
    thr = st_scr[0]
    clo = st_scr[2]

    @pl.when(jnp.max(clo) > kf)
    def _():
        cgt, = count([lambda t: t > thr[None]])
        need = kf - cgt
        tied_value = jnp.where(clo > kf, thr, jnp.nan)
        lower = (lax.broadcasted_iota(jnp.int32, (tk, tk), 1)
                 <= lax.broadcasted_iota(jnp.int32, (tk, tk), 0))
        prefix_matrix = jnp.where(lower, 1.0, 0.0).astype(BF16)

        def drop_pair(pair, seen):
            for u in range(2):
                j = 2 * pair + u
                t3 = s_scr[j].reshape(grp, SUBLANES, tq)
                tied = t3 == tied_value[None]
                ones = jnp.where(tied, 1.0, 0.0).reshape(tk, tq).astype(BF16)
                prefix = _dot(prefix_matrix, ones)
                rank = prefix.reshape(grp, SUBLANES, tq) + seen[None]
                kill = jnp.logical_and(tied, rank > need[None])
                s_scr[j] = jnp.where(kill, -jnp.inf, t3).reshape(tk, tq)
                seen = seen + spread(prefix[tk - 1:tk, :])
            return seen

        lax.fori_loop(0, (i + 2) >> 1, drop_pair, jnp.zeros((SUBLANES, tq), F32))

    hrow = lax.broadcasted_iota(jnp.int32, (LANES, tq), 0)
    for h in range(DSA_HEADS):
        slab = qt_ref[0, (h // 2) * LANES:(h // 2 + 1) * LANES, :]
        mine = (hrow >= DSA_HEAD_DIM) if (h % 2) else (hrow < DSA_HEAD_DIM)
        qm_scr[h] = jnp.where(mine, slab, jnp.zeros_like(slab))
        m_scr[h] = jnp.full((SUBLANES, tq), NEG_BIG, F32)
        acc_scr[h] = jnp.zeros((ACC_ROWS, tq), F32)
    ones_rows = jnp.ones((ACC_ROWS - LANES, tk), BF16)

    def slab(h):
        return slice((h // 2) * LANES, (h // 2 + 1) * LANES)

    def tile_operands(j):
        t3 = s_scr[j].reshape(grp, SUBLANES, tq)
        bias = jnp.where(t3 >= thr[None], 0.0, NEG_BIG).reshape(tk, tq)
        return k_ref[0, pl.ds(pl.multiple_of(j * tk, tk), tk), :], bias

    def masked_scores(keys, bias, h):
        return _dot(keys[:, slab(h)], qm_scr[h]) + bias

    keys0, bias0 = tile_operands(0)
    for h in range(DSA_HEADS):
        sn_scr[h] = masked_scores(keys0, bias0, h)

    def att_body(j, _):
        keys_n, bias_n = tile_operands(jnp.minimum(j + 1, i))
        for h in range(DSA_HEADS):
            s3 = sn_scr[h].reshape(grp, SUBLANES, tq)
            sn_scr[h] = masked_scores(keys_n, bias_n, h)
            m_prev = m_scr[h]
            m_new = jnp.maximum(m_prev, spread(jnp.max(tree(s3, jnp.max), axis=0, keepdims=True)))
            p = jnp.exp2((s3 - m_new[None]).reshape(tk, tq).astype(BF16))
            corr = jnp.exp2(m_prev - m_new)
            pv = _dot(jnp.concatenate([vt_ref[0, j, slab(h), :], ones_rows], axis=0), p)
            acc = acc_scr[h].reshape(ACC_ROWS // SUBLANES, SUBLANES, tq) * corr[None]
            acc_scr[h] = acc.reshape(ACC_ROWS, tq) + pv
            m_scr[h] = m_new
        return 0

    lax.fori_loop(0, i + 1, att_body, 0)

    def normalised(h):
        inv = 1.0 / acc_scr[h, LANES:LANES + SUBLANES, :]
        return (acc_scr[h, 0:LANES, :].reshape(LANES // SUBLANES, SUBLANES, tq) * inv[None]).reshape(LANES, tq)

    for sp in range(DSA_HEADS // 2):
        pair = jnp.where(hrow < DSA_HEAD_DIM, normalised(2 * sp), normalised(2 * sp + 1))
        o_ref[0, sp * LANES:(sp + 1) * LANES, :] = pair.astype(o_ref.dtype)


def _dsa(iqt, ik, iwt, qt, k, vt):
    B, T, _ = k.shape
    tq = DSA_TILE
    nq = T // tq
    topk = min(DSA_TOPK_MAX, T // 4)
    kern = functools.partial(_dsa_kernel, topk=topk)
    return pl.pallas_call(
        kern,
        grid=(B, nq),
        in_specs=[
            pl.BlockSpec((1, 1, IDX_DIM, IDX_HEADS * tq), lambda b, i: (b, i, 0, 0)),
            pl.BlockSpec((1, T, IDX_DIM), lambda b, i: (b, 0, 0)),
            pl.BlockSpec((1, IDX_HEADS, tq), lambda b, i: (b, 0, i)),
            pl.BlockSpec((1, DSA_WIDTH, tq), lambda b, i: (b, 0, i)),
            pl.BlockSpec((1, T, DSA_WIDTH), lambda b, i: (b, 0, 0)),
            pl.BlockSpec((1, nq, DSA_WIDTH, tq), lambda b, i: (b, 0, 0, 0)),
        ],
        out_specs=pl.BlockSpec((1, DSA_WIDTH, tq), lambda b, i: (b, 0, i)),
        out_shape=jax.ShapeDtypeStruct((B, DSA_WIDTH, T), BF16),
        scratch_shapes=[
            pltpu.VMEM((nq + 1, tq, tq), F32),
            pltpu.VMEM((DSA_HEADS, LANES, tq), BF16),
            pltpu.VMEM((4, SUBLANES, tq), F32),
            pltpu.VMEM((DSA_HEADS, SUBLANES, tq), F32),
            pltpu.VMEM((DSA_HEADS, ACC_ROWS, tq), F32),
            pltpu.VMEM((DSA_HEADS, tq, tq), F32),
            pltpu.VMEM((tq, IDX_HEADS * tq), F32),
        ],
        compiler_params=_params(("parallel", "arbitrary")),
    )(iqt, ik, iwt, qt, k, vt)


def _gla_kernel(q_ref, k_ref, v_ref, r_ref, la_ref, g_ref, o_ref, s_scr):
    rows = GLA_BLOCK
    nch = rows // CHUNK
    npair = GLA_HEADS // 2
    nsub = q_ref.shape[1] // rows

    @pl.when(pl.program_id(1) == 0)
    def _():
        s_scr[...] = jnp.zeros_like(s_scr)

    ri = lax.broadcasted_iota(jnp.int32, (rows, rows), 0)
    ci = lax.broadcasted_iota(jnp.int32, (rows, rows), 1)
    causal = jnp.logical_and((ri >> 6) == (ci >> 6), ci <= ri)
    tri = jnp.where(causal, 1.0, 0.0).astype(BF16)
    lane = lax.broadcasted_iota(jnp.int32, (rows, LANES), 1)
    eye = lax.broadcasted_iota(jnp.int32, (LANES, LANES), 0) == lax.broadcasted_iota(jnp.int32, (LANES, LANES), 1)
    top_rows = lax.broadcasted_iota(jnp.int32, (LANES, GLA_DV), 0) < GLA_DK

    def decayed(r0):
        la = la_ref[0, r0:r0 + rows, :]
        la_hi = la.astype(BF16)
        rem = la - la_hi.astype(F32)
        la_mid = rem.astype(BF16)
        la_lo = (rem - la_mid.astype(F32)).astype(BF16)
        b = _dot(tri, la_hi) + _dot(tri, la_mid) + _dot(tri, la_lo)
        b_last = jnp.concatenate(
            [jnp.broadcast_to(b[(c + 1) * CHUNK - 1:(c + 1) * CHUNK, :], (CHUNK, b.shape[1])) for c in range(nch)],
            axis=0)
        q = q_ref[0, r0:r0 + rows, :]
        k = k_ref[0, r0:r0 + rows, :]
        return q * jnp.exp(b), (k * jnp.exp(-b)).astype(BF16), (k * jnp.exp(b_last - b)).astype(BF16), jnp.exp(b_last)

    subs = [decayed(sb * rows) for sb in range(nsub)]

    for pr in range(npair):
        sl = slice(pr * LANES, (pr + 1) * LANES)
        heads = (2 * pr, 2 * pr + 1)
        state = s_scr[pr]
        for sb in range(nsub):
            r0 = sb * rows
            qg, kg, kd, decay_rows = subs[sb]
            qg_p = qg[:, sl]
            kg_p = kg[:, sl]
            kd_p = kd[:, sl]
            qg_h = [jnp.where(lane < GLA_DK, qg_p, 0.0).astype(BF16), jnp.where(lane >= GLA_DK, qg_p, 0.0).astype(BF16)]
            v_h = [v_ref[0, r0:r0 + rows, h * GLA_DV:(h + 1) * GLA_DV].astype(BF16) for h in heads]
            o_h = []
            for t in range(2):
                a = jnp.where(causal, _dot_nt(qg_h[t], kg_p), 0.0)
                o_h.append(_dot(a.astype(BF16), v_h[t]))
            inter = [[], []]
            for c in range(nch):
                rs = slice(c * CHUNK, (c + 1) * CHUNK)
                state_b = state.astype(BF16)
                for t in range(2):
                    inter[t].append(_dot(qg_h[t][rs], state_b))
                kv = jnp.where(top_rows, _dot_tn(kd_p[rs], v_h[0][rs]), _dot_tn(kd_p[rs], v_h[1][rs]))
                drow = jnp.broadcast_to(decay_rows[c * CHUNK:c * CHUNK + 1, sl], (LANES, LANES))
                dcol = jnp.sum(jnp.where(eye, drow, 0.0), axis=1, keepdims=True)
                state = dcol * state + kv
            for t in range(2):
                h = heads[t]
                o = o_h[t] + jnp.concatenate(inter[t], axis=0)
                ms = jnp.mean(o * o, axis=1, keepdims=True)
                on = o * lax.rsqrt(ms + LN_EPS) * g_ref[...]
                r = r_ref[0, r0:r0 + rows, h * GLA_DV:(h + 1) * GLA_DV]
                o_ref[0, r0:r0 + rows, h * GLA_DV:(h + 1) * GLA_DV] = (on * (r / (1.0 + jnp.exp(-r)))).astype(o_ref.dtype)
        s_scr[pr] = state


def _gla(bq, bk, bv, br, la, g):
    B, T, _ = bq.shape
    rows = GLA_STEP
    blk = lambda w: pl.BlockSpec((1, rows, w), lambda b, i: (b, i, 0))
    return pl.pallas_call(
        _gla_kernel,
        grid=(B, T // rows),
        in_specs=[blk(256), blk(256), blk(512), blk(512), blk(256), pl.BlockSpec((1, GLA_DV), lambda b, i: (0, 0))],
        out_specs=blk(GLA_WIDTH),
        out_shape=jax.ShapeDtypeStruct((B, T, GLA_WIDTH), BF16),
        scratch_shapes=[pltpu.VMEM((GLA_HEADS // 2, 2 * GLA_DK, GLA_DV), F32)],
        compiler_params=_params(("parallel", "arbitrary")),
    )(bq, bk, bv, br, la, g)


def _layer_norm(x, g, b):
    mu = jnp.mean(x, axis=1, keepdims=True)
    xc = x - mu
    var = jnp.mean(xc * xc, axis=1, keepdims=True)
    return xc * lax.rsqrt(var + LN_EPS) * g + b


def _out_proj_kernel(x_ref, ya_ref, yb_ref, wo_ref, g_ref, b_ref, wr_hi_ref, wr_lo_ref, br_ref, h_ref, route_ref):
    tm = x_ref.shape[0]
    mix = _dot(ya_ref[...], wo_ref[0:DSA_WIDTH, :]) + _dot(yb_ref[...], wo_ref[DSA_WIDTH:D_MODEL, :])
    h = _layer_norm(DEEPNORM_ALPHA * x_ref[...] + mix, g_ref[...], b_ref[...])
    h_ref[...] = h

    h_hi = h.astype(BF16)
    h_lo = (h - h_hi.astype(F32)).astype(BF16)
    logits = _dot(h_hi, wr_hi_ref[...]) + _dot(h_lo, wr_hi_ref[...]) + _dot(h_hi, wr_lo_ref[...]) + br_ref[...]

    lane = lax.broadcasted_iota(jnp.int32, (tm, LANES), 1)
    lanef = lane.astype(F32)
    gl = jnp.where(lane < N_GROUPS, logits, -jnp.inf)
    gmax = jnp.max(gl, axis=1, keepdims=True)
    gsel = jnp.min(jnp.where(gl == gmax, lanef, 1e9), axis=1, keepdims=True)
    pg = 1.0 / jnp.sum(jnp.exp(gl - gmax), axis=1, keepdims=True)
    egrp = ((lane - N_GROUPS) >> 3).astype(F32)
    in_grp = jnp.logical_and(jnp.logical_and(lane >= N_GROUPS, lane < N_GROUPS + N_EXPERTS), egrp == gsel)
    el = jnp.where(in_grp, logits, -jnp.inf)
    t1 = jnp.max(el, axis=1, keepdims=True)
    i1 = jnp.min(jnp.where(el == t1, lanef, 1e9), axis=1, keepdims=True)
    el2 = jnp.where(lanef == i1, -jnp.inf, el)
    t2 = jnp.max(el2, axis=1, keepdims=True)
    i2 = jnp.min(jnp.where(el2 == t2, lanef, 1e9), axis=1, keepdims=True)
    e21 = jnp.exp(t2 - t1)
    g1 = pg / (1.0 + e21)
    g2 = pg * e21 / (1.0 + e21)
    route = jnp.where(lane == 0, i1 - N_GROUPS, 0.0)
    route = jnp.where(lane == 1, i2 - N_GROUPS, route)
    route = jnp.where(lane == 2, g1, route)
    route = jnp.where(lane == 3, g2, route)
    route_ref[...] = route


def _out_proj(x2, ya, yb, wo, g1, b1, wr_hi, wr_lo, br):
    N = x2.shape[0]
    tm = TOK_TILE
    row = lambda i: (i, 0)
    const = lambda i: (0, 0)
    return pl.pallas_call(
        _out_proj_kernel,
        grid=(N // tm,),
        in_specs=[
            pl.BlockSpec((tm, D_MODEL), row), pl.BlockSpec((tm, DSA_WIDTH), row), pl.BlockSpec((tm, GLA_WIDTH), row),
            pl.BlockSpec(wo.shape, const), pl.BlockSpec(g1.shape, const), pl.BlockSpec(b1.shape, const),
            pl.BlockSpec(wr_hi.shape, const), pl.BlockSpec(wr_lo.shape, const), pl.BlockSpec(br.shape, const),
        ],
        out_specs=[pl.BlockSpec((tm, D_MODEL), row), pl.BlockSpec((tm, LANES), row)],
        out_shape=[jax.ShapeDtypeStruct((N, D_MODEL), F32), jax.ShapeDtypeStruct((N, LANES), F32)],
        compiler_params=_params(("parallel",)),
    )(x2, ya, yb, wo, g1, b1, wr_hi, wr_lo, br)


def _rank_kernel(route_ref, rank_ref, cnt_ref, carry_scr):
    tm = route_ref.shape[0]

    @pl.when(pl.program_id(0) == 0)
    def _():
        carry_scr[...] = jnp.zeros_like(carry_scr)

    route = route_ref[...]
    lanef = lax.broadcasted_iota(jnp.int32, (tm, LANES), 1).astype(F32)
    e1 = route[:, 0:1]
    e2 = route[:, 1:2]
    hit1 = lanef == e1
    hit2 = lanef == e2
    onehot = jnp.where(jnp.logical_or(hit1, hit2), 1.0, 0.0).astype(BF16)
    ri = lax.broadcasted_iota(jnp.int32, (tm, tm), 0)
    ci = lax.broadcasted_iota(jnp.int32, (tm, tm), 1)
    before = jnp.where(ci < ri, 1.0, 0.0).astype(BF16)
    prefix = _dot(before, onehot) + carry_scr[0:1, :]
    r1 = jnp.sum(jnp.where(hit1, prefix, 0.0), axis=1, keepdims=True)
    r2 = jnp.sum(jnp.where(hit2, prefix, 0.0), axis=1, keepdims=True)
    rank_ref[...] = jnp.where(lanef == 0.0, r1, jnp.where(lanef == 1.0, r2, 0.0))
    total = _dot(jnp.ones((8, tm), BF16), onehot)
    carry_scr[...] = carry_scr[...] + total
    cnt_ref[...] = carry_scr[...]


def _rank(route):
    N = route.shape[0]
    tm = RANK_TILE
    return pl.pallas_call(
        _rank_kernel,
        grid=(N // tm,),
        in_specs=[pl.BlockSpec((tm, LANES), lambda i: (i, 0))],
        out_specs=[pl.BlockSpec((tm, LANES), lambda i: (i, 0)), pl.BlockSpec((8, LANES), lambda i: (0, 0))],
        out_shape=[jax.ShapeDtypeStruct((N, LANES), F32), jax.ShapeDtypeStruct((8, LANES), F32)],
        scratch_shapes=[pltpu.VMEM((8, LANES), F32)],
        compiler_params=_params(("arbitrary",)),
    )(route)


def _dispatch_kernel(pos_ref, h_ref, xs_in_ref, xs_ref, hbuf, pbuf, load_sem, row_sem):
    del xs_in_ref
    tm = DISPATCH_TILE
    nbuf = hbuf.shape[0]
    i = pl.program_id(0)
    last = pl.num_programs(0) - 1

    def tile_load(step):
        return pltpu.make_async_copy(h_ref.at[pl.ds(step * tm, tm), :], hbuf.at[step % nbuf], load_sem.at[step % nbuf])

    def row_copy(step, r, slot):
        p = pos_ref[2 * (step * tm + r) + slot]
        return pltpu.make_async_copy(pbuf.at[step % nbuf, pl.ds(r, 1), :], xs_ref.at[pl.ds(p, 1), :],
                                     row_sem.at[step % nbuf])

    def for_rows(fn):
        def body(r, _):
            fn(r, 0)
            fn(r, 1)
            return 0
        lax.fori_loop(0, tm, body, 0, unroll=DMA_UNROLL)

    @pl.when(i == 0)
    def _():
        tile_load(0).start()

    @pl.when(i < last)
    def _():
        tile_load(i + 1).start()

    tile_load(i).wait()
    pbuf[i % nbuf] = _pack_halves(hbuf[i % nbuf])
    for_rows(lambda r, slot: row_copy(i, r, slot).start())

    @pl.when(i > 0)
    def _():
        for_rows(lambda r, slot: row_copy(i - 1, r, slot).wait())

    @pl.when(i == last)
    def _():
        for_rows(lambda r, slot: row_copy(i, r, slot).wait())


def _dispatch(pos_flat, h, xs_init):
    N = h.shape[0]
    grid_spec = pltpu.PrefetchScalarGridSpec(
        num_scalar_prefetch=1,
        grid=(N // DISPATCH_TILE,),
        in_specs=[pl.BlockSpec(memory_space=pl.ANY), pl.BlockSpec(memory_space=pl.ANY)],
        out_specs=pl.BlockSpec(memory_space=pl.ANY),
        scratch_shapes=[pltpu.VMEM((3, DISPATCH_TILE, D_MODEL), F32),
                        pltpu.VMEM((3, DISPATCH_TILE, D_MODEL // 2), jnp.uint32),
                        pltpu.SemaphoreType.DMA((3,)), pltpu.SemaphoreType.DMA((3,))],
    )
    return pl.pallas_call(
        _dispatch_kernel,
        grid_spec=grid_spec,
        out_shape=jax.ShapeDtypeStruct(xs_init.shape, xs_init.dtype),
        input_output_aliases={2: 0},
        compiler_params=_params(("arbitrary",)),
    )(pos_flat, h, xs_init)


def _ffn_kernel(te_ref, na_ref, x_ref, wi_ref, wo_ref, y_ref, wi_bf, wo_bf):
    i = pl.program_id(0)
    active = i < na_ref[0]
    fresh = jnp.logical_or(i == 0, te_ref[i] != te_ref[jnp.maximum(i - 1, 0)])

    @pl.when(jnp.logical_and(active, fresh))
    def _():
        wi_bf[...] = wi_ref[0].astype(BF16)
        wo_bf[...] = wo_ref[0].astype(BF16)

    @pl.when(active)
    def _():
        x_left, x_right = _unpack_halves(x_ref[...])
        half = D_MODEL // 2
        hid = (_dot(x_left.astype(BF16), wi_bf[0:half, :])
               + _dot(x_right.astype(BF16), wi_bf[half:D_MODEL, :]))
        hg = hid[:, :D_EXPERT]
        hu = hid[:, D_EXPERT:]
        act = (hg / (1.0 + jnp.exp(-hg))) * hu
        y_ref[...] = _pack_halves(_dot(act.astype(BF16), wo_bf[...]))

    @pl.when(jnp.logical_not(active))
    def _():
        y_ref[...] = jnp.zeros_like(y_ref)


def _ffn(tile_expert, n_active, xs, w_e_in, w_e_out):
    R = xs.shape[0]
    tm = MOE_TILE

    def live(i, te, na):
        return jnp.minimum(i, na[0] - 1)

    grid_spec = pltpu.PrefetchScalarGridSpec(
        num_scalar_prefetch=2,
        grid=(R // tm,),
        in_specs=[
            pl.BlockSpec((tm, D_MODEL // 2), lambda i, te, na: (live(i, te, na), 0)),
            pl.BlockSpec((1, D_MODEL, 2 * D_EXPERT), lambda i, te, na: (te[live(i, te, na)], 0, 0)),
            pl.BlockSpec((1, D_EXPERT, D_MODEL), lambda i, te, na: (te[live(i, te, na)], 0, 0)),
        ],
        out_specs=pl.BlockSpec((tm, D_MODEL // 2), lambda i, te, na: (i, 0)),
        scratch_shapes=[pltpu.VMEM((D_MODEL, 2 * D_EXPERT), BF16), pltpu.VMEM((D_EXPERT, D_MODEL), BF16)],
    )
    return pl.pallas_call(
        _ffn_kernel,
        grid_spec=grid_spec,
        out_shape=jax.ShapeDtypeStruct((R, D_MODEL // 2), jnp.uint32),
        compiler_params=_params(("arbitrary",)),
    )(tile_expert, n_active, xs, w_e_in, w_e_out)


def _combine_kernel(pos_ref, h_ref, route_ref, g_ref, b_ref, ys_ref, o_ref, buf, sem):
    tm = h_ref.shape[0]
    i = pl.program_id(0)
    cur = i % 2

    def row_copy(step, r, slot):
        p = pos_ref[2 * (step * tm + r) + slot]
        half = step % 2
        return pltpu.make_async_copy(ys_ref.at[pl.ds(p, 1), :], buf.at[half, slot, pl.ds(r, 1), :], sem.at[half])

    def for_rows(fn):
        def body(r, _):
            fn(r, 0)
            fn(r, 1)
            return 0
        lax.fori_loop(0, tm, body, 0, unroll=DMA_UNROLL)

    @pl.when(i == 0)
    def _():
        for_rows(lambda r, slot: row_copy(0, r, slot).start())

    @pl.when(i + 1 < pl.num_programs(0))
    def _():
        for_rows(lambda r, slot: row_copy(i + 1, r, slot).start())

    for_rows(lambda r, slot: row_copy(i, r, slot).wait())
    route = route_ref[...]
    l0, r0 = _unpack_halves(buf[cur, 0])
    l1, r1 = _unpack_halves(buf[cur, 1])
    g0, g1 = route[:, 2:3], route[:, 3:4]
    ffn = jnp.concatenate([l0 * g0 + l1 * g1, r0 * g0 + r1 * g1], axis=1)
    o_ref[...] = _layer_norm(DEEPNORM_ALPHA * h_ref[...] + ffn, g_ref[...], b_ref[...])


def _combine(pos_flat, h, route, g2, b2, ys):
    N = h.shape[0]
    tm = TOK_TILE
    row = lambda i, pos: (i, 0)
    const = lambda i, pos: (0, 0)
    grid_spec = pltpu.PrefetchScalarGridSpec(
        num_scalar_prefetch=1,
        grid=(N // tm,),
        in_specs=[
            pl.BlockSpec((tm, D_MODEL), row), pl.BlockSpec((tm, LANES), row),
            pl.BlockSpec(g2.shape, const), pl.BlockSpec(b2.shape, const),
            pl.BlockSpec(memory_space=pl.ANY),
        ],
        out_specs=pl.BlockSpec((tm, D_MODEL), row),
        scratch_shapes=[pltpu.VMEM((2, 2, tm, D_MODEL // 2), jnp.uint32), pltpu.SemaphoreType.DMA((2,))],
    )
    return pl.pallas_call(
        _combine_kernel,
        grid_spec=grid_spec,
        out_shape=jax.ShapeDtypeStruct((N, D_MODEL), F32),
        compiler_params=_params(("arbitrary",)),
    )(pos_flat, h, route, g2, b2, ys)


def _rope_tables(T, dim):
    half = dim // 2
    inv = 1.0 / (ROPE_THETA ** (jnp.arange(half, dtype=F32) / half))
    ang = jnp.arange(T).astype(F32)[:, None] * inv[None, :]
    cos = jnp.cos(ang)
    sin = jnp.sin(ang)
    reps = LANES // dim
    cos_t = jnp.tile(jnp.concatenate([cos, cos], axis=1), (1, reps))
    sin_t = jnp.tile(jnp.concatenate([-sin, sin], axis=1), (1, reps))
    return cos_t, sin_t


def _pad_cols(w, width):
    return jnp.pad(w, ((0, 0), (0, width - w.shape[1])))


def _layer(x, w_in, w_gla_gate, b_gla_gate, g_gla_norm, w_out, ln1_g, ln1_b,
           w_gr, b_gr, w_er, b_er, w_e_in, w_e_out, ln2_g, ln2_b):
    B, T, D = x.shape
    N = B * T
    assert D == D_MODEL and T % DSA_TILE == 0 and N % DISPATCH_TILE == 0 and DISPATCH_TILE % RANK_TILE == 0
    x2 = x.reshape(N, D)

    sizes = (512, 512, 512, 256, 32, 8, 256, 256, 512, 512, 16)
    offs = np.concatenate([[0], np.cumsum(sizes)])
    col = lambda k: w_in[:, offs[k]:offs[k + 1]]
    wa = jnp.concatenate([col(0) * (DSA_HEAD_DIM ** -0.5 * LOG2E), col(1), col(2)], axis=1).astype(BF16)
    wi = _pad_cols(jnp.concatenate([col(3), col(4), col(5) * IDX_SCALE], axis=1), 3 * LANES).astype(BF16)
    wb = jnp.concatenate([col(6) * (GLA_DK ** -0.5), col(7), col(8), col(9)], axis=1).astype(BF16)
    wg = _pad_cols(col(10), LANES).astype(BF16)
    wgate = jnp.pad(w_gla_gate, ((0, LANES - GLA_GATE_RANK), (0, 0))).astype(BF16)
    bgate = b_gla_gate.reshape(1, -1)
    cosa, sina = _rope_tables(T, DSA_HEAD_DIM)
    cosi, sini = _rope_tables(T, IDX_DIM)

    q, k, v, iq, ikw, bq, bk, bv, br, la = _in_proj(x2, wa, wi, wb, wg, wgate, bgate, cosa, sina, cosi, sini, T)

    nq = T // DSA_TILE
    iqt = iq.reshape(B, nq, DSA_TILE, IDX_HEADS, IDX_DIM).transpose(0, 1, 4, 3, 2)
    iqt = iqt.reshape(B, nq, IDX_DIM, IDX_HEADS * DSA_TILE)
    ik = ikw[:, :IDX_DIM].astype(BF16).reshape(B, T, IDX_DIM)
    iwt = ikw[:, IDX_DIM:IDX_DIM + IDX_HEADS].reshape(B, T, IDX_HEADS).transpose(0, 2, 1)
    qt = q.reshape(B, T, DSA_WIDTH).transpose(0, 2, 1)
    vt = v.reshape(B, nq, DSA_TILE, DSA_WIDTH).transpose(0, 1, 3, 2)
    ya = _dsa(iqt, ik, iwt, qt, k.reshape(B, T, DSA_WIDTH), vt).transpose(0, 2, 1)

    r3 = lambda a: a.reshape(B, T, a.shape[-1])
    yb = _gla(r3(bq), r3(bk), r3(bv), r3(br), r3(la), g_gla_norm.reshape(1, GLA_DV))

    wr = _pad_cols(jnp.concatenate([w_gr, w_er], axis=1), LANES)
    wr_hi = wr.astype(BF16)
    wr_lo = (wr - wr_hi.astype(F32)).astype(BF16)
    brt = _pad_cols(jnp.concatenate([b_gr, b_er]).reshape(1, -1), LANES)
    h, route = _out_proj(x2, ya.reshape(N, DSA_WIDTH), yb.reshape(N, GLA_WIDTH), w_out.astype(BF16),
                         ln1_g.reshape(1, D), ln1_b.reshape(1, D), wr_hi, wr_lo, brt)

    rank, cnt = _rank(route)
    counts = cnt[0, :N_EXPERTS].astype(jnp.int32)
    padded = ((counts + MOE_TILE - 1) // MOE_TILE) * MOE_TILE
    ends = jnp.cumsum(padded)
    starts = ends - padded
    eid = route[:, 0:2].astype(jnp.int32)
    pos = (starts[eid] + rank[:, 0:2].astype(jnp.int32)).reshape(-1)
    n_rows = 2 * N + N_EXPERTS * MOE_TILE
    n_tiles = n_rows // MOE_TILE
    tile_start = jnp.arange(n_tiles, dtype=jnp.int32) * MOE_TILE
    tile_expert = jnp.minimum(jnp.sum(tile_start[:, None] >= ends[None, :], axis=1), N_EXPERTS - 1).astype(jnp.int32)
    n_active = (ends[-1] // MOE_TILE).astype(jnp.int32).reshape(1)

    xs = _dispatch(pos, h, jnp.zeros((n_rows, D // 2), jnp.uint32))
    ys = _ffn(tile_expert, n_active, xs, w_e_in, w_e_out)
    out = _combine(pos, h, route, ln2_g.reshape(1, D), ln2_b.reshape(1, D), ys)
    return out.reshape(B, T, D)


def kernel(x, w_in, w_gla_gate, b_gla_gate, g_gla_norm, w_out, ln1_g, ln1_b, w_group_router, b_group_router,
           w_expert_router, b_expert_router, w_expert_in, w_expert_out, ln2_g, ln2_b):
    h = x
    for l in range(w_in.shape[0]):
        h = _layer(h, w_in[l], w_gla_gate[l], b_gla_gate[l], g_gla_norm[l], w_out[l], ln1_g[l], ln1_b[l],
                   w_group_router[l], b_group_router[l], w_expert_router[l], b_expert_router[l],
                   w_expert_in[l], w_expert_out[l], ln2_g[l], ln2_b[l])
    return h
```

```python
import functools

import numpy as np
import jax
import jax.numpy as jnp
from jax import lax
from jax.experimental import pallas as pl
from jax.experimental.pallas import tpu as pltpu

F32 = jnp.float32
BF16 = jnp.bfloat16

D_MODEL = 1024
CHUNK = 64
ROPE_THETA = 10000.0
LN_EPS = 1e-5
DSA_WIDTH = 512
DSA_HEAD_DIM = 64
DSA_HEADS = 8
IDX_HEADS = 8
IDX_DIM = 32
IDX_SCALE = (IDX_HEADS * IDX_DIM) ** -0.5
DSA_TOPK_MAX = 256
GLA_WIDTH = 512
GLA_HEADS = 4
GLA_DV = 128
GLA_DK = 64
GLA_GATE_RANK = 16
GLA_TAU = 16.0
N_GROUPS = 4
EXPERTS_PER_GROUP = 8
N_EXPERTS = 32
D_EXPERT = 512
DEEPNORM_ALPHA = 2.0 ** 0.25

LANES = 128
SUBLANES = 8
TINY = 2.0 ** -126
NEG_BIG = -1e30
ACC_ROWS = LANES + 16
VMEM_LIMIT = 56 * 1024 * 1024

TOK_TILE = 1024
DSA_TILE = 256
GLA_BLOCK = 256
GLA_STEP = 1024
MOE_TILE = 512
RANK_TILE = 512
DISPATCH_TILE = 512
DMA_UNROLL = 8
LOG2E = 1.4426950408889634
COUNT_UNROLL = 4
BISECT_ARITH = 40
BISECT_CAP = 80


def _dot(a, b):
    return jnp.dot(a, b, preferred_element_type=F32)


def _dot_nt(a, b):
    return lax.dot_general(a, b, (((1,), (1,)), ((), ())), preferred_element_type=F32)


def _dot_tn(a, b):
    return lax.dot_general(a, b, (((0,), (0,)), ((), ())), preferred_element_type=F32)


def _pack_halves(x):
    w = x.shape[1] // 2
    hi = lax.bitcast_convert_type(x[:, :w].astype(BF16).astype(F32), jnp.uint32)
    lo = lax.bitcast_convert_type(x[:, w:].astype(BF16).astype(F32), jnp.uint32)
    return (hi & jnp.uint32(0xFFFF0000)) | (lo >> 16)


def _unpack_halves(p):
    left = lax.bitcast_convert_type(p & jnp.uint32(0xFFFF0000), F32)
    right = lax.bitcast_convert_type(p << 16, F32)
    return left, right


def _params(sem):
    return pltpu.CompilerParams(dimension_semantics=sem, vmem_limit_bytes=VMEM_LIMIT)


def _rope_slab(slab, cos, sin, first_half, half):
    swapped = jnp.where(first_half, pltpu.roll(slab, LANES - half, 1), pltpu.roll(slab, half, 1))
    return slab * cos + swapped * sin


def _in_proj_kernel(x_ref, wa_ref, wi_ref, wb_ref, wg_ref, wgate_ref, bgate_ref,
                    cosa_ref, sina_ref, cosi_ref, sini_ref,
                    q_ref, k_ref, v_ref, iq_ref, ikw_ref, bq_ref, bk_ref, bv_ref, br_ref, la_ref):
    tm = x_ref.shape[0]
    xb = x_ref[...].astype(BF16)
    lane = lax.broadcasted_iota(jnp.int32, (tm, LANES), 1)

    a = _dot(xb, wa_ref[...])
    cosa, sina = cosa_ref[...], sina_ref[...]
    first_a = (lane & (DSA_HEAD_DIM - 1)) < DSA_HEAD_DIM // 2
    for c in range(DSA_WIDTH // LANES):
        sl = slice(c * LANES, (c + 1) * LANES)
        q_ref[:, sl] = _rope_slab(a[:, sl], cosa, sina, first_a, DSA_HEAD_DIM // 2).astype(BF16)
        ks = slice(DSA_WIDTH + c * LANES, DSA_WIDTH + (c + 1) * LANES)
        k_ref[:, sl] = _rope_slab(a[:, ks], cosa, sina, first_a, DSA_HEAD_DIM // 2).astype(BF16)
    v_ref[...] = a[:, 2 * DSA_WIDTH:3 * DSA_WIDTH].astype(BF16)

    ii = _dot(xb, wi_ref[...])
    cosi, sini = cosi_ref[...], sini_ref[...]
    first_i = (lane & (IDX_DIM - 1)) < IDX_DIM // 2
    for c in range(2):
        sl = slice(c * LANES, (c + 1) * LANES)
        iq_ref[:, sl] = _rope_slab(ii[:, sl], cosi, sini, first_i, IDX_DIM // 2).astype(BF16)
    last = ii[:, 2 * LANES:3 * LANES]
    ikw_ref[...] = jnp.where(lane < IDX_DIM, _rope_slab(last, cosi, sini, first_i, IDX_DIM // 2), last)

    b = _dot(xb, wb_ref[...])
    bq_ref[...] = b[:, 0:256]
    bk_ref[...] = b[:, 256:512]
    bv_ref[...] = b[:, 512:1024]
    br_ref[...] = b[:, 1024:1536]

    g = _dot(xb, wg_ref[...])
    z = _dot(g.astype(BF16), wgate_ref[...]) + bgate_ref[...]
    log_sig = jnp.minimum(z, 0.0) - jnp.log(1.0 + jnp.exp(-jnp.abs(z)))
    la_ref[...] = log_sig * (1.0 / GLA_TAU)


def _in_proj(x2, wa, wi, wb, wg, wgate, bgate, cosa, sina, cosi, sini, T):
    N = x2.shape[0]
    tm = TOK_TILE
    nt = T // tm
    row = lambda i: (i, 0)
    const = lambda i: (0, 0)
    pos = lambda i: (i % nt, 0)
    outs = [
        (DSA_WIDTH, BF16), (DSA_WIDTH, BF16), (DSA_WIDTH, BF16), (IDX_HEADS * IDX_DIM, BF16), (LANES, F32),
        (256, F32), (256, F32), (512, F32), (512, F32), (256, F32),
    ]
    return pl.pallas_call(
        _in_proj_kernel,
        grid=(N // tm,),
        in_specs=[
            pl.BlockSpec((tm, D_MODEL), row),
            pl.BlockSpec(wa.shape, const), pl.BlockSpec(wi.shape, const), pl.BlockSpec(wb.shape, const),
            pl.BlockSpec(wg.shape, const), pl.BlockSpec(wgate.shape, const), pl.BlockSpec(bgate.shape, const),
            pl.BlockSpec((tm, LANES), pos), pl.BlockSpec((tm, LANES), pos),
            pl.BlockSpec((tm, LANES), pos), pl.BlockSpec((tm, LANES), pos),
        ],
        out_specs=[pl.BlockSpec((tm, w), row) for w, _ in outs],
        out_shape=[jax.ShapeDtypeStruct((N, w), dt) for w, dt in outs],
        compiler_params=_params(("parallel",)),
    )(x2, wa, wi, wb, wg, wgate, bgate, cosa, sina, cosi, sini)


def _dsa_kernel(iqt_ref, ik_ref, iwt_ref, qt_ref, k_ref, vt_ref, o_ref,
                s_scr, qm_scr, st_scr, m_scr, acc_scr, sn_scr, rt_scr, *, topk):
    tq = DSA_TILE
    tk = DSA_TILE
    grp = tk // SUBLANES
    i = pl.program_id(1)
    kf = jnp.float32(topk)

    iqt = iqt_ref[0, 0]
    iwt = iwt_ref[0]
    krow = lax.broadcasted_iota(jnp.int32, (tk, tq), 0)
    qcol = lax.broadcasted_iota(jnp.int32, (tk, tq), 1)
    qcol8 = lax.broadcasted_iota(jnp.int32, (SUBLANES, tq), 1)

    def tree(x3, op):
        q4 = grp // 4
        return op(jnp.stack([op(x3[a * q4:(a + 1) * q4], axis=0) for a in range(4)]), axis=0)

    def fold(x, op):
        return tree(x.reshape(grp, SUBLANES, tq), op)

    def spread(x):
        return jnp.broadcast_to(x, (SUBLANES, tq))

    def head_products(j):
        keys = ik_ref[0, pl.ds(pl.multiple_of(j * tk, tk), tk), :]
        return _dot(keys, iqt)

    def weighted_relu_sum():
        acc = None
        for h in range(IDX_HEADS):
            term = jnp.maximum(rt_scr[:, h * tq:(h + 1) * tq], 0.0) * iwt[h:h + 1, :]
            acc = term if acc is None else acc + term
        return acc

    def tile_stats(s_hi, s_lo, carry):
        rmax, rmin, cpos, cnn = carry
        return (jnp.maximum(rmax, fold(s_hi, jnp.max)), jnp.minimum(rmin, fold(s_lo, jnp.min)),
                cpos + fold(jnp.where(s_hi > 0.0, 1.0, 0.0), jnp.sum),
                cnn + fold(jnp.where(s_hi >= 0.0, 1.0, 0.0), jnp.sum))

    def score_body(j, carry):
        s = weighted_relu_sum()
        s_scr[j] = s
        rt_scr[...] = head_products(j + 1)
        return tile_stats(s, s, carry)

    zeros8 = jnp.zeros((SUBLANES, tq), F32)
    init = (jnp.full((SUBLANES, tq), -jnp.inf, F32), jnp.full((SUBLANES, tq), jnp.inf, F32), zeros8, zeros8)
    rt_scr[...] = head_products(0)
    carry = lax.fori_loop(0, i, score_body, init)
    s = weighted_relu_sum()
    adm = (krow >> 6) <= (qcol >> 6)
    s_adm = jnp.where(adm, s, -jnp.inf)
    s_scr[i] = s_adm
    s_scr[i + 1] = jnp.full((tk, tq), -jnp.inf, F32)
    rmax, rmin, cpos, cnn = tile_stats(s_adm, jnp.where(adm, s, jnp.inf), carry)
    rmax = spread(jnp.max(rmax, axis=0, keepdims=True))
    rmin = spread(jnp.min(rmin, axis=0, keepdims=True))
    cpos = spread(jnp.sum(cpos, axis=0, keepdims=True))
    cnn = spread(jnp.sum(cnn, axis=0, keepdims=True))

    def count(preds):
        def one(j, accs):
            t3 = s_scr[j].reshape(grp, SUBLANES, tq)
            return tuple(a + tree(jnp.where(p(t3), 1.0, 0.0), jnp.sum) for a, p in zip(accs, preds))

        def group(g, accs):
            for u in range(COUNT_UNROLL):
                accs = one(g * COUNT_UNROLL + u, accs)
            return accs

        groups = (i + 1) >> COUNT_UNROLL.bit_length() - 1
        accs = lax.fori_loop(0, groups, group, tuple(jnp.zeros((SUBLANES, tq), F32) for _ in preds))
        accs = lax.fori_loop(groups * COUNT_UNROLL, i + 1, one, accs)
        return [spread(jnp.sum(a, axis=0, keepdims=True)) for a in accs]

    n_adm = (((qcol8 >> 6) + 1 + i * (tq // CHUNK)) * CHUNK).astype(F32)
    search = n_adm > kf
    positive = cpos >= kf
    negative = cnn < kf
    lo = jnp.where(positive, TINY, jnp.where(negative, rmin, 0.0))
    clo = jnp.where(positive, cpos, jnp.where(negative, n_adm, cnn))
    st_scr[0] = jnp.where(search, lo, rmin)
    st_scr[1] = jnp.where(negative, -TINY, rmax + (jnp.abs(rmax) * (2.0 ** -10) + TINY))
    st_scr[2] = jnp.where(search, clo, n_adm)
    zero_thr = jnp.logical_and(jnp.logical_not(positive), jnp.logical_not(negative))
    st_scr[3] = jnp.where(jnp.logical_or(jnp.logical_not(search), jnp.logical_or(zero_thr, clo == kf)), 1.0, 0.0)

    def bis_cond(carry):
        it, pending = carry
        return jnp.logical_and(pending > 0.0, it < BISECT_CAP)

    def bis_step(it):
        lo, hi, clo, done = st_scr[0], st_scr[1], st_scr[2], st_scr[3]
        a = lax.bitcast_convert_type(jnp.abs(lo), jnp.int32)
        b = lax.bitcast_convert_type(jnp.abs(hi), jnp.int32)
        geo = lax.bitcast_convert_type(a + ((b - a) >> 1), F32)
        geo = jnp.where(hi > 0.0, geo, -geo)
        mid = jnp.where(it < BISECT_ARITH, lo + (hi - lo) * 0.5, geo)
        stuck = jnp.logical_or(mid <= lo, mid >= hi)
        cnt, = count([lambda t: t >= mid[None]])
        live = jnp.logical_and(done == 0.0, jnp.logical_not(stuck))
        up = jnp.logical_and(live, cnt >= kf)
        dn = jnp.logical_and(live, cnt < kf)
        lo = jnp.where(up, mid, lo)
        clo = jnp.where(up, cnt, clo)
        hi = jnp.where(dn, mid, hi)
        done = jnp.where(jnp.logical_or(stuck, clo == kf), 1.0, done)
        st_scr[0], st_scr[1], st_scr[2], st_scr[3] = lo, hi, clo, done
        return done

    def bis_body(carry):
        it, _ = carry
        bis_step(it)
        done = bis_step(it + 1)
        return it + 2, jnp.max(1.0 - done)

    lax.while_loop(bis_cond, bis_body, (jnp.int32(0), jnp.max(1.0 - st_scr[3])))
    thr = st_scr[0]
    clo = st_scr[2]

    @pl.when(jnp.max(clo) > kf)
    def _():
        cgt, = count([lambda t: t > thr[None]])
        need = kf - cgt
        tied_value = jnp.where(clo > kf, thr, jnp.nan)
        lower = (lax.broadcasted_iota(jnp.int32, (tk, tk), 1)
                 <= lax.broadcasted_iota(jnp.int32, (tk, tk), 0))
        prefix_matrix = jnp.where(lower, 1.0, 0.0).astype(BF16)

        def drop_pair(pair, seen):
            for u in range(2):
                j = 2 * pair + u
                t3 = s_scr[j].reshape(grp, SUBLANES, tq)
                tied = t3 == tied_value[None]
                ones = jnp.where(tied, 1.0, 0.0).reshape(tk, tq).astype(BF16)
                prefix = _dot(prefix_matrix, ones)
                rank = prefix.reshape(grp, SUBLANES, tq) + seen[None]
                kill = jnp.logical_and(tied, rank > need[None])
                s_scr[j] = jnp.where(kill, -jnp.inf, t3).reshape(tk, tq)
                seen = seen + spread(prefix[tk - 1:tk, :])
            return seen

        lax.fori_loop(0, (i + 2) >> 1, drop_pair, jnp.zeros((SUBLANES, tq), F32))

    hrow = lax.broadcasted_iota(jnp.int32, (LANES, tq), 0)
    for h in range(DSA_HEADS):
        slab = qt_ref[0, (h // 2) * LANES:(h // 2 + 1) * LANES, :]
        mine = (hrow >= DSA_HEAD_DIM) if (h % 2) else (hrow < DSA_HEAD_DIM)
        qm_scr[h] = jnp.where(mine, slab, jnp.zeros_like(slab))
        m_scr[h] = jnp.full((SUBLANES, tq), NEG_BIG, F32)
        acc_scr[h] = jnp.zeros((ACC_ROWS, tq), F32)
    ones_rows = jnp.ones((ACC_ROWS - LANES, tk), BF16)

    def slab(h):
        return slice((h // 2) * LANES, (h // 2 + 1) * LANES)

    def tile_operands(j):
        t3 = s_scr[j].reshape(grp, SUBLANES, tq)
        bias = jnp.where(t3 >= thr[None], 0.0, NEG_BIG).reshape(tk, tq)
        return k_ref[0, pl.ds(pl.multiple_of(j * tk, tk), tk), :], bias

    def masked_scores(keys, bias, h):
        return _dot(keys[:, slab(h)], qm_scr[h]) + bias

    keys0, bias0 = tile_operands(0)
    for h in range(DSA_HEADS):
        sn_scr[h] = masked_scores(keys0, bias0, h)

    def att_body(j, _):
        keys_n, bias_n = tile_operands(jnp.minimum(j + 1, i))
        for h in range(DSA_HEADS):
            s3 = sn_scr[h].reshape(grp, SUBLANES, tq)
            m_prev = m_scr[h]
            m_new = jnp.maximum(m_prev, spread(jnp.max(tree(s3, jnp.max), axis=0, keepdims=True)))
            p = jnp.exp2((s3 - m_new[None]).reshape(tk, tq).astype(BF16))
            corr = jnp.exp2(m_prev - m_new)
            pv = _dot(jnp.concatenate([vt_ref[0, j, slab(h), :], ones_rows], axis=0), p)
            acc = acc_scr[h].reshape(ACC_ROWS // SUBLANES, SUBLANES, tq) * corr[None]
            acc_scr[h] = acc.reshape(ACC_ROWS, tq) + pv
            m_scr[h] = m_new
            sn_scr[h] = masked_scores(keys_n, bias_n, h)
        return 0

    lax.fori_loop(0, i + 1, att_body, 0)

    def normalised(h):
        inv = 1.0 / acc_scr[h, LANES:LANES + SUBLANES, :]
        return (acc_scr[h, 0:LANES, :].reshape(LANES // SUBLANES, SUBLANES, tq) * inv[None]).reshape(LANES, tq)

    for sp in range(DSA_HEADS // 2):
        pair = jnp.where(hrow < DSA_HEAD_DIM, normalised(2 * sp), normalised(2 * sp + 1))
        o_ref[0, sp * LANES:(sp + 1) * LANES, :] = pair.astype(o_ref.dtype)


def _dsa(iqt, ik, iwt, qt, k, vt):
    B, T, _ = k.shape
    tq = DSA_TILE
    nq = T // tq
    topk = min(DSA_TOPK_MAX, T // 4)
    kern = functools.partial(_dsa_kernel, topk=topk)
    return pl.pallas_call(
        kern,
        grid=(B, nq),
        in_specs=[
            pl.BlockSpec((1, 1, IDX_DIM, IDX_HEADS * tq), lambda b, i: (b, i, 0, 0)),
            pl.BlockSpec((1, T, IDX_DIM), lambda b, i: (b, 0, 0)),
            pl.BlockSpec((1, IDX_HEADS, tq), lambda b, i: (b, 0, i)),
            pl.BlockSpec((1, DSA_WIDTH, tq), lambda b, i: (b, 0, i)),
            pl.BlockSpec((1, T, DSA_WIDTH), lambda b, i: (b, 0, 0)),
            pl.BlockSpec((1, nq, DSA_WIDTH, tq), lambda b, i: (b, 0, 0, 0)),
        ],
        out_specs=pl.BlockSpec((1, DSA_WIDTH, tq), lambda b, i: (b, 0, i)),
        out_shape=jax.ShapeDtypeStruct((B, DSA_WIDTH, T), BF16),
        scratch_shapes=[
            pltpu.VMEM((nq + 1, tq, tq), F32),
            pltpu.VMEM((DSA_HEADS, LANES, tq), BF16),
            pltpu.VMEM((4, SUBLANES, tq), F32),
            pltpu.VMEM((DSA_HEADS, SUBLANES, tq), F32),
            pltpu.VMEM((DSA_HEADS, ACC_ROWS, tq), F32),
            pltpu.VMEM((DSA_HEADS, tq, tq), F32),
            pltpu.VMEM((tq, IDX_HEADS * tq), F32),
        ],
        compiler_params=_params(("parallel", "arbitrary")),
    )(iqt, ik, iwt, qt, k, vt)


def _gla_kernel(q_ref, k_ref, v_ref, r_ref, la_ref, g_ref, o_ref, s_scr):
    rows = GLA_BLOCK
    nch = rows // CHUNK
    npair = GLA_HEADS // 2
    nsub = q_ref.shape[1] // rows

    @pl.when(pl.program_id(1) == 0)
    def _():
        s_scr[...] = jnp.zeros_like(s_scr)

    ri = lax.broadcasted_iota(jnp.int32, (rows, rows), 0)
    ci = lax.broadcasted_iota(jnp.int32, (rows, rows), 1)
    causal = jnp.logical_and((ri >> 6) == (ci >> 6), ci <= ri)
    tri = jnp.where(causal, 1.0, 0.0).astype(BF16)
    lane = lax.broadcasted_iota(jnp.int32, (rows, LANES), 1)
    eye = lax.broadcasted_iota(jnp.int32, (LANES, LANES), 0) == lax.broadcasted_iota(jnp.int32, (LANES, LANES), 1)
    top_rows = lax.broadcasted_iota(jnp.int32, (LANES, GLA_DV), 0) < GLA_DK

    def decayed(r0):
        la = la_ref[0, r0:r0 + rows, :]
        la_hi = la.astype(BF16)
        rem = la - la_hi.astype(F32)
        la_mid = rem.astype(BF16)
        la_lo = (rem - la_mid.astype(F32)).astype(BF16)
        b = _dot(tri, la_hi) + _dot(tri, la_mid) + _dot(tri, la_lo)
        b_last = jnp.concatenate(
            [jnp.broadcast_to(b[(c + 1) * CHUNK - 1:(c + 1) * CHUNK, :], (CHUNK, b.shape[1])) for c in range(nch)],
            axis=0)
        q = q_ref[0, r0:r0 + rows, :]
        k = k_ref[0, r0:r0 + rows, :]
        return q * jnp.exp(b), (k * jnp.exp(-b)).astype(BF16), (k * jnp.exp(b_last - b)).astype(BF16), jnp.exp(b_last)

    subs = [decayed(sb * rows) for sb in range(nsub)]

    for pr in range(npair):
        sl = slice(pr * LANES, (pr + 1) * LANES)
        heads = (2 * pr, 2 * pr + 1)
        state = s_scr[pr]
        for sb in range(nsub):
            r0 = sb * rows
            qg, kg, kd, decay_rows = subs[sb]
            qg_p = qg[:, sl]
            kg_p = kg[:, sl]
            kd_p = kd[:, sl]
            qg_h = [jnp.where(lane < GLA_DK, qg_p, 0.0).astype(BF16), jnp.where(lane >= GLA_DK, qg_p, 0.0).astype(BF16)]
            v_h = [v_ref[0, r0:r0 + rows, h * GLA_DV:(h + 1) * GLA_DV].astype(BF16) for h in heads]
            o_h = []
            for t in range(2):
                a = jnp.where(causal, _dot_nt(qg_h[t], kg_p), 0.0)
                o_h.append(_dot(a.astype(BF16), v_h[t]))
            inter = [[], []]
            for c in range(nch):
                rs = slice(c * CHUNK, (c + 1) * CHUNK)
                state_b = state.astype(BF16)
                for t in range(2):
                    inter[t].append(_dot(qg_h[t][rs], state_b))
                kv = jnp.where(top_rows, _dot_tn(kd_p[rs], v_h[0][rs]), _dot_tn(kd_p[rs], v_h[1][rs]))
                drow = jnp.broadcast_to(decay_rows[c * CHUNK:c * CHUNK + 1, sl], (LANES, LANES))
                dcol = jnp.sum(jnp.where(eye, drow, 0.0), axis=1, keepdims=True)
                state = dcol * state + kv
            for t in range(2):
                h = heads[t]
                o = o_h[t] + jnp.concatenate(inter[t], axis=0)
                ms = jnp.mean(o * o, axis=1, keepdims=True)
                on = o * lax.rsqrt(ms + LN_EPS) * g_ref[...]
                r = r_ref[0, r0:r0 + rows, h * GLA_DV:(h + 1) * GLA_DV]
                o_ref[0, r0:r0 + rows, h * GLA_DV:(h + 1) * GLA_DV] = (on * (r / (1.0 + jnp.exp(-r)))).astype(o_ref.dtype)
        s_scr[pr] = state


def _gla(bq, bk, bv, br, la, g):
    B, T, _ = bq.shape
    rows = GLA_STEP
    blk = lambda w: pl.BlockSpec((1, rows, w), lambda b, i: (b, i, 0))
    return pl.pallas_call(
        _gla_kernel,
        grid=(B, T // rows),
        in_specs=[blk(256), blk(256), blk(512), blk(512), blk(256), pl.BlockSpec((1, GLA_DV), lambda b, i: (0, 0))],
        out_specs=blk(GLA_WIDTH),
        out_shape=jax.ShapeDtypeStruct((B, T, GLA_WIDTH), BF16),
        scratch_shapes=[pltpu.VMEM((GLA_HEADS // 2, 2 * GLA_DK, GLA_DV), F32)],
        compiler_params=_params(("parallel", "arbitrary")),
    )(bq, bk, bv, br, la, g)


def _layer_norm(x, g, b):
    mu = jnp.mean(x, axis=1, keepdims=True)
    xc = x - mu
    var = jnp.mean(xc * xc, axis=1, keepdims=True)
    return xc * lax.rsqrt(var + LN_EPS) * g + b


def _out_proj_kernel(x_ref, ya_ref, yb_ref, wo_ref, g_ref, b_ref, wr_hi_ref, wr_lo_ref, br_ref, h_ref, route_ref):
    tm = x_ref.shape[0]
    mix = _dot(ya_ref[...], wo_ref[0:DSA_WIDTH, :]) + _dot(yb_ref[...], wo_ref[DSA_WIDTH:D_MODEL, :])
    h = _layer_norm(DEEPNORM_ALPHA * x_ref[...] + mix, g_ref[...], b_ref[...])
    h_ref[...] = h

    h_hi = h.astype(BF16)
    h_lo = (h - h_hi.astype(F32)).astype(BF16)
    logits = _dot(h_hi, wr_hi_ref[...]) + _dot(h_lo, wr_hi_ref[...]) + _dot(h_hi, wr_lo_ref[...]) + br_ref[...]

    lane = lax.broadcasted_iota(jnp.int32, (tm, LANES), 1)
    lanef = lane.astype(F32)
    gl = jnp.where(lane < N_GROUPS, logits, -jnp.inf)
    gmax = jnp.max(gl, axis=1, keepdims=True)
    gsel = jnp.min(jnp.where(gl == gmax, lanef, 1e9), axis=1, keepdims=True)
    pg = 1.0 / jnp.sum(jnp.exp(gl - gmax), axis=1, keepdims=True)
    egrp = ((lane - N_GROUPS) >> 3).astype(F32)
    in_grp = jnp.logical_and(jnp.logical_and(lane >= N_GROUPS, lane < N_GROUPS + N_EXPERTS), egrp == gsel)
    el = jnp.where(in_grp, logits, -jnp.inf)
    t1 = jnp.max(el, axis=1, keepdims=True)
    i1 = jnp.min(jnp.where(el == t1, lanef, 1e9), axis=1, keepdims=True)
    el2 = jnp.where(lanef == i1, -jnp.inf, el)
    t2 = jnp.max(el2, axis=1, keepdims=True)
    i2 = jnp.min(jnp.where(el2 == t2, lanef, 1e9), axis=1, keepdims=True)
    e21 = jnp.exp(t2 - t1)
    g1 = pg / (1.0 + e21)
    g2 = pg * e21 / (1.0 + e21)
    route = jnp.where(lane == 0, i1 - N_GROUPS, 0.0)
    route = jnp.where(lane == 1, i2 - N_GROUPS, route)
    route = jnp.where(lane == 2, g1, route)
    route = jnp.where(lane == 3, g2, route)
    route_ref[...] = route


def _out_proj(x2, ya, yb, wo, g1, b1, wr_hi, wr_lo, br):
    N = x2.shape[0]
    tm = TOK_TILE
    row = lambda i: (i, 0)
    const = lambda i: (0, 0)
    return pl.pallas_call(
        _out_proj_kernel,
        grid=(N // tm,),
        in_specs=[
            pl.BlockSpec((tm, D_MODEL), row), pl.BlockSpec((tm, DSA_WIDTH), row), pl.BlockSpec((tm, GLA_WIDTH), row),
            pl.BlockSpec(wo.shape, const), pl.BlockSpec(g1.shape, const), pl.BlockSpec(b1.shape, const),
            pl.BlockSpec(wr_hi.shape, const), pl.BlockSpec(wr_lo.shape, const), pl.BlockSpec(br.shape, const),
        ],
        out_specs=[pl.BlockSpec((tm, D_MODEL), row), pl.BlockSpec((tm, LANES), row)],
        out_shape=[jax.ShapeDtypeStruct((N, D_MODEL), F32), jax.ShapeDtypeStruct((N, LANES), F32)],
        compiler_params=_params(("parallel",)),
    )(x2, ya, yb, wo, g1, b1, wr_hi, wr_lo, br)


def _rank_kernel(route_ref, rank_ref, cnt_ref, carry_scr):
    tm = route_ref.shape[0]

    @pl.when(pl.program_id(0) == 0)
    def _():
        carry_scr[...] = jnp.zeros_like(carry_scr)

    route = route_ref[...]
    lanef = lax.broadcasted_iota(jnp.int32, (tm, LANES), 1).astype(F32)
    e1 = route[:, 0:1]
    e2 = route[:, 1:2]
    hit1 = lanef == e1
    hit2 = lanef == e2
    onehot = jnp.where(jnp.logical_or(hit1, hit2), 1.0, 0.0).astype(BF16)
    ri = lax.broadcasted_iota(jnp.int32, (tm, tm), 0)
    ci = lax.broadcasted_iota(jnp.int32, (tm, tm), 1)
    before = jnp.where(ci < ri, 1.0, 0.0).astype(BF16)
    prefix = _dot(before, onehot) + carry_scr[0:1, :]
    r1 = jnp.sum(jnp.where(hit1, prefix, 0.0), axis=1, keepdims=True)
    r2 = jnp.sum(jnp.where(hit2, prefix, 0.0), axis=1, keepdims=True)
    rank_ref[...] = jnp.where(lanef == 0.0, r1, jnp.where(lanef == 1.0, r2, 0.0))
    total = _dot(jnp.ones((8, tm), BF16), onehot)
    carry_scr[...] = carry_scr[...] + total
    cnt_ref[...] = carry_scr[...]


def _rank(route):
    N = route.shape[0]
    tm = RANK_TILE
    return pl.pallas_call(
        _rank_kernel,
        grid=(N // tm,),
        in_specs=[pl.BlockSpec((tm, LANES), lambda i: (i, 0))],
        out_specs=[pl.BlockSpec((tm, LANES), lambda i: (i, 0)), pl.BlockSpec((8, LANES), lambda i: (0, 0))],
        out_shape=[jax.ShapeDtypeStruct((N, LANES), F32), jax.ShapeDtypeStruct((8, LANES), F32)],
        scratch_shapes=[pltpu.VMEM((8, LANES), F32)],
        compiler_params=_params(("arbitrary",)),
    )(route)


def _dispatch_kernel(pos_ref, h_ref, xs_in_ref, xs_ref, hbuf, pbuf, load_sem, row_sem):
    del xs_in_ref
    tm = DISPATCH_TILE
    nbuf = hbuf.shape[0]
    i = pl.program_id(0)
    last = pl.num_programs(0) - 1

    def tile_load(step):
        return pltpu.make_async_copy(h_ref.at[pl.ds(step * tm, tm), :], hbuf.at[step % nbuf], load_sem.at[step % nbuf])

    def row_copy(step, r, slot):
        p = pos_ref[2 * (step * tm + r) + slot]
        return pltpu.make_async_copy(pbuf.at[step % nbuf, pl.ds(r, 1), :], xs_ref.at[pl.ds(p, 1), :],
                                     row_sem.at[step % nbuf])

    def for_rows(fn):
        def body(r, _):
            fn(r, 0)
            fn(r, 1)
            return 0
        lax.fori_loop(0, tm, body, 0, unroll=DMA_UNROLL)

    @pl.when(i == 0)
    def _():
        tile_load(0).start()

    @pl.when(i < last)
    def _():
        tile_load(i + 1).start()

    tile_load(i).wait()
    pbuf[i % nbuf] = _pack_halves(hbuf[i % nbuf])
    for_rows(lambda r, slot: row_copy(i, r, slot).start())

    @pl.when(i > 0)
    def _():
        for_rows(lambda r, slot: row_copy(i - 1, r, slot).wait())

    @pl.when(i == last)
    def _():
        for_rows(lambda r, slot: row_copy(i, r, slot).wait())


def _dispatch(pos_flat, h, xs_init):
    N = h.shape[0]
    grid_spec = pltpu.PrefetchScalarGridSpec(
        num_scalar_prefetch=1,
        grid=(N // DISPATCH_TILE,),
        in_specs=[pl.BlockSpec(memory_space=pl.ANY), pl.BlockSpec(memory_space=pl.ANY)],
        out_specs=pl.BlockSpec(memory_space=pl.ANY),
        scratch_shapes=[pltpu.VMEM((3, DISPATCH_TILE, D_MODEL), F32),
                        pltpu.VMEM((3, DISPATCH_TILE, D_MODEL // 2), jnp.uint32),
                        pltpu.SemaphoreType.DMA((3,)), pltpu.SemaphoreType.DMA((3,))],
    )
    return pl.pallas_call(
        _dispatch_kernel,
        grid_spec=grid_spec,
        out_shape=jax.ShapeDtypeStruct(xs_init.shape, xs_init.dtype),
        input_output_aliases={2: 0},
        compiler_params=_params(("arbitrary",)),
    )(pos_flat, h, xs_init)


def _ffn_kernel(te_ref, na_ref, x_ref, wi_ref, wo_ref, y_ref, wi_bf, wo_bf):
    i = pl.program_id(0)
    active = i < na_ref[0]
    fresh = jnp.logical_or(i == 0, te_ref[i] != te_ref[jnp.maximum(i - 1, 0)])

    @pl.when(jnp.logical_and(active, fresh))
    def _():
        wi_bf[...] = wi_ref[0].astype(BF16)
        wo_bf[...] = wo_ref[0].astype(BF16)

    @pl.when(active)
    def _():
        x_left, x_right = _unpack_halves(x_ref[...])
        half = D_MODEL // 2
        hid = (_dot(x_left.astype(BF16), wi_bf[0:half, :])
               + _dot(x_right.astype(BF16), wi_bf[half:D_MODEL, :]))
        hg = hid[:, :D_EXPERT]
        hu = hid[:, D_EXPERT:]
        act = (hg / (1.0 + jnp.exp(-hg))) * hu
        y_ref[...] = _pack_halves(_dot(act.astype(BF16), wo_bf[...]))

    @pl.when(jnp.logical_not(active))
    def _():
        y_ref[...] = jnp.zeros_like(y_ref)


def _ffn(tile_expert, n_active, xs, w_e_in, w_e_out):
    R = xs.shape[0]
    tm = MOE_TILE

    def live(i, te, na):
        return jnp.minimum(i, na[0] - 1)

    grid_spec = pltpu.PrefetchScalarGridSpec(
        num_scalar_prefetch=2,
        grid=(R // tm,),
        in_specs=[
            pl.BlockSpec((tm, D_MODEL // 2), lambda i, te, na: (live(i, te, na), 0)),
            pl.BlockSpec((1, D_MODEL, 2 * D_EXPERT), lambda i, te, na: (te[live(i, te, na)], 0, 0)),
            pl.BlockSpec((1, D_EXPERT, D_MODEL), lambda i, te, na: (te[live(i, te, na)], 0, 0)),
        ],
        out_specs=pl.BlockSpec((tm, D_MODEL // 2), lambda i, te, na: (i, 0)),
        scratch_shapes=[pltpu.VMEM((D_MODEL, 2 * D_EXPERT), BF16), pltpu.VMEM((D_EXPERT, D_MODEL), BF16)],
    )
    return pl.pallas_call(
        _ffn_kernel,
        grid_spec=grid_spec,
        out_shape=jax.ShapeDtypeStruct((R, D_MODEL // 2), jnp.uint32),
        compiler_params=_params(("arbitrary",)),
    )(tile_expert, n_active, xs, w_e_in, w_e_out)


def _combine_kernel(pos_ref, h_ref, route_ref, g_ref, b_ref, ys_ref, o_ref, buf, sem):
    tm = h_ref.shape[0]
    i = pl.program_id(0)
    cur = i % 2

    def row_copy(step, r, slot):
        p = pos_ref[2 * (step * tm + r) + slot]
        half = step % 2
        return pltpu.make_async_copy(ys_ref.at[pl.ds(p, 1), :], buf.at[half, slot, pl.ds(r, 1), :], sem.at[half])

    def for_rows(fn):
        def body(r, _):
            fn(r, 0)
            fn(r, 1)
            return 0
        lax.fori_loop(0, tm, body, 0, unroll=DMA_UNROLL)

    @pl.when(i == 0)
    def _():
        for_rows(lambda r, slot: row_copy(0, r, slot).start())

    @pl.when(i + 1 < pl.num_programs(0))
    def _():
        for_rows(lambda r, slot: row_copy(i + 1, r, slot).start())

    for_rows(lambda r, slot: row_copy(i, r, slot).wait())
    route = route_ref[...]
    l0, r0 = _unpack_halves(buf[cur, 0])
    l1, r1 = _unpack_halves(buf[cur, 1])
    g0, g1 = route[:, 2:3], route[:, 3:4]
    ffn = jnp.concatenate([l0 * g0 + l1 * g1, r0 * g0 + r1 * g1], axis=1)
    o_ref[...] = _layer_norm(DEEPNORM_ALPHA * h_ref[...] + ffn, g_ref[...], b_ref[...])


def _combine(pos_flat, h, route, g2, b2, ys):
    N = h.shape[0]
    tm = TOK_TILE
    row = lambda i, pos: (i, 0)
    const = lambda i, pos: (0, 0)
    grid_spec = pltpu.PrefetchScalarGridSpec(
        num_scalar_prefetch=1,
        grid=(N // tm,),
        in_specs=[
            pl.BlockSpec((tm, D_MODEL), row), pl.BlockSpec((tm, LANES), row),
            pl.BlockSpec(g2.shape, const), pl.BlockSpec(b2.shape, const),
            pl.BlockSpec(memory_space=pl.ANY),
        ],
        out_specs=pl.BlockSpec((tm, D_MODEL), row),
        scratch_shapes=[pltpu.VMEM((2, 2, tm, D_MODEL // 2), jnp.uint32), pltpu.SemaphoreType.DMA((2,))],
    )
    return pl.pallas_call(
        _combine_kernel,
        grid_spec=grid_spec,
        out_shape=jax.ShapeDtypeStruct((N, D_MODEL), F32),
        compiler_params=_params(("arbitrary",)),
    )(pos_flat, h, route, g2, b2, ys)


def _rope_tables(T, dim):
    half = dim // 2
    inv = 1.0 / (ROPE_THETA ** (jnp.arange(half, dtype=F32) / half))
    ang = jnp.arange(T).astype(F32)[:, None] * inv[None, :]
    cos = jnp.cos(ang)
    sin = jnp.sin(ang)
    reps = LANES // dim
    cos_t = jnp.tile(jnp.concatenate([cos, cos], axis=1), (1, reps))
    sin_t = jnp.tile(jnp.concatenate([-sin, sin], axis=1), (1, reps))
    return cos_t, sin_t


def _pad_cols(w, width):
    return jnp.pad(w, ((0, 0), (0, width - w.shape[1])))


def _layer(x, w_in, w_gla_gate, b_gla_gate, g_gla_norm, w_out, ln1_g, ln1_b,
           w_gr, b_gr, w_er, b_er, w_e_in, w_e_out, ln2_g, ln2_b):
    B, T, D = x.shape
    N = B * T
    assert D == D_MODEL and T % DSA_TILE == 0 and N % DISPATCH_TILE == 0 and DISPATCH_TILE % RANK_TILE == 0
    x2 = x.reshape(N, D)

    sizes = (512, 512, 512, 256, 32, 8, 256, 256, 512, 512, 16)
    offs = np.concatenate([[0], np.cumsum(sizes)])
    col = lambda k: w_in[:, offs[k]:offs[k + 1]]
    wa = jnp.concatenate([col(0) * (DSA_HEAD_DIM ** -0.5 * LOG2E), col(1), col(2)], axis=1).astype(BF16)
    wi = _pad_cols(jnp.concatenate([col(3), col(4), col(5) * IDX_SCALE], axis=1), 3 * LANES).astype(BF16)
    wb = jnp.concatenate([col(6) * (GLA_DK ** -0.5), col(7), col(8), col(9)], axis=1).astype(BF16)
    wg = _pad_cols(col(10), LANES).astype(BF16)
    wgate = jnp.pad(w_gla_gate, ((0, LANES - GLA_GATE_RANK), (0, 0))).astype(BF16)
    bgate = b_gla_gate.reshape(1, -1)
    cosa, sina = _rope_tables(T, DSA_HEAD_DIM)
    cosi, sini = _rope_tables(T, IDX_DIM)

    q, k, v, iq, ikw, bq, bk, bv, br, la = _in_proj(x2, wa, wi, wb, wg, wgate, bgate, cosa, sina, cosi, sini, T)

    nq = T // DSA_TILE
    iqt = iq.reshape(B, nq, DSA_TILE, IDX_HEADS, IDX_DIM).transpose(0, 1, 4, 3, 2)
    iqt = iqt.reshape(B, nq, IDX_DIM, IDX_HEADS * DSA_TILE)
    ik = ikw[:, :IDX_DIM].astype(BF16).reshape(B, T, IDX_DIM)
    iwt = ikw[:, IDX_DIM:IDX_DIM + IDX_HEADS].reshape(B, T, IDX_HEADS).transpose(0, 2, 1)
    qt = q.reshape(B, T, DSA_WIDTH).transpose(0, 2, 1)
    vt = v.reshape(B, nq, DSA_TILE, DSA_WIDTH).transpose(0, 1, 3, 2)
    ya = _dsa(iqt, ik, iwt, qt, k.reshape(B, T, DSA_WIDTH), vt).transpose(0, 2, 1)

    r3 = lambda a: a.reshape(B, T, a.shape[-1])
    yb = _gla(r3(bq), r3(bk), r3(bv), r3(br), r3(la), g_gla_norm.reshape(1, GLA_DV))

    wr = _pad_cols(jnp.concatenate([w_gr, w_er], axis=1), LANES)
    wr_hi = wr.astype(BF16)
    wr_lo = (wr - wr_hi.astype(F32)).astype(BF16)
    brt = _pad_cols(jnp.concatenate([b_gr, b_er]).reshape(1, -1), LANES)
    h, route = _out_proj(x2, ya.reshape(N, DSA_WIDTH), yb.reshape(N, GLA_WIDTH), w_out.astype(BF16),
                         ln1_g.reshape(1, D), ln1_b.reshape(1, D), wr_hi, wr_lo, brt)

    rank, cnt = _rank(route)
    counts = cnt[0, :N_EXPERTS].astype(jnp.int32)
    padded = ((counts + MOE_TILE - 1) // MOE_TILE) * MOE_TILE
    ends = jnp.cumsum(padded)
    starts = ends - padded
    eid = route[:, 0:2].astype(jnp.int32)
    pos = (starts[eid] + rank[:, 0:2].astype(jnp.int32)).reshape(-1)
    n_rows = 2 * N + N_EXPERTS * MOE_TILE
    n_tiles = n_rows // MOE_TILE
    tile_start = jnp.arange(n_tiles, dtype=jnp.int32) * MOE_TILE
    tile_expert = jnp.minimum(jnp.sum(tile_start[:, None] >= ends[None, :], axis=1), N_EXPERTS - 1).astype(jnp.int32)
    n_active = (ends[-1] // MOE_TILE).astype(jnp.int32).reshape(1)

    xs = _dispatch(pos, h, jnp.zeros((n_rows, D // 2), jnp.uint32))
    ys = _ffn(tile_expert, n_active, xs, w_e_in, w_e_out)
    out = _combine(pos, h, route, ln2_g.reshape(1, D), ln2_b.reshape(1, D), ys)
    return out.reshape(B, T, D)


def kernel(x, w_in, w_gla_gate, b_gla_gate, g_gla_norm, w_out, ln1_g, ln1_b, w_group_router, b_group_router,
           w_expert_router, b_expert_router, w_expert_in, w_expert_out, ln2_g, ln2_b):
    h = x
    for l in range(w_in.shape[0]):
        h = _layer(h, w_in[l], w_gla_gate[l], b_gla_gate[l], g_gla_norm[l], w_out[l], ln1_g[l], ln1_b[l],
                   w_group_router[l], b_group_router[l], w_expert_router[l], b_expert_router[l],
                   w_expert_in[l], w_expert_out[l], ln2_g[l], ln2_b[l])
    return h
```

```python
import functools

import numpy as np
import jax
import jax.numpy as jnp
from jax import lax
from jax.experimental import pallas as pl
from jax.experimental.pallas import tpu as pltpu

F32 = jnp.float32
BF16 = jnp.bfloat16

D_MODEL = 1024
CHUNK = 64
ROPE_THETA = 10000.0
LN_EPS = 1e-5
DSA_WIDTH = 512
DSA_HEAD_DIM = 64
DSA_HEADS = 8
IDX_HEADS = 8
IDX_DIM = 32
IDX_SCALE = (IDX_HEADS * IDX_DIM) ** -0.5
DSA_TOPK_MAX = 256
GLA_WIDTH = 512
GLA_HEADS = 4
GLA_DV = 128
GLA_DK = 64
GLA_GATE_RANK = 16
GLA_TAU = 16.0
N_GROUPS = 4
EXPERTS_PER_GROUP = 8
N_EXPERTS = 32
D_EXPERT = 512
DEEPNORM_ALPHA = 2.0 ** 0.25

LANES = 128
SUBLANES = 8
TINY = 2.0 ** -126
NEG_BIG = -1e30
ACC_ROWS = LANES + 16
VMEM_LIMIT = 56 * 1024 * 1024

TOK_TILE = 1024
DSA_TILE = 256
GLA_BLOCK = 256
GLA_STEP = 1024
MOE_TILE = 512
RANK_TILE = 512
DISPATCH_TILE = 512
DMA_UNROLL = 8
LOG2E = 1.4426950408889634
COUNT_UNROLL = 4
BISECT_ARITH = 40
BISECT_CAP = 80


def _dot(a, b):
    return jnp.dot(a, b, preferred_element_type=F32)


def _dot_nt(a, b):
    return lax.dot_general(a, b, (((1,), (1,)), ((), ())), preferred_element_type=F32)


def _dot_tn(a, b):
    return lax.dot_general(a, b, (((0,), (0,)), ((), ())), preferred_element_type=F32)


def _pack_halves(x):
    w = x.shape[1] // 2
    hi = lax.bitcast_convert_type(x[:, :w].astype(BF16).astype(F32), jnp.uint32)
    lo = lax.bitcast_convert_type(x[:, w:].astype(BF16).astype(F32), jnp.uint32)
    return (hi & jnp.uint32(0xFFFF0000)) | (lo >> 16)


def _unpack_halves(p):
    left = lax.bitcast_convert_type(p & jnp.uint32(0xFFFF0000), F32)
    right = lax.bitcast_convert_type(p << 16, F32)
    return left, right


def _params(sem):
    return pltpu.CompilerParams(dimension_semantics=sem, vmem_limit_bytes=VMEM_LIMIT)


def _rope_slab(slab, cos, sin, first_half, half):
    swapped = jnp.where(first_half, pltpu.roll(slab, LANES - half, 1), pltpu.roll(slab, half, 1))
    return slab * cos + swapped * sin


def _in_proj_kernel(x_ref, wa_ref, wi_ref, wb_ref, wg_ref, wgate_ref, bgate_ref,
                    cosa_ref, sina_ref, cosi_ref, sini_ref,
                    q_ref, k_ref, v_ref, iq_ref, ikw_ref, bq_ref, bk_ref, bv_ref, br_ref, la_ref):
    tm = x_ref.shape[0]
    xb = x_ref[...].astype(BF16)
    lane = lax.broadcasted_iota(jnp.int32, (tm, LANES), 1)

    a = _dot(xb, wa_ref[...])
    cosa, sina = cosa_ref[...], sina_ref[...]
    first_a = (lane & (DSA_HEAD_DIM - 1)) < DSA_HEAD_DIM // 2
    for c in range(DSA_WIDTH // LANES):
        sl = slice(c * LANES, (c + 1) * LANES)
        q_ref[:, sl] = _rope_slab(a[:, sl], cosa, sina, first_a, DSA_HEAD_DIM // 2).astype(BF16)
        ks = slice(DSA_WIDTH + c * LANES, DSA_WIDTH + (c + 1) * LANES)
        k_ref[:, sl] = _rope_slab(a[:, ks], cosa, sina, first_a, DSA_HEAD_DIM // 2).astype(BF16)
    v_ref[...] = a[:, 2 * DSA_WIDTH:3 * DSA_WIDTH].astype(BF16)

    ii = _dot(xb, wi_ref[...])
    cosi, sini = cosi_ref[...], sini_ref[...]
    first_i = (lane & (IDX_DIM - 1)) < IDX_DIM // 2
    for c in range(2):
        sl = slice(c * LANES, (c + 1) * LANES)
        iq_ref[:, sl] = _rope_slab(ii[:, sl], cosi, sini, first_i, IDX_DIM // 2).astype(BF16)
    last = ii[:, 2 * LANES:3 * LANES]
    ikw_ref[...] = jnp.where(lane < IDX_DIM, _rope_slab(last, cosi, sini, first_i, IDX_DIM // 2), last)

    b = _dot(xb, wb_ref[...])
    bq_ref[...] = b[:, 0:256]
    bk_ref[...] = b[:, 256:512]
    bv_ref[...] = b[:, 512:1024]
    br_ref[...] = b[:, 1024:1536]

    g = _dot(xb, wg_ref[...])
    z = _dot(g.astype(BF16), wgate_ref[...]) + bgate_ref[...]
    log_sig = jnp.minimum(z, 0.0) - jnp.log(1.0 + jnp.exp(-jnp.abs(z)))
    la_ref[...] = log_sig * (1.0 / GLA_TAU)


def _in_proj(x2, wa, wi, wb, wg, wgate, bgate, cosa, sina, cosi, sini, T):
    N = x2.shape[0]
    tm = TOK_TILE
    nt = T // tm
    row = lambda i: (i, 0)
    const = lambda i: (0, 0)
    pos = lambda i: (i % nt, 0)
    outs = [
        (DSA_WIDTH, BF16), (DSA_WIDTH, BF16), (DSA_WIDTH, BF16), (IDX_HEADS * IDX_DIM, BF16), (LANES, F32),
        (256, F32), (256, F32), (512, F32), (512, F32), (256, F32),
    ]
    return pl.pallas_call(
        _in_proj_kernel,
        grid=(N // tm,),
        in_specs=[
            pl.BlockSpec((tm, D_MODEL), row),
            pl.BlockSpec(wa.shape, const), pl.BlockSpec(wi.shape, const), pl.BlockSpec(wb.shape, const),
            pl.BlockSpec(wg.shape, const), pl.BlockSpec(wgate.shape, const), pl.BlockSpec(bgate.shape, const),
            pl.BlockSpec((tm, LANES), pos), pl.BlockSpec((tm, LANES), pos),
            pl.BlockSpec((tm, LANES), pos), pl.BlockSpec((tm, LANES), pos),
        ],
        out_specs=[pl.BlockSpec((tm, w), row) for w, _ in outs],
        out_shape=[jax.ShapeDtypeStruct((N, w), dt) for w, dt in outs],
        compiler_params=_params(("parallel",)),
    )(x2, wa, wi, wb, wg, wgate, bgate, cosa, sina, cosi, sini)


def _dsa_kernel(iqt_ref, ik_ref, iwt_ref, qt_ref, k_ref, vt_ref, o_ref,
                s_scr, qm_scr, st_scr, m_scr, acc_scr, sn_scr, rt_scr, *, topk):
    tq = DSA_TILE
    tk = DSA_TILE
    grp = tk // SUBLANES
    i = pl.program_id(1)
    kf = jnp.float32(topk)

    iqt = iqt_ref[0, 0]
    iwt = iwt_ref[0]
    krow = lax.broadcasted_iota(jnp.int32, (tk, tq), 0)
    qcol = lax.broadcasted_iota(jnp.int32, (tk, tq), 1)
    qcol8 = lax.broadcasted_iota(jnp.int32, (SUBLANES, tq), 1)

    def tree(x3, op):
        q4 = grp // 4
        return op(jnp.stack([op(x3[a * q4:(a + 1) * q4], axis=0) for a in range(4)]), axis=0)

    def fold(x, op):
        return tree(x.reshape(grp, SUBLANES, tq), op)

    def spread(x):
        return jnp.broadcast_to(x, (SUBLANES, tq))

    def head_products(j):
        keys = ik_ref[0, pl.ds(pl.multiple_of(j * tk, tk), tk), :]
        return _dot(keys, iqt)

    def weighted_relu_sum():
        acc = None
        for h in range(IDX_HEADS):
            term = jnp.maximum(rt_scr[:, h * tq:(h + 1) * tq], 0.0) * iwt[h:h + 1, :]
            acc = term if acc is None else acc + term
        return acc

    def tile_stats(s_hi, s_lo, carry):
        rmax, rmin, cpos, cnn = carry
        return (jnp.maximum(rmax, fold(s_hi, jnp.max)), jnp.minimum(rmin, fold(s_lo, jnp.min)),
                cpos + fold(jnp.where(s_hi > 0.0, 1.0, 0.0), jnp.sum),
                cnn + fold(jnp.where(s_hi >= 0.0, 1.0, 0.0), jnp.sum))

    def score_body(j, carry):
        s = weighted_relu_sum()
        s_scr[j] = s
        rt_scr[...] = head_products(j + 1)
        return tile_stats(s, s, carry)

    zeros8 = jnp.zeros((SUBLANES, tq), F32)
    init = (jnp.full((SUBLANES, tq), -jnp.inf, F32), jnp.full((SUBLANES, tq), jnp.inf, F32), zeros8, zeros8)
    rt_scr[...] = head_products(0)
    carry = lax.fori_loop(0, i, score_body, init)
    s = weighted_relu_sum()
    adm = (krow >> 6) <= (qcol >> 6)
    s_adm = jnp.where(adm, s, -jnp.inf)
    s_scr[i] = s_adm
    s_scr[i + 1] = jnp.full((tk, tq), -jnp.inf, F32)
    rmax, rmin, cpos, cnn = tile_stats(s_adm, jnp.where(adm, s, jnp.inf), carry)
    rmax = spread(jnp.max(rmax, axis=0, keepdims=True))
    rmin = spread(jnp.min(rmin, axis=0, keepdims=True))
    cpos = spread(jnp.sum(cpos, axis=0, keepdims=True))
    cnn = spread(jnp.sum(cnn, axis=0, keepdims=True))

    def count(preds):
        def one(j, accs):
            t3 = s_scr[j].reshape(grp, SUBLANES, tq)
            return tuple(a + tree(jnp.where(p(t3), 1.0, 0.0), jnp.sum) for a, p in zip(accs, preds))

        def group(g, accs):
            for u in range(COUNT_UNROLL):
                accs = one(g * COUNT_UNROLL + u, accs)
            return accs

        groups = (i + 1) >> COUNT_UNROLL.bit_length() - 1
        accs = lax.fori_loop(0, groups, group, tuple(jnp.zeros((SUBLANES, tq), F32) for _ in preds))
        accs = lax.fori_loop(groups * COUNT_UNROLL, i + 1, one, accs)
        return [spread(jnp.sum(a, axis=0, keepdims=True)) for a in accs]

    n_adm = (((qcol8 >> 6) + 1 + i * (tq // CHUNK)) * CHUNK).astype(F32)
    search = n_adm > kf
    positive = cpos >= kf
    negative = cnn < kf
    lo = jnp.where(positive, TINY, jnp.where(negative, rmin, 0.0))
    clo = jnp.where(positive, cpos, jnp.where(negative, n_adm, cnn))
    st_scr[0] = jnp.where(search, lo, rmin)
    st_scr[1] = jnp.where(negative, -TINY, rmax + (jnp.abs(rmax) * (2.0 ** -10) + TINY))
    st_scr[2] = jnp.where(search, clo, n_adm)
    zero_thr = jnp.logical_and(jnp.logical_not(positive), jnp.logical_not(negative))
    st_scr[3] = jnp.where(jnp.logical_or(jnp.logical_not(search), jnp.logical_or(zero_thr, clo == kf)), 1.0, 0.0)

    def bis_cond(carry):
        it, pending = carry
        return jnp.logical_and(pending > 0.0, it < BISECT_CAP)

    def bis_step(it):
        lo, hi, clo, done = st_scr[0], st_scr[1], st_scr[2], st_scr[3]
        a = lax.bitcast_convert_type(jnp.abs(lo), jnp.int32)
        b = lax.bitcast_convert_type(jnp.abs(hi), jnp.int32)
        geo = lax.bitcast_convert_type(a + ((b - a) >> 1), F32)
        geo = jnp.where(hi > 0.0, geo, -geo)
        mid = jnp.where(it < BISECT_ARITH, lo + (hi - lo) * 0.5, geo)
        stuck = jnp.logical_or(mid <= lo, mid >= hi)
        cnt, = count([lambda t: t >= mid[None]])
        live = jnp.logical_and(done == 0.0, jnp.logical_not(stuck))
        up = jnp.logical_and(live, cnt >= kf)
        dn = jnp.logical_and(live, cnt < kf)
        lo = jnp.where(up, mid, lo)
        clo = jnp.where(up, cnt, clo)
        hi = jnp.where(dn, mid, hi)
        done = jnp.where(jnp.logical_or(stuck, clo == kf), 1.0, done)
        st_scr[0], st_scr[1], st_scr[2], st_scr[3] = lo, hi, clo, done
        return done

    def bis_body(carry):
        it, _ = carry
        bis_step(it)
        done = bis_step(it + 1)
        return it + 2, jnp.max(1.0 - done)

    lax.while_loop(bis_cond, bis_body, (jnp.int32(0), jnp.max(1.0 - st_scr[3])))
    thr = st_scr[0]
    clo = st_scr[2]

    @pl.when(jnp.max(clo) > kf)
    def _():
        cgt, = count([lambda t: t > thr[None]])
        need = kf - cgt
        tied_value = jnp.where(clo > kf, thr, jnp.nan)
        lower = (lax.broadcasted_iota(jnp.int32, (tk, tk), 1)
                 <= lax.broadcasted_iota(jnp.int32, (tk, tk), 0))
        prefix_matrix = jnp.where(lower, 1.0, 0.0).astype(BF16)

        def drop_pair(pair, seen):
            for u in range(2):
                j = 2 * pair + u
                t3 = s_scr[j].reshape(grp, SUBLANES, tq)
                tied = t3 == tied_value[None]
                ones = jnp.where(tied, 1.0, 0.0).reshape(tk, tq).astype(BF16)
                prefix = _dot(prefix_matrix, ones)
                rank = prefix.reshape(grp, SUBLANES, tq) + seen[None]
                kill = jnp.logical_and(tied, rank > need[None])
                s_scr[j] = jnp.where(kill, -jnp.inf, t3).reshape(tk, tq)
                seen = seen + spread(prefix[tk - 1:tk, :])
            return seen

        lax.fori_loop(0, (i + 2) >> 1, drop_pair, jnp.zeros((SUBLANES, tq), F32))

    hrow = lax.broadcasted_iota(jnp.int32, (LANES, tq), 0)
    for h in range(DSA_HEADS):
        slab = qt_ref[0, (h // 2) * LANES:(h // 2 + 1) * LANES, :]
        mine = (hrow >= DSA_HEAD_DIM) if (h % 2) else (hrow < DSA_HEAD_DIM)
        qm_scr[h] = jnp.where(mine, slab, jnp.zeros_like(slab))
        m_scr[h] = jnp.full((SUBLANES, tq), NEG_BIG, F32)
        acc_scr[h] = jnp.zeros((ACC_ROWS, tq), F32)
    ones_rows = jnp.ones((ACC_ROWS - LANES, tk), BF16)

    def slab(h):
        return slice((h // 2) * LANES, (h // 2 + 1) * LANES)

    def tile_operands(j):
        t3 = s_scr[j].reshape(grp, SUBLANES, tq)
        bias = jnp.where(t3 >= thr[None], 0.0, NEG_BIG).reshape(tk, tq)
        return k_ref[0, pl.ds(pl.multiple_of(j * tk, tk), tk), :], bias

    def masked_scores(keys, bias, h):
        return _dot(keys[:, slab(h)], qm_scr[h]) + bias

    keys0, bias0 = tile_operands(0)
    for h in range(DSA_HEADS):
        sn_scr[h] = masked_scores(keys0, bias0, h)

    def att_body(j, _):
        keys_n, bias_n = tile_operands(jnp.minimum(j + 1, i))
        for h in range(DSA_HEADS):
            s3 = sn_scr[h].reshape(grp, SUBLANES, tq)
            m_prev = m_scr[h]
            m_new = jnp.maximum(m_prev, spread(jnp.max(tree(s3, jnp.max), axis=0, keepdims=True)))
            p = jnp.exp2((s3 - m_new[None]).reshape(tk, tq).astype(BF16))
            corr = jnp.exp2(m_prev - m_new)
            pv = _dot(jnp.concatenate([vt_ref[0, j, slab(h), :], ones_rows], axis=0), p)
            acc = acc_scr[h].reshape(ACC_ROWS // SUBLANES, SUBLANES, tq) * corr[None]
            acc_scr[h] = acc.reshape(ACC_ROWS, tq) + pv
            m_scr[h] = m_new
            sn_scr[h] = masked_scores(keys_n, bias_n, h)
        return 0

    lax.fori_loop(0, i + 1, att_body, 0)

    def normalised(h):
        inv = 1.0 / acc_scr[h, LANES:LANES + SUBLANES, :]
        return (acc_scr[h, 0:LANES, :].reshape(LANES // SUBLANES, SUBLANES, tq) * inv[None]).reshape(LANES, tq)

    for sp in range(DSA_HEADS // 2):
        pair = jnp.where(hrow < DSA_HEAD_DIM, normalised(2 * sp), normalised(2 * sp + 1))
        o_ref[0, sp * LANES:(sp + 1) * LANES, :] = pair.astype(o_ref.dtype)


def _dsa(iqt, ik, iwt, qt, k, vt):
    B, T, _ = k.shape
    tq = DSA_TILE
    nq = T // tq
    topk = min(DSA_TOPK_MAX, T // 4)
    kern = functools.partial(_dsa_kernel, topk=topk)
    return pl.pallas_call(
        kern,
        grid=(B, nq),
        in_specs=[
            pl.BlockSpec((1, 1, IDX_DIM, IDX_HEADS * tq), lambda b, i: (b, i, 0, 0)),
            pl.BlockSpec((1, T, IDX_DIM), lambda b, i: (b, 0, 0)),
            pl.BlockSpec((1, IDX_HEADS, tq), lambda b, i: (b, 0, i)),
            pl.BlockSpec((1, DSA_WIDTH, tq), lambda b, i: (b, 0, i)),
            pl.BlockSpec((1, T, DSA_WIDTH), lambda b, i: (b, 0, 0)),
            pl.BlockSpec((1, nq, DSA_WIDTH, tq), lambda b, i: (b, 0, 0, 0)),
        ],
        out_specs=pl.BlockSpec((1, DSA_WIDTH, tq), lambda b, i: (b, 0, i)),
        out_shape=jax.ShapeDtypeStruct((B, DSA_WIDTH, T), BF16),
        scratch_shapes=[
            pltpu.VMEM((nq + 1, tq, tq), F32),
            pltpu.VMEM((DSA_HEADS, LANES, tq), BF16),
            pltpu.VMEM((4, SUBLANES, tq), F32),
            pltpu.VMEM((DSA_HEADS, SUBLANES, tq), F32),
            pltpu.VMEM((DSA_HEADS, ACC_ROWS, tq), F32),
            pltpu.VMEM((DSA_HEADS, tq, tq), F32),
            pltpu.VMEM((tq, IDX_HEADS * tq), F32),
        ],
        compiler_params=_params(("parallel", "arbitrary")),
    )(iqt, ik, iwt, qt, k, vt)


def _gla_kernel(q_ref, k_ref, v_ref, r_ref, la_ref, g_ref, o_ref, s_scr):
    rows = GLA_BLOCK
    nch = rows // CHUNK
    npair = GLA_HEADS // 2
    nsub = q_ref.shape[1] // rows

    @pl.when(pl.program_id(1) == 0)
    def _():
        s_scr[...] = jnp.zeros_like(s_scr)

    ri = lax.broadcasted_iota(jnp.int32, (rows, rows), 0)
    ci = lax.broadcasted_iota(jnp.int32, (rows, rows), 1)
    causal = jnp.logical_and((ri >> 6) == (ci >> 6), ci <= ri)
    tri = jnp.where(causal, 1.0, 0.0).astype(BF16)
    lane = lax.broadcasted_iota(jnp.int32, (rows, LANES), 1)
    eye = lax.broadcasted_iota(jnp.int32, (LANES, LANES), 0) == lax.broadcasted_iota(jnp.int32, (LANES, LANES), 1)
    top_rows = lax.broadcasted_iota(jnp.int32, (LANES, GLA_DV), 0) < GLA_DK

    def decayed(r0):
        la = la_ref[0, r0:r0 + rows, :]
        la_hi = la.astype(BF16)
        rem = la - la_hi.astype(F32)
        la_mid = rem.astype(BF16)
        la_lo = (rem - la_mid.astype(F32)).astype(BF16)
        b = _dot(tri, la_hi) + _dot(tri, la_mid) + _dot(tri, la_lo)
        b_last = jnp.concatenate(
            [jnp.broadcast_to(b[(c + 1) * CHUNK - 1:(c + 1) * CHUNK, :], (CHUNK, b.shape[1])) for c in range(nch)],
            axis=0)
        q = q_ref[0, r0:r0 + rows, :]
        k = k_ref[0, r0:r0 + rows, :]
        return q * jnp.exp(b), (k * jnp.exp(-b)).astype(BF16), (k * jnp.exp(b_last - b)).astype(BF16), jnp.exp(b_last)

    subs = [decayed(sb * rows) for sb in range(nsub)]

    for pr in range(npair):
        sl = slice(pr * LANES, (pr + 1) * LANES)
        heads = (2 * pr, 2 * pr + 1)
        state = s_scr[pr]
        for sb in range(nsub):
            r0 = sb * rows
            qg, kg, kd, decay_rows = subs[sb]
            qg_p = qg[:, sl]
            kg_p = kg[:, sl]
            kd_p = kd[:, sl]
            qg_h = [jnp.where(lane < GLA_DK, qg_p, 0.0).astype(BF16), jnp.where(lane >= GLA_DK, qg_p, 0.0).astype(BF16)]
            v_h = [v_ref[0, r0:r0 + rows, h * GLA_DV:(h + 1) * GLA_DV].astype(BF16) for h in heads]
            o_h = []
            for t in range(2):
                a = jnp.where(causal, _dot_nt(qg_h[t], kg_p), 0.0)
                o_h.append(_dot(a.astype(BF16), v_h[t]))
            inter = [[], []]
            for c in range(nch):
                rs = slice(c * CHUNK, (c + 1) * CHUNK)
                state_b = state.astype(BF16)
                for t in range(2):
                    inter[t].append(_dot(qg_h[t][rs], state_b))
                kv = jnp.where(top_rows, _dot_tn(kd_p[rs], v_h[0][rs]), _dot_tn(kd_p[rs], v_h[1][rs]))
                drow = jnp.broadcast_to(decay_rows[c * CHUNK:c * CHUNK + 1, sl], (LANES, LANES))
                dcol = jnp.sum(jnp.where(eye, drow, 0.0), axis=1, keepdims=True)
                state = dcol * state + kv
            for t in range(2):
                h = heads[t]
                o = o_h[t] + jnp.concatenate(inter[t], axis=0)
                ms = jnp.mean(o * o, axis=1, keepdims=True)
                on = o * lax.rsqrt(ms + LN_EPS) * g_ref[...]
                r = r_ref[0, r0:r0 + rows, h * GLA_DV:(h + 1) * GLA_DV]
                o_ref[0, r0:r0 + rows, h * GLA_DV:(h + 1) * GLA_DV] = (on * (r / (1.0 + jnp.exp(-r)))).astype(o_ref.dtype)
        s_scr[pr] = state


def _gla(bq, bk, bv, br, la, g):
    B, T, _ = bq.shape
    rows = GLA_STEP
    blk = lambda w: pl.BlockSpec((1, rows, w), lambda b, i: (b, i, 0))
    return pl.pallas_call(
        _gla_kernel,
        grid=(B, T // rows),
        in_specs=[blk(256), blk(256), blk(512), blk(512), blk(256), pl.BlockSpec((1, GLA_DV), lambda b, i: (0, 0))],
        out_specs=blk(GLA_WIDTH),
        out_shape=jax.ShapeDtypeStruct((B, T, GLA_WIDTH), BF16),
        scratch_shapes=[pltpu.VMEM((GLA_HEADS // 2, 2 * GLA_DK, GLA_DV), F32)],
        compiler_params=_params(("parallel", "arbitrary")),
    )(bq, bk, bv, br, la, g)


def _layer_norm(x, g, b):
    mu = jnp.mean(x, axis=1, keepdims=True)
    xc = x - mu
    var = jnp.mean(xc * xc, axis=1, keepdims=True)
    return xc * lax.rsqrt(var + LN_EPS) * g + b


def _out_proj_kernel(x_ref, ya_ref, yb_ref, wo_ref, g_ref, b_ref, wr_hi_ref, wr_lo_ref, br_ref, h_ref, route_ref):
    tm = x_ref.shape[0]
    mix = _dot(ya_ref[...], wo_ref[0:DSA_WIDTH, :]) + _dot(yb_ref[...], wo_ref[DSA_WIDTH:D_MODEL, :])
    h = _layer_norm(DEEPNORM_ALPHA * x_ref[...] + mix, g_ref[...], b_ref[...])
    h_ref[...] = h

    h_hi = h.astype(BF16)
    h_lo = (h - h_hi.astype(F32)).astype(BF16)
    logits = _dot(h_hi, wr_hi_ref[...]) + _dot(h_lo, wr_hi_ref[...]) + _dot(h_hi, wr_lo_ref[...]) + br_ref[...]

    lane = lax.broadcasted_iota(jnp.int32, (tm, LANES), 1)
    lanef = lane.astype(F32)
    gl = jnp.where(lane < N_GROUPS, logits, -jnp.inf)
    gmax = jnp.max(gl, axis=1, keepdims=True)
    gsel = jnp.min(jnp.where(gl == gmax, lanef, 1e9), axis=1, keepdims=True)
    pg = 1.0 / jnp.sum(jnp.exp(gl - gmax), axis=1, keepdims=True)
    egrp = ((lane - N_GROUPS) >> 3).astype(F32)
    in_grp = jnp.logical_and(jnp.logical_and(lane >= N_GROUPS, lane < N_GROUPS + N_EXPERTS), egrp == gsel)
    el = jnp.where(in_grp, logits, -jnp.inf)
    t1 = jnp.max(el, axis=1, keepdims=True)
    i1 = jnp.min(jnp.where(el == t1, lanef, 1e9), axis=1, keepdims=True)
    el2 = jnp.where(lanef == i1, -jnp.inf, el)
    t2 = jnp.max(el2, axis=1, keepdims=True)
    i2 = jnp.min(jnp.where(el2 == t2, lanef, 1e9), axis=1, keepdims=True)
    e21 = jnp.exp(t2 - t1)
    g1 = pg / (1.0 + e21)
    g2 = pg * e21 / (1.0 + e21)
    route = jnp.where(lane == 0, i1 - N_GROUPS, 0.0)
    route = jnp.where(lane == 1, i2 - N_GROUPS, route)
    route = jnp.where(lane == 2, g1, route)
    route = jnp.where(lane == 3, g2, route)
    route_ref[...] = route


def _out_proj(x2, ya, yb, wo, g1, b1, wr_hi, wr_lo, br):
    N = x2.shape[0]
    tm = TOK_TILE
    row = lambda i: (i, 0)
    const = lambda i: (0, 0)
    return pl.pallas_call(
        _out_proj_kernel,
        grid=(N // tm,),
        in_specs=[
            pl.BlockSpec((tm, D_MODEL), row), pl.BlockSpec((tm, DSA_WIDTH), row), pl.BlockSpec((tm, GLA_WIDTH), row),
            pl.BlockSpec(wo.shape, const), pl.BlockSpec(g1.shape, const), pl.BlockSpec(b1.shape, const),
            pl.BlockSpec(wr_hi.shape, const), pl.BlockSpec(wr_lo.shape, const), pl.BlockSpec(br.shape, const),
        ],
        out_specs=[pl.BlockSpec((tm, D_MODEL), row), pl.BlockSpec((tm, LANES), row)],
        out_shape=[jax.ShapeDtypeStruct((N, D_MODEL), F32), jax.ShapeDtypeStruct((N, LANES), F32)],
        compiler_params=_params(("parallel",)),
    )(x2, ya, yb, wo, g1, b1, wr_hi, wr_lo, br)


def _rank_kernel(route_ref, rank_ref, cnt_ref, carry_scr):
    tm = route_ref.shape[0]

    @pl.when(pl.program_id(0) == 0)
    def _():
        carry_scr[...] = jnp.zeros_like(carry_scr)

    route = route_ref[...]
    lanef = lax.broadcasted_iota(jnp.int32, (tm, LANES), 1).astype(F32)
    e1 = route[:, 0:1]
    e2 = route[:, 1:2]
    hit1 = lanef == e1
    hit2 = lanef == e2
    onehot = jnp.where(jnp.logical_or(hit1, hit2), 1.0, 0.0).astype(BF16)
    ri = lax.broadcasted_iota(jnp.int32, (tm, tm), 0)
    ci = lax.broadcasted_iota(jnp.int32, (tm, tm), 1)
    before = jnp.where(ci < ri, 1.0, 0.0).astype(BF16)
    prefix = _dot(before, onehot) + carry_scr[0:1, :]
    r1 = jnp.sum(jnp.where(hit1, prefix, 0.0), axis=1, keepdims=True)
    r2 = jnp.sum(jnp.where(hit2, prefix, 0.0), axis=1, keepdims=True)
    rank_ref[...] = jnp.where(lanef == 0.0, r1, jnp.where(lanef == 1.0, r2, 0.0))
    total = _dot(jnp.ones((8, tm), BF16), onehot)
    carry_scr[...] = carry_scr[...] + total
    cnt_ref[...] = carry_scr[...]


def _rank(route):
    N = route.shape[0]
    tm = RANK_TILE
    return pl.pallas_call(
        _rank_kernel,
        grid=(N // tm,),
        in_specs=[pl.BlockSpec((tm, LANES), lambda i: (i, 0))],
        out_specs=[pl.BlockSpec((tm, LANES), lambda i: (i, 0)), pl.BlockSpec((8, LANES), lambda i: (0, 0))],
        out_shape=[jax.ShapeDtypeStruct((N, LANES), F32), jax.ShapeDtypeStruct((8, LANES), F32)],
        scratch_shapes=[pltpu.VMEM((8, LANES), F32)],
        compiler_params=_params(("arbitrary",)),
    )(route)


def _dispatch_kernel(pos_ref, h_ref, xs_in_ref, xs_ref, hbuf, pbuf, load_sem, row_sem):
    del xs_in_ref
    tm = DISPATCH_TILE
    nbuf = hbuf.shape[0]
    i = pl.program_id(0)
    last = pl.num_programs(0) - 1

    def tile_load(step):
        return pltpu.make_async_copy(h_ref.at[pl.ds(step * tm, tm), :], hbuf.at[step % nbuf], load_sem.at[step % nbuf])

    def row_copy(step, r, slot):
        p = pos_ref[2 * (step * tm + r) + slot]
        return pltpu.make_async_copy(pbuf.at[step % nbuf, pl.ds(r, 1), :], xs_ref.at[pl.ds(p, 1), :],
                                     row_sem.at[step % nbuf])

    def for_rows(fn):
        def body(r, _):
            fn(r, 0)
            fn(r, 1)
            return 0
        lax.fori_loop(0, tm, body, 0, unroll=DMA_UNROLL)

    @pl.when(i == 0)
    def _():
        tile_load(0).start()

    @pl.when(i < last)
    def _():
        tile_load(i + 1).start()

    tile_load(i).wait()
    pbuf[i % nbuf] = _pack_halves(hbuf[i % nbuf])
    for_rows(lambda r, slot: row_copy(i, r, slot).start(priority=slot))

    @pl.when(i > 0)
    def _():
        for_rows(lambda r, slot: row_copy(i - 1, r, slot).wait())

    @pl.when(i == last)
    def _():
        for_rows(lambda r, slot: row_copy(i, r, slot).wait())


def _dispatch(pos_flat, h, xs_init):
    N = h.shape[0]
    grid_spec = pltpu.PrefetchScalarGridSpec(
        num_scalar_prefetch=1,
        grid=(N // DISPATCH_TILE,),
        in_specs=[pl.BlockSpec(memory_space=pl.ANY), pl.BlockSpec(memory_space=pl.ANY)],
        out_specs=pl.BlockSpec(memory_space=pl.ANY),
        scratch_shapes=[pltpu.VMEM((3, DISPATCH_TILE, D_MODEL), F32),
                        pltpu.VMEM((3, DISPATCH_TILE, D_MODEL // 2), jnp.uint32),
                        pltpu.SemaphoreType.DMA((3,)), pltpu.SemaphoreType.DMA((3,))],
    )
    return pl.pallas_call(
        _dispatch_kernel,
        grid_spec=grid_spec,
        out_shape=jax.ShapeDtypeStruct(xs_init.shape, xs_init.dtype),
        input_output_aliases={2: 0},
        compiler_params=_params(("arbitrary",)),
    )(pos_flat, h, xs_init)


def _ffn_kernel(te_ref, na_ref, x_ref, wi_ref, wo_ref, y_ref, wi_bf, wo_bf):
    i = pl.program_id(0)
    active = i < na_ref[0]
    fresh = jnp.logical_or(i == 0, te_ref[i] != te_ref[jnp.maximum(i - 1, 0)])

    @pl.when(jnp.logical_and(active, fresh))
    def _():
        wi_bf[...] = wi_ref[0].astype(BF16)
        wo_bf[...] = wo_ref[0].astype(BF16)

    @pl.when(active)
    def _():
        x_left, x_right = _unpack_halves(x_ref[...])
        half = D_MODEL // 2
        hid = (_dot(x_left.astype(BF16), wi_bf[0:half, :])
               + _dot(x_right.astype(BF16), wi_bf[half:D_MODEL, :]))
        hg = hid[:, :D_EXPERT]
        hu = hid[:, D_EXPERT:]
        act = (hg / (1.0 + jnp.exp(-hg))) * hu
        y_ref[...] = _pack_halves(_dot(act.astype(BF16), wo_bf[...]))

    @pl.when(jnp.logical_not(active))
    def _():
        y_ref[...] = jnp.zeros_like(y_ref)


def _ffn(tile_expert, n_active, xs, w_e_in, w_e_out):
    R = xs.shape[0]
    tm = MOE_TILE

    def live(i, te, na):
        return jnp.minimum(i, na[0] - 1)

    grid_spec = pltpu.PrefetchScalarGridSpec(
        num_scalar_prefetch=2,
        grid=(R // tm,),
        in_specs=[
            pl.BlockSpec((tm, D_MODEL // 2), lambda i, te, na: (live(i, te, na), 0)),
            pl.BlockSpec((1, D_MODEL, 2 * D_EXPERT), lambda i, te, na: (te[live(i, te, na)], 0, 0)),
            pl.BlockSpec((1, D_EXPERT, D_MODEL), lambda i, te, na: (te[live(i, te, na)], 0, 0)),
        ],
        out_specs=pl.BlockSpec((tm, D_MODEL // 2), lambda i, te, na: (i, 0)),
        scratch_shapes=[pltpu.VMEM((D_MODEL, 2 * D_EXPERT), BF16), pltpu.VMEM((D_EXPERT, D_MODEL), BF16)],
    )
    return pl.pallas_call(
        _ffn_kernel,
        grid_spec=grid_spec,
        out_shape=jax.ShapeDtypeStruct((R, D_MODEL // 2), jnp.uint32),
        compiler_params=_params(("arbitrary",)),
    )(tile_expert, n_active, xs, w_e_in, w_e_out)


def _combine_kernel(pos_ref, h_ref, route_ref, g_ref, b_ref, ys_ref, o_ref, buf, sem):
    tm = h_ref.shape[0]
    i = pl.program_id(0)
    cur = i % 2

    def row_copy(step, r, slot):
        p = pos_ref[2 * (step * tm + r) + slot]
        half = step % 2
        return pltpu.make_async_copy(ys_ref.at[pl.ds(p, 1), :], buf.at[half, slot, pl.ds(r, 1), :], sem.at[half])

    def for_rows(fn):
        def body(r, _):
            fn(r, 0)
            fn(r, 1)
            return 0
        lax.fori_loop(0, tm, body, 0, unroll=DMA_UNROLL)

    @pl.when(i == 0)
    def _():
        for_rows(lambda r, slot: row_copy(0, r, slot).start(priority=slot))

    @pl.when(i + 1 < pl.num_programs(0))
    def _():
        for_rows(lambda r, slot: row_copy(i + 1, r, slot).start(priority=slot))

    for_rows(lambda r, slot: row_copy(i, r, slot).wait())
    route = route_ref[...]
    l0, r0 = _unpack_halves(buf[cur, 0])
    l1, r1 = _unpack_halves(buf[cur, 1])
    g0, g1 = route[:, 2:3], route[:, 3:4]
    ffn = jnp.concatenate([l0 * g0 + l1 * g1, r0 * g0 + r1 * g1], axis=1)
    o_ref[...] = _layer_norm(DEEPNORM_ALPHA * h_ref[...] + ffn, g_ref[...], b_ref[...])


def _combine(pos_flat, h, route, g2, b2, ys):
    N = h.shape[0]
    tm = TOK_TILE
    row = lambda i, pos: (i, 0)
    const = lambda i, pos: (0, 0)
    grid_spec = pltpu.PrefetchScalarGridSpec(
        num_scalar_prefetch=1,
        grid=(N // tm,),
        in_specs=[
            pl.BlockSpec((tm, D_MODEL), row), pl.BlockSpec((tm, LANES), row),
            pl.BlockSpec(g2.shape, const), pl.BlockSpec(b2.shape, const),
            pl.BlockSpec(memory_space=pl.ANY),
        ],
        out_specs=pl.BlockSpec((tm, D_MODEL), row),
        scratch_shapes=[pltpu.VMEM((2, 2, tm, D_MODEL // 2), jnp.uint32), pltpu.SemaphoreType.DMA((2,))],
    )
    return pl.pallas_call(
        _combine_kernel,
        grid_spec=grid_spec,
        out_shape=jax.ShapeDtypeStruct((N, D_MODEL), F32),
        compiler_params=_params(("arbitrary",)),
    )(pos_flat, h, route, g2, b2, ys)


def _rope_tables(T, dim):
    half = dim // 2
    inv = 1.0 / (ROPE_THETA ** (jnp.arange(half, dtype=F32) / half))
    ang = jnp.arange(T).astype(F32)[:, None] * inv[None, :]
    cos = jnp.cos(ang)
    sin = jnp.sin(ang)
    reps = LANES // dim
    cos_t = jnp.tile(jnp.concatenate([cos, cos], axis=1), (1, reps))
    sin_t = jnp.tile(jnp.concatenate([-sin, sin], axis=1), (1, reps))
    return cos_t, sin_t


def _pad_cols(w, width):
    return jnp.pad(w, ((0, 0), (0, width - w.shape[1])))


def _layer(x, w_in, w_gla_gate, b_gla_gate, g_gla_norm, w_out, ln1_g, ln1_b,
           w_gr, b_gr, w_er, b_er, w_e_in, w_e_out, ln2_g, ln2_b):
    B, T, D = x.shape
    N = B * T
    assert D == D_MODEL and T % DSA_TILE == 0 and N % DISPATCH_TILE == 0 and DISPATCH_TILE % RANK_TILE == 0
    x2 = x.reshape(N, D)

    sizes = (512, 512, 512, 256, 32, 8, 256, 256, 512, 512, 16)
    offs = np.concatenate([[0], np.cumsum(sizes)])
    col = lambda k: w_in[:, offs[k]:offs[k + 1]]
    wa = jnp.concatenate([col(0) * (DSA_HEAD_DIM ** -0.5 * LOG2E), col(1), col(2)], axis=1).astype(BF16)
    wi = _pad_cols(jnp.concatenate([col(3), col(4), col(5) * IDX_SCALE], axis=1), 3 * LANES).astype(BF16)
    wb = jnp.concatenate([col(6) * (GLA_DK ** -0.5), col(7), col(8), col(9)], axis=1).astype(BF16)
    wg = _pad_cols(col(10), LANES).astype(BF16)
    wgate = jnp.pad(w_gla_gate, ((0, LANES - GLA_GATE_RANK), (0, 0))).astype(BF16)
    bgate = b_gla_gate.reshape(1, -1)
    cosa, sina = _rope_tables(T, DSA_HEAD_DIM)
    cosi, sini = _rope_tables(T, IDX_DIM)

    q, k, v, iq, ikw, bq, bk, bv, br, la = _in_proj(x2, wa, wi, wb, wg, wgate, bgate, cosa, sina, cosi, sini, T)

    nq = T // DSA_TILE
    iqt = iq.reshape(B, nq, DSA_TILE, IDX_HEADS, IDX_DIM).transpose(0, 1, 4, 3, 2)
    iqt = iqt.reshape(B, nq, IDX_DIM, IDX_HEADS * DSA_TILE)
    ik = ikw[:, :IDX_DIM].astype(BF16).reshape(B, T, IDX_DIM)
    iwt = ikw[:, IDX_DIM:IDX_DIM + IDX_HEADS].reshape(B, T, IDX_HEADS).transpose(0, 2, 1)
    qt = q.reshape(B, T, DSA_WIDTH).transpose(0, 2, 1)
    vt = v.reshape(B, nq, DSA_TILE, DSA_WIDTH).transpose(0, 1, 3, 2)
    ya = _dsa(iqt, ik, iwt, qt, k.reshape(B, T, DSA_WIDTH), vt).transpose(0, 2, 1)

    r3 = lambda a: a.reshape(B, T, a.shape[-1])
    yb = _gla(r3(bq), r3(bk), r3(bv), r3(br), r3(la), g_gla_norm.reshape(1, GLA_DV))

    wr = _pad_cols(jnp.concatenate([w_gr, w_er], axis=1), LANES)
    wr_hi = wr.astype(BF16)
    wr_lo = (wr - wr_hi.astype(F32)).astype(BF16)
    brt = _pad_cols(jnp.concatenate([b_gr, b_er]).reshape(1, -1), LANES)
    h, route = _out_proj(x2, ya.reshape(N, DSA_WIDTH), yb.reshape(N, GLA_WIDTH), w_out.astype(BF16),
                         ln1_g.reshape(1, D), ln1_b.reshape(1, D), wr_hi, wr_lo, brt)

    rank, cnt = _rank(route)
    counts = cnt[0, :N_EXPERTS].astype(jnp.int32)
    padded = ((counts + MOE_TILE - 1) // MOE_TILE) * MOE_TILE
    ends = jnp.cumsum(padded)
    starts = ends - padded
    eid = route[:, 0:2].astype(jnp.int32)
    pos = (starts[eid] + rank[:, 0:2].astype(jnp.int32)).reshape(-1)
    n_rows = 2 * N + N_EXPERTS * MOE_TILE
    n_tiles = n_rows // MOE_TILE
    tile_start = jnp.arange(n_tiles, dtype=jnp.int32) * MOE_TILE
    tile_expert = jnp.minimum(jnp.sum(tile_start[:, None] >= ends[None, :], axis=1), N_EXPERTS - 1).astype(jnp.int32)
    n_active = (ends[-1] // MOE_TILE).astype(jnp.int32).reshape(1)

    xs = _dispatch(pos, h, jnp.zeros((n_rows, D // 2), jnp.uint32))
    ys = _ffn(tile_expert, n_active, xs, w_e_in, w_e_out)
    out = _combine(pos, h, route, ln2_g.reshape(1, D), ln2_b.reshape(1, D), ys)
    return out.reshape(B, T, D)


def kernel(x, w_in, w_gla_gate, b_gla_gate, g_gla_norm, w_out, ln1_g, ln1_b, w_group_router, b_group_router,
           w_expert_router, b_expert_router, w_expert_in, w_expert_out, ln2_g, ln2_b):
    h = x
    for l in range(w_in.shape[0]):
        h = _layer(h, w_in[l], w_gla_gate[l], b_gla_gate[l], g_gla_norm[l], w_out[l], ln1_g[l], ln1_b[l],
                   w_group_router[l], b_group_router[l], w_expert_router[l], b_expert_router[l],
                   w_expert_in[l], w_expert_out[l], ln2_g[l], ln2_b[l])
    return h
```
